```python
import math
import jax, jax.numpy as jnp
from jax import lax
import numpy as np

D_MODEL = 2048
BATCH = 2
SEQ = 16384
DEPTH = 2

CHUNK = 64
N_MIXERS = 2
HEAD_DIM = 128
MIX_WIDTH = D_MODEL
MEM_LEN = 256
MEM_HEADS = 4
MEM_WIDTH = MEM_HEADS * HEAD_DIM
POOL_WIDTH = MIX_WIDTH - MEM_WIDTH
POOL_GROUPS = 4
POOL_GROUP_WIDTH = POOL_WIDTH // POOL_GROUPS
POOL_WINDOWS = (2, 4, 8, 16)
FOX_HEADS = POOL_WIDTH // HEAD_DIM
FOX_WIDTH = FOX_HEADS * HEAD_DIM
Q_BLOCK = 128
FFN_HIDDEN = 4 * D_MODEL
RMS_EPS = 1e-6
FORGET_BIAS_INIT = 2.0
N_POOL_LAYERS = (DEPTH + 1) // 2
N_FOX_LAYERS = DEPTH // 2
POOL_IN_COLS = POOL_WIDTH + MEM_WIDTH
FOX_IN_COLS = 3 * FOX_WIDTH + FOX_HEADS + MEM_WIDTH

kernel_name = "interleaved_pool_fox_memxattn_trunk"


def _rmsnorm(x, gain):
    x32 = x.astype(jnp.float32)
    y = x32 * lax.rsqrt(jnp.mean(x32 * x32, axis=-1, keepdims=True) + RMS_EPS)
    return (y * gain.astype(jnp.float32)).astype(x.dtype)


def _pool_mixer(u, w_grp, scale):
    B, S, _ = u.shape
    ug = u.reshape(B, S, POOL_GROUPS, POOL_GROUP_WIDTH).astype(jnp.float32)
    csum = jnp.pad(jnp.cumsum(ug, axis=1), ((0, 0), (1, 0), (0, 0), (0, 0)))
    t = jnp.arange(S)
    outs = []
    for g, w in enumerate(POOL_WINDOWS):
        lo = jnp.maximum(t + 1 - w, 0)
        window_sum = csum[:, 1:, g] - csum[:, lo, g]
        count = jnp.minimum(t + 1, w).astype(jnp.float32)[None, :, None]
        outs.append(window_sum / count - ug[:, :, g])
    pooled = jnp.stack(outs, axis=2).astype(u.dtype)
    mixed = jnp.einsum('bsgc,gcd->bsgd', pooled, w_grp).reshape(B, S, POOL_WIDTH)
    return mixed * scale


def _forgetting_attention(q, k, v, log_f):
    B, S, H, Dh = q.shape
    nb = S // Q_BLOCK
    fcum = jnp.cumsum(log_f, axis=1)
    fk = jnp.transpose(fcum, (0, 2, 1))
    kpos = jnp.arange(S)
    scale = 1.0 / math.sqrt(Dh)
    qb = jnp.transpose(q.reshape(B, nb, Q_BLOCK, H, Dh), (1, 0, 2, 3, 4))
    fqb = jnp.transpose(fcum.reshape(B, nb, Q_BLOCK, H), (1, 0, 3, 2))

    def one_block(args):
        qblk, fq, bi = args
        qpos = bi * Q_BLOCK + jnp.arange(Q_BLOCK)
        logits = jnp.einsum('bqhd,bkhd->bhqk', qblk, k).astype(jnp.float32) * scale
        logits = logits + fq[..., None] - fk[:, :, None, :]
        mask = kpos[None, :] <= qpos[:, None]
        logits = jnp.where(mask[None, None], logits, -jnp.inf)
        p = jax.nn.softmax(logits, axis=-1)
        return jnp.einsum('bhqk,bkhd->bqhd', p.astype(v.dtype), v)

    out = lax.map(one_block, (qb, fqb, jnp.arange(nb)))
    return jnp.transpose(out, (1, 0, 2, 3, 4)).reshape(B, S, H * Dh)


def _memory_attention(q_mem, mem_n, w_mem_kv):
    B, S, _ = q_mem.shape
    kv = mem_n @ w_mem_kv
    k, v = jnp.split(kv, 2, axis=-1)
    k = k.reshape(B, MEM_LEN, MEM_HEADS, HEAD_DIM)
    v = v.reshape(B, MEM_LEN, MEM_HEADS, HEAD_DIM)
    q = q_mem.reshape(B, S, MEM_HEADS, HEAD_DIM)
    logits = jnp.einsum('bshd,bmhd->bhsm', q, k).astype(jnp.float32) / math.sqrt(HEAD_DIM)
    p = jax.nn.softmax(logits, axis=-1)
    out = jnp.einsum('bhsm,bmhd->bshd', p.astype(v.dtype), v)
    return out.reshape(B, S, MEM_WIDTH)


def setup_inputs(seed: int = 0) -> dict:
    key = jax.random.key(seed)
    ks = jax.random.split(key, 16)
    f32 = jnp.float32

    def w(k, shape, fan_in):
        return jax.random.normal(k, shape, f32) * (fan_in ** -0.5)

    def gain(k, shape):
        return 1.0 + 0.02 * jax.random.normal(k, shape, f32)

    return {
        "x": jax.random.normal(ks[0], (BATCH, SEQ, D_MODEL), f32),
        "mem": jax.random.normal(ks[1], (BATCH, MEM_LEN, D_MODEL), f32),
        "norm_mix": gain(ks[2], (DEPTH, D_MODEL)),
        "norm_mem": gain(ks[3], (DEPTH, D_MODEL)),
        "pool_w_in": w(ks[4], (N_POOL_LAYERS, D_MODEL, POOL_IN_COLS), D_MODEL),
        "pool_w_grp": w(ks[5], (N_POOL_LAYERS, POOL_GROUPS, POOL_GROUP_WIDTH, POOL_GROUP_WIDTH), POOL_GROUP_WIDTH),
        "pool_scale": 1.0 + 0.1 * jax.random.normal(ks[6], (N_POOL_LAYERS, POOL_WIDTH), f32),
        "fox_w_in": w(ks[7], (N_FOX_LAYERS, D_MODEL, FOX_IN_COLS), D_MODEL),
        "fox_b_f": FORGET_BIAS_INIT + 0.1 * jax.random.normal(ks[8], (N_FOX_LAYERS, FOX_HEADS), f32),
        "w_mem_kv": w(ks[9], (DEPTH, D_MODEL, 2 * MEM_WIDTH), D_MODEL),
        "w_out": w(ks[10], (DEPTH, MIX_WIDTH, D_MODEL), MIX_WIDTH),
        "norm_ffn": gain(ks[11], (DEPTH, D_MODEL)),
        "w_ffn1": w(ks[12], (DEPTH, D_MODEL, FFN_HIDDEN), D_MODEL),
        "w_ffn2": w(ks[13], (DEPTH, FFN_HIDDEN, D_MODEL), FFN_HIDDEN),
        "norm_final": gain(ks[14], (D_MODEL,)),
    }


def reference(x, mem, norm_mix, norm_mem, pool_w_in, pool_w_grp, pool_scale, fox_w_in, fox_b_f,
              w_mem_kv, w_out, norm_ffn, w_ffn1, w_ffn2, norm_final):
    B, S, _ = x.shape
    for i in range(DEPTH):
        j = i // N_MIXERS
        h = _rmsnorm(x, norm_mix[i])
        mem_n = _rmsnorm(mem, norm_mem[i])
        if i % N_MIXERS == 0:
            proj = h @ pool_w_in[j]
            u, q_mem = jnp.split(proj, [POOL_WIDTH], axis=-1)
            y_mix = _pool_mixer(u, pool_w_grp[j], pool_scale[j])
        else:
            proj = h @ fox_w_in[j]
            q, k, v, f_logit, q_mem = jnp.split(
                proj, [FOX_WIDTH, 2 * FOX_WIDTH, 3 * FOX_WIDTH, 3 * FOX_WIDTH + FOX_HEADS], axis=-1)
            log_f = jax.nn.log_sigmoid(f_logit.astype(jnp.float32) + fox_b_f[j].astype(jnp.float32))
            y_mix = _forgetting_attention(
                q.reshape(B, S, FOX_HEADS, HEAD_DIM),
                k.reshape(B, S, FOX_HEADS, HEAD_DIM),
                v.reshape(B, S, FOX_HEADS, HEAD_DIM),
                log_f)
        y_mem = _memory_attention(q_mem, mem_n, w_mem_kv[i])
        x = x + jnp.concatenate([y_mix, y_mem], axis=-1) @ w_out[i]
        h = _rmsnorm(x, norm_ffn[i])
        x = x + jnp.square(jax.nn.relu(h @ w_ffn1[i])) @ w_ffn2[i]
    return _rmsnorm(x, norm_final)
```

```python
import functools
import math

import jax
import jax.numpy as jnp
from jax import lax
from jax.experimental import pallas as pl
from jax.experimental.pallas import tpu as pltpu

D_MODEL = 2048
HEAD_DIM = 128
MEM_LEN = 256
MEM_HEADS = 4
MEM_WIDTH = MEM_HEADS * HEAD_DIM
POOL_WIDTH = D_MODEL - MEM_WIDTH
POOL_GROUPS = 4
POOL_GROUP_WIDTH = POOL_WIDTH // POOL_GROUPS
POOL_WINDOWS = (2, 4, 8, 16)
POOL_HALO = 16
FOX_HEADS = POOL_WIDTH // HEAD_DIM
FOX_WIDTH = FOX_HEADS * HEAD_DIM
FFN_HIDDEN = 4 * D_MODEL
RMS_EPS = 1e-6

LANES = 128
VMEM_LIMIT = 56 * 1024 * 1024

ROW_TILE = 512
FFN_TILE = 512
FOX_TQ = 512
FOX_TK = 512
CUMSUM_TILE = 256

LOG2E = math.log2(math.e)
FOX_QSCALE = LOG2E / math.sqrt(HEAD_DIM)
NEG_BIG = -1e30


def _params(*sem):
    return pltpu.CompilerParams(dimension_semantics=sem, vmem_limit_bytes=VMEM_LIMIT)


def _rms_normalize(x, gain):
    ms = jnp.mean(x * x, axis=-1, keepdims=True)
    return ((x * lax.rsqrt(ms + RMS_EPS)) * gain).astype(jnp.bfloat16)


def _pool_in_kernel(x_ref, g_ref, w_ref, u_ref, qm_ref):
    hn = _rms_normalize(x_ref[...], g_ref[...])
    res = jnp.dot(hn, w_ref[...], preferred_element_type=jnp.float32)
    u_ref[...] = res[:, :POOL_WIDTH]
    qm_ref[...] = res[:, POOL_WIDTH:].astype(jnp.bfloat16)


def _pool_in_proj(x2d, gain, w):
    rows = x2d.shape[0]
    return pl.pallas_call(
        _pool_in_kernel,
        grid=(rows // ROW_TILE,),
        in_specs=[
            pl.BlockSpec((ROW_TILE, D_MODEL), lambda i: (i, 0)),
            pl.BlockSpec((1, D_MODEL), lambda i: (0, 0)),
            pl.BlockSpec((D_MODEL, D_MODEL), lambda i: (0, 0)),
        ],
        out_specs=[
            pl.BlockSpec((ROW_TILE, POOL_WIDTH), lambda i: (i, 0)),
            pl.BlockSpec((ROW_TILE, MEM_WIDTH), lambda i: (i, 0)),
        ],
        out_shape=[
            jax.ShapeDtypeStruct((rows, POOL_WIDTH), jnp.float32),
            jax.ShapeDtypeStruct((rows, MEM_WIDTH), jnp.bfloat16),
        ],
        compiler_params=_params("arbitrary"),
        name="pool_in_proj",
    )(x2d, gain, w)


def _mem_kv_kernel(x_ref, g_ref, w_ref, o_ref):
    hn = _rms_normalize(x_ref[...], g_ref[...])
    o_ref[...] = jnp.dot(hn, w_ref[...], preferred_element_type=jnp.float32).astype(jnp.bfloat16)


def _mem_kv_proj(mem2d, gain, w):
    rows = mem2d.shape[0]
    n = w.shape[1]
    return pl.pallas_call(
        _mem_kv_kernel,
        grid=(1,),
        in_specs=[
            pl.BlockSpec((rows, D_MODEL), lambda i: (0, 0)),
            pl.BlockSpec((1, D_MODEL), lambda i: (0, 0)),
            pl.BlockSpec((D_MODEL, n), lambda i: (0, 0)),
        ],
        out_specs=pl.BlockSpec((rows, n), lambda i: (0, 0)),
        out_shape=jax.ShapeDtypeStruct((rows, n), jnp.bfloat16),
        compiler_params=_params("arbitrary"),
        name="mem_kv_proj",
    )(mem2d, gain, w)


FOX_PROJ_COLS = 3 * FOX_WIDTH + MEM_WIDTH
FOX_PROJ_TILE = 1024


def _fox_in_kernel(x_ref, g_ref, w_ref, cs_ref, wf_ref, o_ref, f_ref, hn_ref):
    j = pl.program_id(1)

    @pl.when(j == 0)
    def _():
        hn = _rms_normalize(x_ref[...], g_ref[...])
        hn_ref[...] = hn
        f_ref[...] = jnp.dot(hn, wf_ref[...], preferred_element_type=jnp.float32)

    res = jnp.dot(hn_ref[...], w_ref[...], preferred_element_type=jnp.float32)
    o_ref[...] = (res * cs_ref[...]).astype(jnp.bfloat16)


def _fox_in_proj(x2d, gain, w, colscale, wf):
    rows = x2d.shape[0]
    return pl.pallas_call(
        _fox_in_kernel,
        grid=(rows // ROW_TILE, FOX_PROJ_COLS // FOX_PROJ_TILE),
        in_specs=[
            pl.BlockSpec((ROW_TILE, D_MODEL), lambda i, j: (i, 0)),
            pl.BlockSpec((1, D_MODEL), lambda i, j: (0, 0)),
            pl.BlockSpec((D_MODEL, FOX_PROJ_TILE), lambda i, j: (0, j)),
            pl.BlockSpec((1, FOX_PROJ_TILE), lambda i, j: (0, j)),
            pl.BlockSpec((D_MODEL, LANES), lambda i, j: (0, 0)),
        ],
        out_specs=[
            pl.BlockSpec((ROW_TILE, FOX_PROJ_TILE), lambda i, j: (i, j)),
            pl.BlockSpec((ROW_TILE, LANES), lambda i, j: (i, 0)),
        ],
        out_shape=[
            jax.ShapeDtypeStruct((rows, FOX_PROJ_COLS), jnp.bfloat16),
            jax.ShapeDtypeStruct((rows, LANES), jnp.float32),
        ],
        scratch_shapes=[pltpu.VMEM((ROW_TILE, D_MODEL), jnp.bfloat16)],
        compiler_params=_params("arbitrary", "arbitrary"),
        name="fox_in_proj",
    )(x2d, gain, w, colscale, wf)


def _pool_mix_kernel(u_ref, w_ref, s_ref, y_ref, ext_ref):
    i = pl.program_id(1)
    tm = u_ref.shape[0]

    @pl.when(i == 0)
    def _():
        ext_ref[0:POOL_HALO, :] = jnp.zeros((POOL_HALO, POOL_WIDTH), jnp.float32)

    @pl.when(i > 0)
    def _():
        ext_ref[0:POOL_HALO, :] = ext_ref[tm:tm + POOL_HALO, :]

    ext_ref[POOL_HALO:, :] = u_ref[...]

    t = i * tm + lax.broadcasted_iota(jnp.int32, (tm, 1), 0)
    for g, win in enumerate(POOL_WINDOWS):
        c0 = g * POOL_GROUP_WIDTH
        c1 = c0 + POOL_GROUP_WIDTH
        tok = ext_ref[POOL_HALO:, c0:c1]
        wsum = tok
        for back in range(1, win):
            wsum = wsum + ext_ref[POOL_HALO - back:POOL_HALO - back + tm, c0:c1]
        count = jnp.minimum(t + 1, win).astype(jnp.float32)
        pooled = wsum / count - tok
        mixed = jnp.dot(pooled.astype(jnp.bfloat16), w_ref[g], preferred_element_type=jnp.float32)
        y_ref[:, c0:c1] = (mixed * s_ref[:, c0:c1]).astype(jnp.bfloat16)


def _pool_mix(u, w_grp, scale):
    b, s, _ = u.shape
    return pl.pallas_call(
        _pool_mix_kernel,
        grid=(b, s // ROW_TILE),
        in_specs=[
            pl.BlockSpec((None, ROW_TILE, POOL_WIDTH), lambda bi, i: (bi, i, 0)),
            pl.BlockSpec((POOL_GROUPS, POOL_GROUP_WIDTH, POOL_GROUP_WIDTH), lambda bi, i: (0, 0, 0)),
            pl.BlockSpec((1, POOL_WIDTH), lambda bi, i: (0, 0)),
        ],
        out_specs=pl.BlockSpec((None, ROW_TILE, POOL_WIDTH), lambda bi, i: (bi, i, 0)),
        out_shape=jax.ShapeDtypeStruct((b, s, POOL_WIDTH), jnp.bfloat16),
        scratch_shapes=[pltpu.VMEM((ROW_TILE + POOL_HALO, POOL_WIDTH), jnp.float32)],
        compiler_params=_params("arbitrary", "arbitrary"),
        name="pool_mix",
    )(u, w_grp, scale)


def _mem_attn_kernel(q_ref, k_ref, v_ref, o_ref):
    inv_sqrt = 1.0 / math.sqrt(HEAD_DIM)
    for h in range(MEM_HEADS):
        c0 = h * HEAD_DIM
        c1 = c0 + HEAD_DIM
        logits = lax.dot_general(q_ref[:, c0:c1], k_ref[:, c0:c1], (((1,), (1,)), ((), ())),
                                 preferred_element_type=jnp.float32)
        logits = logits * inv_sqrt
        m = jnp.max(logits, axis=-1, keepdims=True)
        e = jnp.exp(logits - m)
        p = e / jnp.sum(e, axis=-1, keepdims=True)
        out = jnp.dot(p.astype(jnp.bfloat16), v_ref[:, c0:c1], preferred_element_type=jnp.float32)
        o_ref[:, c0:c1] = out.astype(jnp.bfloat16)


def _mem_attn(q_arr, q_col_block, kv, seq):
    b = q_arr.shape[0]
    return pl.pallas_call(
        _mem_attn_kernel,
        grid=(b, seq // ROW_TILE),
        in_specs=[
            pl.BlockSpec((None, ROW_TILE, MEM_WIDTH), lambda bi, i: (bi, i, q_col_block)),
            pl.BlockSpec((None, MEM_LEN, MEM_WIDTH), lambda bi, i: (bi, 0, 0)),
            pl.BlockSpec((None, MEM_LEN, MEM_WIDTH), lambda bi, i: (bi, 0, 1)),
        ],
        out_specs=pl.BlockSpec((None, ROW_TILE, MEM_WIDTH), lambda bi, i: (bi, i, 0)),
        out_shape=jax.ShapeDtypeStruct((b, seq, MEM_WIDTH), jnp.bfloat16),
        compiler_params=_params("arbitrary", "arbitrary"),
        name="mem_attn",
    )(q_arr, kv, kv)


def _fox_gate_kernel(f_ref, b_ref, o_ref, carry_ref):
    i = pl.program_id(1)
    tm = f_ref.shape[0]

    @pl.when(i == 0)
    def _():
        carry_ref[...] = jnp.zeros_like(carry_ref)

    z = f_ref[...] + b_ref[...]
    log_f = jnp.minimum(z, 0.0) - jnp.log1p(jnp.exp(-jnp.abs(z)))
    row = lax.broadcasted_iota(jnp.int32, (tm, tm), 0)
    col = lax.broadcasted_iota(jnp.int32, (tm, tm), 1)
    tri = (col <= row).astype(jnp.float32)
    csum = jnp.dot(tri, log_f, preferred_element_type=jnp.float32,
                   precision=lax.Precision.HIGHEST) + carry_ref[...]
    o_ref[...] = csum
    carry_ref[...] = csum[tm - 1:tm, :]


def _fox_gate_cumsum(f_logit, bias):
    b, s, _ = f_logit.shape
    return pl.pallas_call(
        _fox_gate_kernel,
        grid=(b, s // CUMSUM_TILE),
        in_specs=[
            pl.BlockSpec((None, CUMSUM_TILE, LANES), lambda bi, i: (bi, i, 0)),
            pl.BlockSpec((1, LANES), lambda bi, i: (0, 0)),
        ],
        out_specs=pl.BlockSpec((None, CUMSUM_TILE, LANES), lambda bi, i: (bi, i, 0)),
        out_shape=jax.ShapeDtypeStruct((b, s, LANES), jnp.float32),
        scratch_shapes=[pltpu.VMEM((1, LANES), jnp.float32)],
        compiler_params=_params("arbitrary", "arbitrary"),
        name="fox_gate_cumsum",
    )(f_logit, bias)


def _bf16_pieces(x):
    hi = x.astype(jnp.bfloat16)
    r1 = x - hi.astype(jnp.float32)
    mid = r1.astype(jnp.bfloat16)
    r2 = r1 - mid.astype(jnp.float32)
    lo = r2.astype(jnp.bfloat16)
    return hi.astype(jnp.float32), mid.astype(jnp.float32), lo.astype(jnp.float32)


def _fox_bias_kernel(c_ref, qe_ref, ke_ref):
    h = pl.program_id(1)
    tm = c_ref.shape[0]
    lane = lax.broadcasted_iota(jnp.int32, (tm, LANES), 1)
    fsel = jnp.sum(jnp.where(lane == h, c_ref[...], 0.0), axis=1, keepdims=True) * LOG2E
    q_hi, q_mid, q_lo = _bf16_pieces(fsel)
    k_hi, k_mid, k_lo = _bf16_pieces(-fsel)
    one = jnp.float32(1.0)
    zero = jnp.float32(0.0)
    qe = jnp.where(lane == 0, q_hi, jnp.where(lane == 1, q_mid, jnp.where(lane == 2, q_lo,
         jnp.where(lane < 6, one, zero))))
    ke = jnp.where(lane < 3, one, jnp.where(lane == 3, k_hi, jnp.where(lane == 4, k_mid,
         jnp.where(lane == 5, k_lo, zero))))
    qe_ref[...] = qe.astype(jnp.bfloat16)
    ke_ref[...] = ke.astype(jnp.bfloat16)


def _fox_bias_operands(fcum):
    b, s, _ = fcum.shape
    out = jax.ShapeDtypeStruct((b, FOX_HEADS, s, LANES), jnp.bfloat16)
    spec = pl.BlockSpec((None, None, ROW_TILE, LANES), lambda bi, h, i: (bi, h, i, 0))
    return pl.pallas_call(
        _fox_bias_kernel,
        grid=(b, FOX_HEADS, s // ROW_TILE),
        in_specs=[pl.BlockSpec((None, ROW_TILE, LANES), lambda bi, h, i: (bi, i, 0))],
        out_specs=[spec, spec],
        out_shape=[out, out],
        compiler_params=_params("arbitrary", "arbitrary", "arbitrary"),
        name="fox_bias_operands",
    )(fcum)


def _fox_attn_kernel(q_ref, qe_ref, k_ref, ke_ref, v_ref, o_ref):
    i = pl.program_id(2)
    tq = q_ref.shape[0]
    q2 = jnp.concatenate([q_ref[...], qe_ref[...]], axis=1)

    def scores(j):
        ks = pl.ds(pl.multiple_of(j * FOX_TK, FOX_TK), FOX_TK)
        k2 = jnp.concatenate([k_ref[ks, :], ke_ref[ks, :]], axis=1)
        s = lax.dot_general(q2, k2, (((1,), (1,)), ((), ())), preferred_element_type=jnp.float32)
        return s, ks

    def update(s, ks, carry):
        m, l, acc = carry
        m_new = jnp.maximum(m, jnp.max(s, axis=-1, keepdims=True))
        alpha = jnp.exp2(m - m_new)
        p = jnp.exp2(s - m_new)
        l = alpha * l + jnp.sum(p, axis=-1, keepdims=True)
        acc = alpha * acc + jnp.dot(p.astype(jnp.bfloat16), v_ref[ks, :],
                                    preferred_element_type=jnp.float32)
        return m_new, l, acc

    def body(j, carry):
        s, ks = scores(j)
        return update(s, ks, carry)

    init = (jnp.full((tq, 1), NEG_BIG, jnp.float32),
            jnp.zeros((tq, 1), jnp.float32),
            jnp.zeros((tq, HEAD_DIM), jnp.float32))
    carry = lax.fori_loop(0, i, body, init)

    s, ks = scores(i)
    row = lax.broadcasted_iota(jnp.int32, (tq, FOX_TK), 0)
    col = lax.broadcasted_iota(jnp.int32, (tq, FOX_TK), 1)
    s = jnp.where(col <= row, s, -jnp.inf)
    _, l, acc = update(s, ks, carry)
    o_ref[...] = (acc / l).astype(jnp.bfloat16)


def _fox_attention(proj, qe, ke):
    b, s, _ = proj.shape
    assert FOX_TQ == FOX_TK
    return pl.pallas_call(
        _fox_attn_kernel,
        grid=(b, FOX_HEADS, s // FOX_TQ),
        in_specs=[
            pl.BlockSpec((None, FOX_TQ, HEAD_DIM), lambda bi, h, i: (bi, i, h)),
            pl.BlockSpec((None, None, FOX_TQ, LANES), lambda bi, h, i: (bi, h, i, 0)),
            pl.BlockSpec((None, s, HEAD_DIM), lambda bi, h, i: (bi, 0, FOX_HEADS + h)),
            pl.BlockSpec((None, None, s, LANES), lambda bi, h, i: (bi, h, 0, 0)),
            pl.BlockSpec((None, s, HEAD_DIM), lambda bi, h, i: (bi, 0, 2 * FOX_HEADS + h)),
        ],
        out_specs=pl.BlockSpec((None, FOX_TQ, HEAD_DIM), lambda bi, h, i: (bi, i, h)),
        out_shape=jax.ShapeDtypeStruct((b, s, FOX_WIDTH), jnp.bfloat16),
        compiler_params=_params("arbitrary", "arbitrary", "arbitrary"),
        name="fox_attention",
    )(proj, qe, proj, ke, proj)


def _out_proj_kernel(x_ref, ya_ref, yb_ref, wa_ref, wb_ref, o_ref):
    acc = jnp.dot(ya_ref[...], wa_ref[...], preferred_element_type=jnp.float32)
    acc = acc + jnp.dot(yb_ref[...], wb_ref[...], preferred_element_type=jnp.float32)
    o_ref[...] = x_ref[...] + acc


def _out_proj(x2d, y_mix, y_mem, w_mix, w_mem):
    rows = x2d.shape[0]
    return pl.pallas_call(
        _out_proj_kernel,
        grid=(rows // ROW_TILE,),
        in_specs=[
            pl.BlockSpec((ROW_TILE, D_MODEL), lambda i: (i, 0)),
            pl.BlockSpec((ROW_TILE, POOL_WIDTH), lambda i: (i, 0)),
            pl.BlockSpec((ROW_TILE, MEM_WIDTH), lambda i: (i, 0)),
            pl.BlockSpec((POOL_WIDTH, D_MODEL), lambda i: (0, 0)),
            pl.BlockSpec((MEM_WIDTH, D_MODEL), lambda i: (0, 0)),
        ],
        out_specs=pl.BlockSpec((ROW_TILE, D_MODEL), lambda i: (i, 0)),
        out_shape=jax.ShapeDtypeStruct((rows, D_MODEL), jnp.float32),
        compiler_params=_params("arbitrary"),
        name="out_proj",
    )(x2d, y_mix, y_mem, w_mix, w_mem)


def _ffn_kernel(x_ref, g_ref, w1_ref, w2_ref, o_ref, hn_ref, acc_ref):
    j = pl.program_id(1)

    @pl.when(j == 0)
    def _():
        hn_ref[...] = _rms_normalize(x_ref[...], g_ref[...])

    a = jnp.dot(hn_ref[...], w1_ref[...], preferred_element_type=jnp.float32)
    a = jnp.square(jnp.maximum(a, 0.0)).astype(jnp.bfloat16)
    part = jnp.dot(a, w2_ref[...], preferred_element_type=jnp.float32)

    @pl.when(j == 0)
    def _():
        acc_ref[...] = part

    @pl.when(j > 0)
    def _():
        acc_ref[...] += part

    @pl.when(j == pl.num_programs(1) - 1)
    def _():
        o_ref[...] = x_ref[...] + acc_ref[...]


def _ffn(x2d, gain, w1, w2):
    rows = x2d.shape[0]
    return pl.pallas_call(
        _ffn_kernel,
        grid=(rows // ROW_TILE, FFN_HIDDEN // FFN_TILE),
        in_specs=[
            pl.BlockSpec((ROW_TILE, D_MODEL), lambda i, j: (i, 0)),
            pl.BlockSpec((1, D_MODEL), lambda i, j: (0, 0)),
            pl.BlockSpec((D_MODEL, FFN_TILE), lambda i, j: (0, j)),
            pl.BlockSpec((FFN_TILE, D_MODEL), lambda i, j: (j, 0)),
        ],
        out_specs=pl.BlockSpec((ROW_TILE, D_MODEL), lambda i, j: (i, 0)),
        out_shape=jax.ShapeDtypeStruct((rows, D_MODEL), jnp.float32),
        scratch_shapes=[pltpu.VMEM((ROW_TILE, D_MODEL), jnp.bfloat16),
                        pltpu.VMEM((ROW_TILE, D_MODEL), jnp.float32)],
        compiler_params=_params("arbitrary", "arbitrary"),
        name="ffn",
    )(x2d, gain, w1, w2)


def _final_norm_kernel(x_ref, g_ref, o_ref):
    x = x_ref[...]
    ms = jnp.mean(x * x, axis=-1, keepdims=True)
    o_ref[...] = (x * lax.rsqrt(ms + RMS_EPS)) * g_ref[...]


def _final_norm(x2d, gain):
    rows = x2d.shape[0]
    return pl.pallas_call(
        _final_norm_kernel,
        grid=(rows // ROW_TILE,),
        in_specs=[pl.BlockSpec((ROW_TILE, D_MODEL), lambda i: (i, 0)),
                  pl.BlockSpec((1, D_MODEL), lambda i: (0, 0))],
        out_specs=pl.BlockSpec((ROW_TILE, D_MODEL), lambda i: (i, 0)),
        out_shape=jax.ShapeDtypeStruct((rows, D_MODEL), jnp.float32),
        compiler_params=_params("arbitrary"),
        name="final_norm",
    )(x2d, gain)


def kernel(x, mem, norm_mix, norm_mem, pool_w_in, pool_w_grp, pool_scale, fox_w_in, fox_b_f,
           w_mem_kv, w_out, norm_ffn, w_ffn1, w_ffn2, norm_final):
    b, s, d = x.shape
    rows = b * s
    bf16 = jnp.bfloat16
    f32 = jnp.float32
    x2d = x.reshape(rows, d)
    mem2d = mem.reshape(b * MEM_LEN, d)

    def row(v):
        return v.reshape(1, -1).astype(f32)

    def mix_tail(x2d, layer, y_mix, q_arr, q_col_block):
        kv = _mem_kv_proj(mem2d, row(norm_mem[layer]), w_mem_kv[layer].astype(bf16))
        y_mem = _mem_attn(q_arr, q_col_block, kv.reshape(b, MEM_LEN, 2 * MEM_WIDTH), s)
        w_o = w_out[layer].astype(bf16)
        x2d = _out_proj(x2d, y_mix.reshape(rows, POOL_WIDTH), y_mem.reshape(rows, MEM_WIDTH),
                        w_o[:POOL_WIDTH], w_o[POOL_WIDTH:])
        return _ffn(x2d, row(norm_ffn[layer]), w_ffn1[layer].astype(bf16), w_ffn2[layer].astype(bf16))

    u, q_mem = _pool_in_proj(x2d, row(norm_mix[0]), pool_w_in[0].astype(bf16))
    y_mix = _pool_mix(u.reshape(b, s, POOL_WIDTH), pool_w_grp[0].astype(bf16), row(pool_scale[0]))
    x2d = mix_tail(x2d, 0, y_mix, q_mem.reshape(b, s, MEM_WIDTH), 0)

    w_in = fox_w_in[0]
    n_qkv = 3 * FOX_WIDTH
    w_main = jnp.concatenate([w_in[:, :n_qkv], w_in[:, n_qkv + FOX_HEADS:]], axis=1).astype(bf16)
    w_f = jnp.pad(w_in[:, n_qkv:n_qkv + FOX_HEADS], ((0, 0), (0, LANES - FOX_HEADS))).astype(bf16)
    colscale = jnp.concatenate([jnp.full((1, FOX_WIDTH), FOX_QSCALE, f32),
                                jnp.ones((1, FOX_PROJ_COLS - FOX_WIDTH), f32)], axis=1)
    proj, f_logit = _fox_in_proj(x2d, row(norm_mix[1]), w_main, colscale, w_f)
    proj = proj.reshape(b, s, FOX_PROJ_COLS)
    bias_f = jnp.pad(fox_b_f[0].astype(f32), (0, LANES - FOX_HEADS)).reshape(1, LANES)
    fcum = _fox_gate_cumsum(f_logit.reshape(b, s, LANES), bias_f)
    qe, ke = _fox_bias_operands(fcum)
    y_mix = _fox_attention(proj, qe, ke)
    x2d = mix_tail(x2d, 1, y_mix, proj, FOX_PROJ_COLS // MEM_WIDTH - 1)

    return _final_norm(x2d, row(norm_final)).reshape(b, s, d)
```

```python
import functools
import math

import jax
import jax.numpy as jnp
from jax import lax
from jax.experimental import pallas as pl
from jax.experimental.pallas import tpu as pltpu

D_MODEL = 2048
HEAD_DIM = 128
MEM_LEN = 256
MEM_HEADS = 4
MEM_WIDTH = MEM_HEADS * HEAD_DIM
POOL_WIDTH = D_MODEL - MEM_WIDTH
POOL_GROUPS = 4
POOL_GROUP_WIDTH = POOL_WIDTH // POOL_GROUPS
POOL_WINDOWS = (2, 4, 8, 16)
POOL_HALO = 16
FOX_HEADS = POOL_WIDTH // HEAD_DIM
FOX_WIDTH = FOX_HEADS * HEAD_DIM
FFN_HIDDEN = 4 * D_MODEL
RMS_EPS = 1e-6

LANES = 128
VMEM_LIMIT = 56 * 1024 * 1024

ROW_TILE = 512
FFN_TILE = 1024
FOX_TQ = 512
FOX_TK = 512
FOX_UNROLL = 4
CUMSUM_TILE = 256

LOG2E = math.log2(math.e)
FOX_QSCALE = LOG2E / math.sqrt(HEAD_DIM)
NEG_BIG = -1e30


def _params(*sem):
    return pltpu.CompilerParams(dimension_semantics=sem, vmem_limit_bytes=VMEM_LIMIT)


def _rms_normalize(x, gain):
    ms = jnp.mean(x * x, axis=-1, keepdims=True)
    return ((x * lax.rsqrt(ms + RMS_EPS)) * gain).astype(jnp.bfloat16)


def _pool_in_kernel(x_ref, g_ref, w_ref, u_ref, qm_ref):
    hn = _rms_normalize(x_ref[...], g_ref[...])
    res = jnp.dot(hn, w_ref[...], preferred_element_type=jnp.float32)
    u_ref[...] = res[:, :POOL_WIDTH]
    qm_ref[...] = res[:, POOL_WIDTH:].astype(jnp.bfloat16)


def _pool_in_proj(x2d, gain, w):
    rows = x2d.shape[0]
    return pl.pallas_call(
        _pool_in_kernel,
        grid=(rows // ROW_TILE,),
        in_specs=[
            pl.BlockSpec((ROW_TILE, D_MODEL), lambda i: (i, 0)),
            pl.BlockSpec((1, D_MODEL), lambda i: (0, 0)),
            pl.BlockSpec((D_MODEL, D_MODEL), lambda i: (0, 0)),
        ],
        out_specs=[
            pl.BlockSpec((ROW_TILE, POOL_WIDTH), lambda i: (i, 0)),
            pl.BlockSpec((ROW_TILE, MEM_WIDTH), lambda i: (i, 0)),
        ],
        out_shape=[
            jax.ShapeDtypeStruct((rows, POOL_WIDTH), jnp.float32),
            jax.ShapeDtypeStruct((rows, MEM_WIDTH), jnp.bfloat16),
        ],
        compiler_params=_params("arbitrary"),
        name="pool_in_proj",
    )(x2d, gain, w)


def _mem_kv_kernel(x_ref, g_ref, w_ref, o_ref):
    hn = _rms_normalize(x_ref[...], g_ref[...])
    o_ref[...] = jnp.dot(hn, w_ref[...], preferred_element_type=jnp.float32).astype(jnp.bfloat16)


def _mem_kv_proj(mem2d, gain, w):
    rows = mem2d.shape[0]
    n = w.shape[1]
    return pl.pallas_call(
        _mem_kv_kernel,
        grid=(1,),
        in_specs=[
            pl.BlockSpec((rows, D_MODEL), lambda i: (0, 0)),
            pl.BlockSpec((1, D_MODEL), lambda i: (0, 0)),
            pl.BlockSpec((D_MODEL, n), lambda i: (0, 0)),
        ],
        out_specs=pl.BlockSpec((rows, n), lambda i: (0, 0)),
        out_shape=jax.ShapeDtypeStruct((rows, n), jnp.bfloat16),
        compiler_params=_params("arbitrary"),
        name="mem_kv_proj",
    )(mem2d, gain, w)


FOX_PROJ_COLS = 3 * FOX_WIDTH + MEM_WIDTH
FOX_PROJ_TILE = 1024


def _fox_in_kernel(x_ref, g_ref, w_ref, cs_ref, wf_ref, o_ref, f_ref, hn_ref):
    j = pl.program_id(1)

    @pl.when(j == 0)
    def _():
        hn = _rms_normalize(x_ref[...], g_ref[...])
        hn_ref[...] = hn
        f_ref[...] = jnp.dot(hn, wf_ref[...], preferred_element_type=jnp.float32)

    res = jnp.dot(hn_ref[...], w_ref[...], preferred_element_type=jnp.float32)
    o_ref[...] = (res * cs_ref[...]).astype(jnp.bfloat16)


def _fox_in_proj(x2d, gain, w, colscale, wf):
    rows = x2d.shape[0]
    return pl.pallas_call(
        _fox_in_kernel,
        grid=(rows // ROW_TILE, FOX_PROJ_COLS // FOX_PROJ_TILE),
        in_specs=[
            pl.BlockSpec((ROW_TILE, D_MODEL), lambda i, j: (i, 0)),
            pl.BlockSpec((1, D_MODEL), lambda i, j: (0, 0)),
            pl.BlockSpec((D_MODEL, FOX_PROJ_TILE), lambda i, j: (0, j)),
            pl.BlockSpec((1, FOX_PROJ_TILE), lambda i, j: (0, j)),
            pl.BlockSpec((D_MODEL, LANES), lambda i, j: (0, 0)),
        ],
        out_specs=[
            pl.BlockSpec((ROW_TILE, FOX_PROJ_TILE), lambda i, j: (i, j)),
            pl.BlockSpec((ROW_TILE, LANES), lambda i, j: (i, 0)),
        ],
        out_shape=[
            jax.ShapeDtypeStruct((rows, FOX_PROJ_COLS), jnp.bfloat16),
            jax.ShapeDtypeStruct((rows, LANES), jnp.float32),
        ],
        scratch_shapes=[pltpu.VMEM((ROW_TILE, D_MODEL), jnp.bfloat16)],
        compiler_params=_params("arbitrary", "arbitrary"),
        name="fox_in_proj",
    )(x2d, gain, w, colscale, wf)


def _pool_mix_kernel(u_ref, w_ref, s_ref, y_ref, ext_ref):
    i = pl.program_id(1)
    tm = u_ref.shape[0]

    @pl.when(i == 0)
    def _():
        ext_ref[0:POOL_HALO, :] = jnp.zeros((POOL_HALO, POOL_WIDTH), jnp.float32)

    @pl.when(i > 0)
    def _():
        ext_ref[0:POOL_HALO, :] = ext_ref[tm:tm + POOL_HALO, :]

    ext_ref[POOL_HALO:, :] = u_ref[...]

    t = i * tm + lax.broadcasted_iota(jnp.int32, (tm, 1), 0)
    for g, win in enumerate(POOL_WINDOWS):
        c0 = g * POOL_GROUP_WIDTH
        c1 = c0 + POOL_GROUP_WIDTH
        tok = ext_ref[POOL_HALO:, c0:c1]
        wsum = tok
        for back in range(1, win):
            wsum = wsum + ext_ref[POOL_HALO - back:POOL_HALO - back + tm, c0:c1]
        count = jnp.minimum(t + 1, win).astype(jnp.float32)
        pooled = wsum / count - tok
        mixed = jnp.dot(pooled.astype(jnp.bfloat16), w_ref[g], preferred_element_type=jnp.float32)
        y_ref[:, c0:c1] = (mixed * s_ref[:, c0:c1]).astype(jnp.bfloat16)


def _pool_mix(u, w_grp, scale):
    b, s, _ = u.shape
    return pl.pallas_call(
        _pool_mix_kernel,
        grid=(b, s // ROW_TILE),
        in_specs=[
            pl.BlockSpec((None, ROW_TILE, POOL_WIDTH), lambda bi, i: (bi, i, 0)),
            pl.BlockSpec((POOL_GROUPS, POOL_GROUP_WIDTH, POOL_GROUP_WIDTH), lambda bi, i: (0, 0, 0)),
            pl.BlockSpec((1, POOL_WIDTH), lambda bi, i: (0, 0)),
        ],
        out_specs=pl.BlockSpec((None, ROW_TILE, POOL_WIDTH), lambda bi, i: (bi, i, 0)),
        out_shape=jax.ShapeDtypeStruct((b, s, POOL_WIDTH), jnp.bfloat16),
        scratch_shapes=[pltpu.VMEM((ROW_TILE + POOL_HALO, POOL_WIDTH), jnp.float32)],
        compiler_params=_params("arbitrary", "arbitrary"),
        name="pool_mix",
    )(u, w_grp, scale)


def _mem_attn_kernel(q_ref, k_ref, v_ref, o_ref):
    inv_sqrt = 1.0 / math.sqrt(HEAD_DIM)
    for h in range(MEM_HEADS):
        c0 = h * HEAD_DIM
        c1 = c0 + HEAD_DIM
        logits = lax.dot_general(q_ref[:, c0:c1], k_ref[:, c0:c1], (((1,), (1,)), ((), ())),
                                 preferred_element_type=jnp.float32)
        logits = logits * inv_sqrt
        m = jnp.max(logits, axis=-1, keepdims=True)
        e = jnp.exp(logits - m)
        p = e / jnp.sum(e, axis=-1, keepdims=True)
        out = jnp.dot(p.astype(jnp.bfloat16), v_ref[:, c0:c1], preferred_element_type=jnp.float32)
        o_ref[:, c0:c1] = out.astype(jnp.bfloat16)


def _mem_attn(q_arr, q_col_block, kv, seq):
    b = q_arr.shape[0]
    return pl.pallas_call(
        _mem_attn_kernel,
        grid=(b, seq // ROW_TILE),
        in_specs=[
            pl.BlockSpec((None, ROW_TILE, MEM_WIDTH), lambda bi, i: (bi, i, q_col_block)),
            pl.BlockSpec((None, MEM_LEN, MEM_WIDTH), lambda bi, i: (bi, 0, 0)),
            pl.BlockSpec((None, MEM_LEN, MEM_WIDTH), lambda bi, i: (bi, 0, 1)),
        ],
        out_specs=pl.BlockSpec((None, ROW_TILE, MEM_WIDTH), lambda bi, i: (bi, i, 0)),
        out_shape=jax.ShapeDtypeStruct((b, seq, MEM_WIDTH), jnp.bfloat16),
        compiler_params=_params("arbitrary", "arbitrary"),
        name="mem_attn",
    )(q_arr, kv, kv)


def _bf16_pieces(x):
    hi = x.astype(jnp.bfloat16)
    r1 = x - hi.astype(jnp.float32)
    mid = r1.astype(jnp.bfloat16)
    r2 = r1 - mid.astype(jnp.float32)
    lo = r2.astype(jnp.bfloat16)
    return hi.astype(jnp.float32), mid.astype(jnp.float32), lo.astype(jnp.float32)


def _fox_gate_kernel(f_ref, b_ref, qe_ref, ke_ref, carry_ref):
    i = pl.program_id(1)
    tm = f_ref.shape[0]

    @pl.when(i == 0)
    def _():
        carry_ref[...] = jnp.zeros_like(carry_ref)

    z = f_ref[...] + b_ref[...]
    log_f = jnp.minimum(z, 0.0) - jnp.log1p(jnp.exp(-jnp.abs(z)))
    row = lax.broadcasted_iota(jnp.int32, (tm, tm), 0)
    col = lax.broadcasted_iota(jnp.int32, (tm, tm), 1)
    tri = (col <= row).astype(jnp.float32)
    csum = jnp.dot(tri, log_f, preferred_element_type=jnp.float32,
                   precision=lax.Precision.HIGHEST) + carry_ref[...]
    carry_ref[...] = csum[tm - 1:tm, :]

    hi, mid, lo = _bf16_pieces(csum * LOG2E)
    lane = lax.broadcasted_iota(jnp.int32, (tm, LANES), 1)
    q_ones = jnp.where((lane >= 3) & (lane < 6), 1.0, 0.0)
    k_ones = jnp.where(lane < 3, 1.0, 0.0)
    for h in range(FOX_HEADS):
        h_hi, h_mid, h_lo = hi[:, h:h + 1], mid[:, h:h + 1], lo[:, h:h + 1]
        qe = jnp.where(lane == 0, h_hi, jnp.where(lane == 1, h_mid, jnp.where(lane == 2, h_lo, q_ones)))
        ke = jnp.where(lane == 3, -h_hi, jnp.where(lane == 4, -h_mid, jnp.where(lane == 5, -h_lo, k_ones)))
        qe_ref[h] = qe.astype(jnp.bfloat16)
        ke_ref[h] = ke.astype(jnp.bfloat16)


def _fox_gate_operands(f_logit, bias):
    b, s, _ = f_logit.shape
    out = jax.ShapeDtypeStruct((b, FOX_HEADS, s, LANES), jnp.bfloat16)
    spec = pl.BlockSpec((None, FOX_HEADS, CUMSUM_TILE, LANES), lambda bi, i: (bi, 0, i, 0))
    return pl.pallas_call(
        _fox_gate_kernel,
        grid=(b, s // CUMSUM_TILE),
        in_specs=[
            pl.BlockSpec((None, CUMSUM_TILE, LANES), lambda bi, i: (bi, i, 0)),
            pl.BlockSpec((1, LANES), lambda bi, i: (0, 0)),
        ],
        out_specs=[spec, spec],
        out_shape=[out, out],
        scratch_shapes=[pltpu.VMEM((1, LANES), jnp.float32)],
        compiler_params=_params("arbitrary", "arbitrary"),
        name="fox_gate_operands",
    )(f_logit, bias)


def _fox_attn_kernel(q_ref, qe_ref, k_ref, ke_ref, v_ref, o_ref):
    i = pl.program_id(2)
    tq = q_ref.shape[0]
    q2 = jnp.concatenate([q_ref[...], qe_ref[...]], axis=1)

    def scores(j):
        ks = pl.ds(pl.multiple_of(j * FOX_TK, FOX_TK), FOX_TK)
        k2 = jnp.concatenate([k_ref[ks, :], ke_ref[ks, :]], axis=1)
        s = lax.dot_general(q2, k2, (((1,), (1,)), ((), ())), preferred_element_type=jnp.float32)
        return s, ks

    def update(s, ks, carry):
        m, l, acc = carry
        m_new = jnp.maximum(m, jnp.max(s, axis=-1, keepdims=True))
        alpha = jnp.exp2(m - m_new)
        p = jnp.exp2(s - m_new)
        l = alpha * l + jnp.sum(p, axis=-1, keepdims=True)
        acc = alpha * acc + jnp.dot(p.astype(jnp.bfloat16), v_ref[ks, :],
                                    preferred_element_type=jnp.float32)
        return m_new, l, acc

    def body(j, carry):
        s, ks = scores(j)
        return update(s, ks, carry)

    def group_body(t, carry):
        for u in range(FOX_UNROLL):
            carry = body(t * FOX_UNROLL + u, carry)
        return carry

    init = (jnp.full((tq, 1), NEG_BIG, jnp.float32),
            jnp.zeros((tq, 1), jnp.float32),
            jnp.zeros((tq, HEAD_DIM), jnp.float32))
    groups = i // FOX_UNROLL
    carry = lax.fori_loop(0, groups, group_body, init)
    carry = lax.fori_loop(groups * FOX_UNROLL, i, body, carry)

    s, ks = scores(i)
    row = lax.broadcasted_iota(jnp.int32, (tq, FOX_TK), 0)
    col = lax.broadcasted_iota(jnp.int32, (tq, FOX_TK), 1)
    s = jnp.where(col <= row, s, -jnp.inf)
    _, l, acc = update(s, ks, carry)
    o_ref[...] = (acc / l).astype(jnp.bfloat16)


def _fox_attention(proj, qe, ke):
    b, s, _ = proj.shape
    assert FOX_TQ == FOX_TK
    return pl.pallas_call(
        _fox_attn_kernel,
        grid=(b, FOX_HEADS, s // FOX_TQ),
        in_specs=[
            pl.BlockSpec((None, FOX_TQ, HEAD_DIM), lambda bi, h, i: (bi, i, h)),
            pl.BlockSpec((None, None, FOX_TQ, LANES), lambda bi, h, i: (bi, h, i, 0)),
            pl.BlockSpec((None, s, HEAD_DIM), lambda bi, h, i: (bi, 0, FOX_HEADS + h)),
            pl.BlockSpec((None, None, s, LANES), lambda bi, h, i: (bi, h, 0, 0)),
            pl.BlockSpec((None, s, HEAD_DIM), lambda bi, h, i: (bi, 0, 2 * FOX_HEADS + h)),
        ],
        out_specs=pl.BlockSpec((None, FOX_TQ, HEAD_DIM), lambda bi, h, i: (bi, i, h)),
        out_shape=jax.ShapeDtypeStruct((b, s, FOX_WIDTH), jnp.bfloat16),
        compiler_params=_params("arbitrary", "arbitrary", "arbitrary"),
        name="fox_attention",
    )(proj, qe, proj, ke, proj)


def _out_proj_kernel(x_ref, ya_ref, yb_ref, wa_ref, wb_ref, g_ref, o_ref, hn_ref):
    acc = jnp.dot(ya_ref[...], wa_ref[...], preferred_element_type=jnp.float32)
    acc = acc + jnp.dot(yb_ref[...], wb_ref[...], preferred_element_type=jnp.float32)
    x_new = x_ref[...] + acc
    o_ref[...] = x_new
    hn_ref[...] = _rms_normalize(x_new, g_ref[...])


def _out_proj(x2d, y_mix, y_mem, w_mix, w_mem, ffn_gain):
    rows = x2d.shape[0]
    return pl.pallas_call(
        _out_proj_kernel,
        grid=(rows // ROW_TILE,),
        in_specs=[
            pl.BlockSpec((ROW_TILE, D_MODEL), lambda i: (i, 0)),
            pl.BlockSpec((ROW_TILE, POOL_WIDTH), lambda i: (i, 0)),
            pl.BlockSpec((ROW_TILE, MEM_WIDTH), lambda i: (i, 0)),
            pl.BlockSpec((POOL_WIDTH, D_MODEL), lambda i: (0, 0)),
            pl.BlockSpec((MEM_WIDTH, D_MODEL), lambda i: (0, 0)),
            pl.BlockSpec((1, D_MODEL), lambda i: (0, 0)),
        ],
        out_specs=[pl.BlockSpec((ROW_TILE, D_MODEL), lambda i: (i, 0)),
                   pl.BlockSpec((ROW_TILE, D_MODEL), lambda i: (i, 0))],
        out_shape=[jax.ShapeDtypeStruct((rows, D_MODEL), jnp.float32),
                   jax.ShapeDtypeStruct((rows, D_MODEL), jnp.bfloat16)],
        compiler_params=_params("arbitrary"),
        name="out_proj",
    )(x2d, y_mix, y_mem, w_mix, w_mem, ffn_gain)


def _ffn_kernel(x_ref, hn_ref, w1_ref, w2_ref, o_ref):
    @pl.when(pl.program_id(1) == 0)
    def _():
        o_ref[...] = x_ref[...]

    a = jnp.dot(hn_ref[...], w1_ref[...], preferred_element_type=jnp.float32)
    a = jnp.square(jnp.maximum(a, 0.0)).astype(jnp.bfloat16)
    o_ref[...] += jnp.dot(a, w2_ref[...], preferred_element_type=jnp.float32)


def _ffn(x2d, hn2d, w1, w2):
    rows = x2d.shape[0]
    return pl.pallas_call(
        _ffn_kernel,
        grid=(rows // ROW_TILE, FFN_HIDDEN // FFN_TILE),
        in_specs=[
            pl.BlockSpec((ROW_TILE, D_MODEL), lambda i, j: (i, 0)),
            pl.BlockSpec((ROW_TILE, D_MODEL), lambda i, j: (i, 0)),
            pl.BlockSpec((D_MODEL, FFN_TILE), lambda i, j: (0, j)),
            pl.BlockSpec((FFN_TILE, D_MODEL), lambda i, j: (j, 0)),
        ],
        out_specs=pl.BlockSpec((ROW_TILE, D_MODEL), lambda i, j: (i, 0)),
        out_shape=jax.ShapeDtypeStruct((rows, D_MODEL), jnp.float32),
        compiler_params=_params("arbitrary", "arbitrary"),
        name="ffn",
    )(x2d, hn2d, w1, w2)


def _final_norm_kernel(x_ref, g_ref, o_ref):
    x = x_ref[...]
    ms = jnp.mean(x * x, axis=-1, keepdims=True)
    o_ref[...] = (x * lax.rsqrt(ms + RMS_EPS)) * g_ref[...]


def _final_norm(x2d, gain):
    rows = x2d.shape[0]
    return pl.pallas_call(
        _final_norm_kernel,
        grid=(rows // ROW_TILE,),
        in_specs=[pl.BlockSpec((ROW_TILE, D_MODEL), lambda i: (i, 0)),
                  pl.BlockSpec((1, D_MODEL), lambda i: (0, 0))],
        out_specs=pl.BlockSpec((ROW_TILE, D_MODEL), lambda i: (i, 0)),
        out_shape=jax.ShapeDtypeStruct((rows, D_MODEL), jnp.float32),
        compiler_params=_params("arbitrary"),
        name="final_norm",
    )(x2d, gain)


def kernel(x, mem, norm_mix, norm_mem, pool_w_in, pool_w_grp, pool_scale, fox_w_in, fox_b_f,
           w_mem_kv, w_out, norm_ffn, w_ffn1, w_ffn2, norm_final):
    b, s, d = x.shape
    rows = b * s
    bf16 = jnp.bfloat16
    f32 = jnp.float32
    x2d = x.reshape(rows, d)
    mem2d = mem.reshape(b * MEM_LEN, d)

    def row(v):
        return v.reshape(1, -1).astype(f32)

    def mix_tail(x2d, layer, y_mix, q_arr, q_col_block):
        kv = _mem_kv_proj(mem2d, row(norm_mem[layer]), w_mem_kv[layer].astype(bf16))
        y_mem = _mem_attn(q_arr, q_col_block, kv.reshape(b, MEM_LEN, 2 * MEM_WIDTH), s)
        w_o = w_out[layer].astype(bf16)
        x2d, hn2d = _out_proj(x2d, y_mix.reshape(rows, POOL_WIDTH), y_mem.reshape(rows, MEM_WIDTH),
                              w_o[:POOL_WIDTH], w_o[POOL_WIDTH:], row(norm_ffn[layer]))
        return _ffn(x2d, hn2d, w_ffn1[layer].astype(bf16), w_ffn2[layer].astype(bf16))

    u, q_mem = _pool_in_proj(x2d, row(norm_mix[0]), pool_w_in[0].astype(bf16))
    y_mix = _pool_mix(u.reshape(b, s, POOL_WIDTH), pool_w_grp[0].astype(bf16), row(pool_scale[0]))
    x2d = mix_tail(x2d, 0, y_mix, q_mem.reshape(b, s, MEM_WIDTH), 0)

    w_in = fox_w_in[0]
    n_qkv = 3 * FOX_WIDTH
    w_main = jnp.concatenate([w_in[:, :n_qkv], w_in[:, n_qkv + FOX_HEADS:]], axis=1).astype(bf16)
    w_f = jnp.pad(w_in[:, n_qkv:n_qkv + FOX_HEADS], ((0, 0), (0, LANES - FOX_HEADS))).astype(bf16)
    colscale = jnp.concatenate([jnp.full((1, FOX_WIDTH), FOX_QSCALE, f32),
                                jnp.ones((1, FOX_PROJ_COLS - FOX_WIDTH), f32)], axis=1)
    proj, f_logit = _fox_in_proj(x2d, row(norm_mix[1]), w_main, colscale, w_f)
    proj = proj.reshape(b, s, FOX_PROJ_COLS)
    bias_f = jnp.pad(fox_b_f[0].astype(f32), (0, LANES - FOX_HEADS)).reshape(1, LANES)
    qe, ke = _fox_gate_operands(f_logit.reshape(b, s, LANES), bias_f)
    y_mix = _fox_attention(proj, qe, ke)
    x2d = mix_tail(x2d, 1, y_mix, proj, FOX_PROJ_COLS // MEM_WIDTH - 1)

    return _final_norm(x2d, row(norm_final)).reshape(b, s, d)
```

```python
import functools
import math

import jax
import jax.numpy as jnp
from jax import lax
from jax.experimental import pallas as pl
from jax.experimental.pallas import tpu as pltpu

D_MODEL = 2048
HEAD_DIM = 128
MEM_LEN = 256
MEM_HEADS = 4
MEM_WIDTH = MEM_HEADS * HEAD_DIM
POOL_WIDTH = D_MODEL - MEM_WIDTH
POOL_GROUPS = 4
POOL_GROUP_WIDTH = POOL_WIDTH // POOL_GROUPS
POOL_WINDOWS = (2, 4, 8, 16)
POOL_HALO = 16
FOX_HEADS = POOL_WIDTH // HEAD_DIM
FOX_WIDTH = FOX_HEADS * HEAD_DIM
FFN_HIDDEN = 4 * D_MODEL
RMS_EPS = 1e-6

LANES = 128
VMEM_LIMIT = 56 * 1024 * 1024

ROW_TILE = 512
FFN_TILE = 1024
FOX_TQ = 1024
FOX_TK = 512
FOX_UNROLLS = (8, 4, 2)
CUMSUM_TILE = 256

LOG2E = math.log2(math.e)
FOX_QSCALE = LOG2E / math.sqrt(HEAD_DIM)
NEG_BIG = -1e30


def _params(*sem):
    return pltpu.CompilerParams(dimension_semantics=sem, vmem_limit_bytes=VMEM_LIMIT)


def _rms_normalize(x, gain):
    ms = jnp.mean(x * x, axis=-1, keepdims=True)
    return ((x * lax.rsqrt(ms + RMS_EPS)) * gain).astype(jnp.bfloat16)


def _pool_in_kernel(x_ref, g_ref, w_ref, u_ref, qm_ref):
    hn = _rms_normalize(x_ref[...], g_ref[...])
    res = jnp.dot(hn, w_ref[...], preferred_element_type=jnp.float32)
    u_ref[...] = res[:, :POOL_WIDTH]
    qm_ref[...] = res[:, POOL_WIDTH:].astype(jnp.bfloat16)


def _pool_in_proj(x2d, gain, w):
    rows = x2d.shape[0]
    return pl.pallas_call(
        _pool_in_kernel,
        grid=(rows // ROW_TILE,),
        in_specs=[
            pl.BlockSpec((ROW_TILE, D_MODEL), lambda i: (i, 0)),
            pl.BlockSpec((1, D_MODEL), lambda i: (0, 0)),
            pl.BlockSpec((D_MODEL, D_MODEL), lambda i: (0, 0)),
        ],
        out_specs=[
            pl.BlockSpec((ROW_TILE, POOL_WIDTH), lambda i: (i, 0)),
            pl.BlockSpec((ROW_TILE, MEM_WIDTH), lambda i: (i, 0)),
        ],
        out_shape=[
            jax.ShapeDtypeStruct((rows, POOL_WIDTH), jnp.float32),
            jax.ShapeDtypeStruct((rows, MEM_WIDTH), jnp.bfloat16),
        ],
        compiler_params=_params("arbitrary"),
        name="pool_in_proj",
    )(x2d, gain, w)


def _mem_kv_kernel(x_ref, g_ref, w_ref, o_ref):
    hn = _rms_normalize(x_ref[...], g_ref[...])
    o_ref[...] = jnp.dot(hn, w_ref[...], preferred_element_type=jnp.float32).astype(jnp.bfloat16)


def _mem_kv_proj(mem2d, gain, w):
    rows = mem2d.shape[0]
    n = w.shape[1]
    return pl.pallas_call(
        _mem_kv_kernel,
        grid=(1,),
        in_specs=[
            pl.BlockSpec((rows, D_MODEL), lambda i: (0, 0)),
            pl.BlockSpec((1, D_MODEL), lambda i: (0, 0)),
            pl.BlockSpec((D_MODEL, n), lambda i: (0, 0)),
        ],
        out_specs=pl.BlockSpec((rows, n), lambda i: (0, 0)),
        out_shape=jax.ShapeDtypeStruct((rows, n), jnp.bfloat16),
        compiler_params=_params("arbitrary"),
        name="mem_kv_proj",
    )(mem2d, gain, w)


FOX_PROJ_COLS = 3 * FOX_WIDTH + MEM_WIDTH
FOX_PROJ_TILE = 1024


def _fox_in_kernel(x_ref, g_ref, w_ref, cs_ref, wf_ref, o_ref, f_ref, hn_ref):
    j = pl.program_id(1)

    @pl.when(j == 0)
    def _():
        hn = _rms_normalize(x_ref[...], g_ref[...])
        hn_ref[...] = hn
        f_ref[...] = jnp.dot(hn, wf_ref[...], preferred_element_type=jnp.float32)

    res = jnp.dot(hn_ref[...], w_ref[...], preferred_element_type=jnp.float32)
    o_ref[...] = (res * cs_ref[...]).astype(jnp.bfloat16)


def _fox_in_proj(x2d, gain, w, colscale, wf):
    rows = x2d.shape[0]
    return pl.pallas_call(
        _fox_in_kernel,
        grid=(rows // ROW_TILE, FOX_PROJ_COLS // FOX_PROJ_TILE),
        in_specs=[
            pl.BlockSpec((ROW_TILE, D_MODEL), lambda i, j: (i, 0)),
            pl.BlockSpec((1, D_MODEL), lambda i, j: (0, 0)),
            pl.BlockSpec((D_MODEL, FOX_PROJ_TILE), lambda i, j: (0, j)),
            pl.BlockSpec((1, FOX_PROJ_TILE), lambda i, j: (0, j)),
            pl.BlockSpec((D_MODEL, LANES), lambda i, j: (0, 0)),
        ],
        out_specs=[
            pl.BlockSpec((ROW_TILE, FOX_PROJ_TILE), lambda i, j: (i, j)),
            pl.BlockSpec((ROW_TILE, LANES), lambda i, j: (i, 0)),
        ],
        out_shape=[
            jax.ShapeDtypeStruct((rows, FOX_PROJ_COLS), jnp.bfloat16),
            jax.ShapeDtypeStruct((rows, LANES), jnp.float32),
        ],
        scratch_shapes=[pltpu.VMEM((ROW_TILE, D_MODEL), jnp.bfloat16)],
        compiler_params=_params("arbitrary", "arbitrary"),
        name="fox_in_proj",
    )(x2d, gain, w, colscale, wf)


def _pool_mix_kernel(u_ref, w_ref, s_ref, y_ref, ext_ref):
    i = pl.program_id(1)
    tm = u_ref.shape[0]

    @pl.when(i == 0)
    def _():
        ext_ref[0:POOL_HALO, :] = jnp.zeros((POOL_HALO, POOL_WIDTH), jnp.float32)

    @pl.when(i > 0)
    def _():
        ext_ref[0:POOL_HALO, :] = ext_ref[tm:tm + POOL_HALO, :]

    ext_ref[POOL_HALO:, :] = u_ref[...]

    t = i * tm + lax.broadcasted_iota(jnp.int32, (tm, 1), 0)
    for g, win in enumerate(POOL_WINDOWS):
        c0 = g * POOL_GROUP_WIDTH
        c1 = c0 + POOL_GROUP_WIDTH
        tok = ext_ref[POOL_HALO:, c0:c1]
        wsum = tok
        for back in range(1, win):
            wsum = wsum + ext_ref[POOL_HALO - back:POOL_HALO - back + tm, c0:c1]
        count = jnp.minimum(t + 1, win).astype(jnp.float32)
        pooled = wsum / count - tok
        mixed = jnp.dot(pooled.astype(jnp.bfloat16), w_ref[g], preferred_element_type=jnp.float32)
        y_ref[:, c0:c1] = (mixed * s_ref[:, c0:c1]).astype(jnp.bfloat16)


def _pool_mix(u, w_grp, scale):
    b, s, _ = u.shape
    return pl.pallas_call(
        _pool_mix_kernel,
        grid=(b, s // ROW_TILE),
        in_specs=[
            pl.BlockSpec((None, ROW_TILE, POOL_WIDTH), lambda bi, i: (bi, i, 0)),
            pl.BlockSpec((POOL_GROUPS, POOL_GROUP_WIDTH, POOL_GROUP_WIDTH), lambda bi, i: (0, 0, 0)),
            pl.BlockSpec((1, POOL_WIDTH), lambda bi, i: (0, 0)),
        ],
        out_specs=pl.BlockSpec((None, ROW_TILE, POOL_WIDTH), lambda bi, i: (bi, i, 0)),
        out_shape=jax.ShapeDtypeStruct((b, s, POOL_WIDTH), jnp.bfloat16),
        scratch_shapes=[pltpu.VMEM((ROW_TILE + POOL_HALO, POOL_WIDTH), jnp.float32)],
        compiler_params=_params("arbitrary", "arbitrary"),
        name="pool_mix",
    )(u, w_grp, scale)


def _mem_attn_kernel(q_ref, k_ref, v_ref, o_ref):
    inv_sqrt = 1.0 / math.sqrt(HEAD_DIM)
    for h in range(MEM_HEADS):
        c0 = h * HEAD_DIM
        c1 = c0 + HEAD_DIM
        logits = lax.dot_general(q_ref[:, c0:c1], k_ref[:, c0:c1], (((1,), (1,)), ((), ())),
                                 preferred_element_type=jnp.float32)
        logits = logits * inv_sqrt
        m = jnp.max(logits, axis=-1, keepdims=True)
        e = jnp.exp(logits - m)
        p = e / jnp.sum(e, axis=-1, keepdims=True)
        out = jnp.dot(p.astype(jnp.bfloat16), v_ref[:, c0:c1], preferred_element_type=jnp.float32)
        o_ref[:, c0:c1] = out.astype(jnp.bfloat16)


def _mem_attn(q_arr, q_col_block, kv, seq):
    b = q_arr.shape[0]
    return pl.pallas_call(
        _mem_attn_kernel,
        grid=(b, seq // ROW_TILE),
        in_specs=[
            pl.BlockSpec((None, ROW_TILE, MEM_WIDTH), lambda bi, i: (bi, i, q_col_block)),
            pl.BlockSpec((None, MEM_LEN, MEM_WIDTH), lambda bi, i: (bi, 0, 0)),
            pl.BlockSpec((None, MEM_LEN, MEM_WIDTH), lambda bi, i: (bi, 0, 1)),
        ],
        out_specs=pl.BlockSpec((None, ROW_TILE, MEM_WIDTH), lambda bi, i: (bi, i, 0)),
        out_shape=jax.ShapeDtypeStruct((b, seq, MEM_WIDTH), jnp.bfloat16),
        compiler_params=_params("arbitrary", "arbitrary"),
        name="mem_attn",
    )(q_arr, kv, kv)


def _bf16_pieces(x):
    hi = x.astype(jnp.bfloat16)
    r1 = x - hi.astype(jnp.float32)
    mid = r1.astype(jnp.bfloat16)
    r2 = r1 - mid.astype(jnp.float32)
    lo = r2.astype(jnp.bfloat16)
    return hi.astype(jnp.float32), mid.astype(jnp.float32), lo.astype(jnp.float32)


def _fox_gate_kernel(f_ref, b_ref, qe_ref, ke_ref, carry_ref):
    i = pl.program_id(1)
    tm = f_ref.shape[0]

    @pl.when(i == 0)
    def _():
        carry_ref[...] = jnp.zeros_like(carry_ref)

    z = f_ref[...] + b_ref[...]
    log_f = jnp.minimum(z, 0.0) - jnp.log1p(jnp.exp(-jnp.abs(z)))
    row = lax.broadcasted_iota(jnp.int32, (tm, tm), 0)
    col = lax.broadcasted_iota(jnp.int32, (tm, tm), 1)
    tri = (col <= row).astype(jnp.float32)
    csum = jnp.dot(tri, log_f, preferred_element_type=jnp.float32,
                   precision=lax.Precision.HIGHEST) + carry_ref[...]
    carry_ref[...] = csum[tm - 1:tm, :]

    hi, mid, lo = _bf16_pieces(csum * LOG2E)
    lane = lax.broadcasted_iota(jnp.int32, (tm, LANES), 1)
    q_ones = jnp.where((lane >= 3) & (lane < 6), 1.0, 0.0)
    k_ones = jnp.where(lane < 3, 1.0, 0.0)
    for h in range(FOX_HEADS):
        h_hi, h_mid, h_lo = hi[:, h:h + 1], mid[:, h:h + 1], lo[:, h:h + 1]
        qe = jnp.where(lane == 0, h_hi, jnp.where(lane == 1, h_mid, jnp.where(lane == 2, h_lo, q_ones)))
        ke = jnp.where(lane == 3, -h_hi, jnp.where(lane == 4, -h_mid, jnp.where(lane == 5, -h_lo, k_ones)))
        qe_ref[h] = qe.astype(jnp.bfloat16)
        ke_ref[h] = ke.astype(jnp.bfloat16)


def _fox_gate_operands(f_logit, bias):
    b, s, _ = f_logit.shape
    out = jax.ShapeDtypeStruct((b, FOX_HEADS, s, LANES), jnp.bfloat16)
    spec = pl.BlockSpec((None, FOX_HEADS, CUMSUM_TILE, LANES), lambda bi, i: (bi, 0, i, 0))
    return pl.pallas_call(
        _fox_gate_kernel,
        grid=(b, s // CUMSUM_TILE),
        in_specs=[
            pl.BlockSpec((None, CUMSUM_TILE, LANES), lambda bi, i: (bi, i, 0)),
            pl.BlockSpec((1, LANES), lambda bi, i: (0, 0)),
        ],
        out_specs=[spec, spec],
        out_shape=[out, out],
        scratch_shapes=[pltpu.VMEM((1, LANES), jnp.float32)],
        compiler_params=_params("arbitrary", "arbitrary"),
        name="fox_gate_operands",
    )(f_logit, bias)


def _fox_attn_kernel(q_ref, qe_ref, k_ref, ke_ref, v_ref, o_ref):
    i = pl.program_id(2)
    tq, tk = FOX_TQ, FOX_TK
    per_tile = tq // tk
    q2 = jnp.concatenate([q_ref[...], qe_ref[...]], axis=1)
    lane = lax.broadcasted_iota(jnp.int32, (tk, LANES), 1)
    v_ext = jnp.where(lane == 0, 1.0, 0.0).astype(jnp.bfloat16)

    def block(j, carry, row0=0, masked=False):
        m, acc = carry
        ks = pl.ds(pl.multiple_of(j * tk, tk), tk)
        k2 = jnp.concatenate([k_ref[ks, :], ke_ref[ks, :]], axis=1)
        v2 = jnp.concatenate([v_ref[ks, :], v_ext], axis=1)
        s = lax.dot_general(q2[row0:], k2, (((1,), (1,)), ((), ())), preferred_element_type=jnp.float32)
        if masked:
            row = lax.broadcasted_iota(jnp.int32, s.shape, 0)
            col = lax.broadcasted_iota(jnp.int32, s.shape, 1)
            s = jnp.where(col <= row, s, -jnp.inf)
        m_old = m[row0:]
        m_new = jnp.maximum(m_old, jnp.max(s, axis=-1, keepdims=True))
        alpha = jnp.exp2(m_old - m_new)
        p = jnp.exp2(s - m_new).astype(jnp.bfloat16)
        acc_new = alpha * acc[row0:] + jnp.dot(p, v2, preferred_element_type=jnp.float32)
        if row0:
            m_new = jnp.concatenate([m[:row0], m_new], axis=0)
            acc_new = jnp.concatenate([acc[:row0], acc_new], axis=0)
        return m_new, acc_new

    carry = (jnp.full((tq, 1), NEG_BIG, jnp.float32),
             jnp.zeros((tq, 2 * HEAD_DIM), jnp.float32))
    n_full = i * per_tile
    done = 0
    for n in FOX_UNROLLS:
        def body(t, c, n=n, base=done):
            for u in range(n):
                c = block(base + t * n + u, c)
            return c
        trips = (n_full - done) // n
        carry = lax.fori_loop(0, trips, body, carry)
        done = done + trips * n

    for d in range(per_tile):
        carry = block(i * per_tile + d, carry, row0=d * tk, masked=True)
    _, acc = carry
    o_ref[...] = (acc[:, :HEAD_DIM] / acc[:, HEAD_DIM:HEAD_DIM + 1]).astype(jnp.bfloat16)


def _fox_attention(proj, qe, ke):
    b, s, _ = proj.shape
    assert FOX_TQ % FOX_TK == 0 and FOX_UNROLLS[-1] == FOX_TQ // FOX_TK
    return pl.pallas_call(
        _fox_attn_kernel,
        grid=(b, FOX_HEADS, s // FOX_TQ),
        in_specs=[
            pl.BlockSpec((None, FOX_TQ, HEAD_DIM), lambda bi, h, i: (bi, i, h)),
            pl.BlockSpec((None, None, FOX_TQ, LANES), lambda bi, h, i: (bi, h, i, 0)),
            pl.BlockSpec((None, s, HEAD_DIM), lambda bi, h, i: (bi, 0, FOX_HEADS + h)),
            pl.BlockSpec((None, None, s, LANES), lambda bi, h, i: (bi, h, 0, 0)),
            pl.BlockSpec((None, s, HEAD_DIM), lambda bi, h, i: (bi, 0, 2 * FOX_HEADS + h)),
        ],
        out_specs=pl.BlockSpec((None, FOX_TQ, HEAD_DIM), lambda bi, h, i: (bi, i, h)),
        out_shape=jax.ShapeDtypeStruct((b, s, FOX_WIDTH), jnp.bfloat16),
        compiler_params=_params("arbitrary", "arbitrary", "arbitrary"),
        name="fox_attention",
    )(proj, qe, proj, ke, proj)


def _out_proj_kernel(x_ref, ya_ref, yb_ref, wa_ref, wb_ref, g_ref, o_ref, hn_ref):
    acc = jnp.dot(ya_ref[...], wa_ref[...], preferred_element_type=jnp.float32)
    acc = acc + jnp.dot(yb_ref[...], wb_ref[...], preferred_element_type=jnp.float32)
    x_new = x_ref[...] + acc
    o_ref[...] = x_new
    hn_ref[...] = _rms_normalize(x_new, g_ref[...])


def _out_proj(x2d, y_mix, y_mem, w_mix, w_mem, ffn_gain):
    rows = x2d.shape[0]
    return pl.pallas_call(
        _out_proj_kernel,
        grid=(rows // ROW_TILE,),
        in_specs=[
            pl.BlockSpec((ROW_TILE, D_MODEL), lambda i: (i, 0)),
            pl.BlockSpec((ROW_TILE, POOL_WIDTH), lambda i: (i, 0)),
            pl.BlockSpec((ROW_TILE, MEM_WIDTH), lambda i: (i, 0)),
            pl.BlockSpec((POOL_WIDTH, D_MODEL), lambda i: (0, 0)),
            pl.BlockSpec((MEM_WIDTH, D_MODEL), lambda i: (0, 0)),
            pl.BlockSpec((1, D_MODEL), lambda i: (0, 0)),
        ],
        out_specs=[pl.BlockSpec((ROW_TILE, D_MODEL), lambda i: (i, 0)),
                   pl.BlockSpec((ROW_TILE, D_MODEL), lambda i: (i, 0))],
        out_shape=[jax.ShapeDtypeStruct((rows, D_MODEL), jnp.float32),
                   jax.ShapeDtypeStruct((rows, D_MODEL), jnp.bfloat16)],
        compiler_params=_params("arbitrary"),
        name="out_proj",
    )(x2d, y_mix, y_mem, w_mix, w_mem, ffn_gain)


def _ffn_kernel(x_ref, hn_ref, w1_ref, w2_ref, o_ref):
    @pl.when(pl.program_id(1) == 0)
    def _():
        o_ref[...] = x_ref[...]

    a = jnp.dot(hn_ref[...], w1_ref[...], preferred_element_type=jnp.float32)
    a = jnp.square(jnp.maximum(a, 0.0)).astype(jnp.bfloat16)
    o_ref[...] += jnp.dot(a, w2_ref[...], preferred_element_type=jnp.float32)


def _ffn(x2d, hn2d, w1, w2):
    rows = x2d.shape[0]
    return pl.pallas_call(
        _ffn_kernel,
        grid=(rows // ROW_TILE, FFN_HIDDEN // FFN_TILE),
        in_specs=[
            pl.BlockSpec((ROW_TILE, D_MODEL), lambda i, j: (i, 0)),
            pl.BlockSpec((ROW_TILE, D_MODEL), lambda i, j: (i, 0)),
            pl.BlockSpec((D_MODEL, FFN_TILE), lambda i, j: (0, j)),
            pl.BlockSpec((FFN_TILE, D_MODEL), lambda i, j: (j, 0)),
        ],
        out_specs=pl.BlockSpec((ROW_TILE, D_MODEL), lambda i, j: (i, 0)),
        out_shape=jax.ShapeDtypeStruct((rows, D_MODEL), jnp.float32),
        compiler_params=_params("arbitrary", "arbitrary"),
        name="ffn",
    )(x2d, hn2d, w1, w2)


def _final_norm_kernel(x_ref, g_ref, o_ref):
    x = x_ref[...]
    ms = jnp.mean(x * x, axis=-1, keepdims=True)
    o_ref[...] = (x * lax.rsqrt(ms + RMS_EPS)) * g_ref[...]


def _final_norm(x2d, gain):
    rows = x2d.shape[0]
    return pl.pallas_call(
        _final_norm_kernel,
        grid=(rows // ROW_TILE,),
        in_specs=[pl.BlockSpec((ROW_TILE, D_MODEL), lambda i: (i, 0)),
                  pl.BlockSpec((1, D_MODEL), lambda i: (0, 0))],
        out_specs=pl.BlockSpec((ROW_TILE, D_MODEL), lambda i: (i, 0)),
        out_shape=jax.ShapeDtypeStruct((rows, D_MODEL), jnp.float32),
        compiler_params=_params("arbitrary"),
        name="final_norm",
    )(x2d, gain)


def kernel(x, mem, norm_mix, norm_mem, pool_w_in, pool_w_grp, pool_scale, fox_w_in, fox_b_f,
           w_mem_kv, w_out, norm_ffn, w_ffn1, w_ffn2, norm_final):
    b, s, d = x.shape
    rows = b * s
    bf16 = jnp.bfloat16
    f32 = jnp.float32
    x2d = x.reshape(rows, d)
    mem2d = mem.reshape(b * MEM_LEN, d)

    def row(v):
        return v.reshape(1, -1).astype(f32)

    def mix_tail(x2d, layer, y_mix, q_arr, q_col_block):
        kv = _mem_kv_proj(mem2d, row(norm_mem[layer]), w_mem_kv[layer].astype(bf16))
        y_mem = _mem_attn(q_arr, q_col_block, kv.reshape(b, MEM_LEN, 2 * MEM_WIDTH), s)
        w_o = w_out[layer].astype(bf16)
        x2d, hn2d = _out_proj(x2d, y_mix.reshape(rows, POOL_WIDTH), y_mem.reshape(rows, MEM_WIDTH),
                              w_o[:POOL_WIDTH], w_o[POOL_WIDTH:], row(norm_ffn[layer]))
        return _ffn(x2d, hn2d, w_ffn1[layer].astype(bf16), w_ffn2[layer].astype(bf16))

    u, q_mem = _pool_in_proj(x2d, row(norm_mix[0]), pool_w_in[0].astype(bf16))
    y_mix = _pool_mix(u.reshape(b, s, POOL_WIDTH), pool_w_grp[0].astype(bf16), row(pool_scale[0]))
    x2d = mix_tail(x2d, 0, y_mix, q_mem.reshape(b, s, MEM_WIDTH), 0)

    w_in = fox_w_in[0]
    n_qkv = 3 * FOX_WIDTH
    w_main = jnp.concatenate([w_in[:, :n_qkv], w_in[:, n_qkv + FOX_HEADS:]], axis=1).astype(bf16)
    w_f = jnp.pad(w_in[:, n_qkv:n_qkv + FOX_HEADS], ((0, 0), (0, LANES - FOX_HEADS))).astype(bf16)
    colscale = jnp.concatenate([jnp.full((1, FOX_WIDTH), FOX_QSCALE, f32),
                                jnp.ones((1, FOX_PROJ_COLS - FOX_WIDTH), f32)], axis=1)
    proj, f_logit = _fox_in_proj(x2d, row(norm_mix[1]), w_main, colscale, w_f)
    proj = proj.reshape(b, s, FOX_PROJ_COLS)
    bias_f = jnp.pad(fox_b_f[0].astype(f32), (0, LANES - FOX_HEADS)).reshape(1, LANES)
    qe, ke = _fox_gate_operands(f_logit.reshape(b, s, LANES), bias_f)
    y_mix = _fox_attention(proj, qe, ke)
    x2d = mix_tail(x2d, 1, y_mix, proj, FOX_PROJ_COLS // MEM_WIDTH - 1)

    return _final_norm(x2d, row(norm_final)).reshape(b, s, d)
```

```python
import functools
import math

import jax
import jax.numpy as jnp
import numpy as np
from jax import lax
from jax.experimental import pallas as pl
from jax.experimental.pallas import tpu as pltpu

D_MODEL = 2048
HEAD_DIM = 128
MEM_LEN = 256
MEM_HEADS = 4
MEM_WIDTH = MEM_HEADS * HEAD_DIM
POOL_WIDTH = D_MODEL - MEM_WIDTH
POOL_GROUPS = 4
POOL_GROUP_WIDTH = POOL_WIDTH // POOL_GROUPS
POOL_WINDOWS = (2, 4, 8, 16)
POOL_HALO = 16
FOX_HEADS = POOL_WIDTH // HEAD_DIM
FOX_WIDTH = FOX_HEADS * HEAD_DIM
FOX_PROJ_COLS = 3 * FOX_WIDTH + MEM_WIDTH
FFN_HIDDEN = 4 * D_MODEL
RMS_EPS = 1e-6

LANES = 128
VMEM_LIMIT = 56 * 1024 * 1024

ROW_TILE = 512
FFN_TILE = 1024
FOX_TQ = 1024
FOX_TK = 512
FOX_UNROLLS = (8, 4, 2)
CUMSUM_TILE = 256

LOG2E = math.log2(math.e)
FOX_QSCALE = LOG2E / math.sqrt(HEAD_DIM)
NEG_BIG = -1e30

_RESIDENT = pl.Buffered(1)


def _params(*sem):
    return pltpu.CompilerParams(dimension_semantics=sem, vmem_limit_bytes=VMEM_LIMIT)


def _rms_scale(x, gain):
    ms = jnp.mean(x * x, axis=-1, keepdims=True)
    return (x * lax.rsqrt(ms + RMS_EPS)) * gain


def _rms_normalize(x, gain):
    return _rms_scale(x, gain).astype(jnp.bfloat16)


def _pool_in_kernel(x_ref, g_ref, w_ref, u_ref, qm_ref):
    hn = _rms_normalize(x_ref[...], g_ref[...])
    res = jnp.dot(hn, w_ref[...], preferred_element_type=jnp.float32)
    u_ref[...] = res[:, :POOL_WIDTH]
    qm_ref[...] = res[:, POOL_WIDTH:].astype(jnp.bfloat16)


def _pool_in_proj(x2d, gain, w_stack):
    rows = x2d.shape[0]
    return pl.pallas_call(
        _pool_in_kernel,
        grid=(rows // ROW_TILE,),
        in_specs=[
            pl.BlockSpec((ROW_TILE, D_MODEL), lambda i: (i, 0)),
            pl.BlockSpec((1, D_MODEL), lambda i: (0, 0)),
            pl.BlockSpec((None, D_MODEL, D_MODEL), lambda i: (0, 0, 0), pipeline_mode=_RESIDENT),
        ],
        out_specs=[
            pl.BlockSpec((ROW_TILE, POOL_WIDTH), lambda i: (i, 0)),
            pl.BlockSpec((ROW_TILE, MEM_WIDTH), lambda i: (i, 0)),
        ],
        out_shape=[
            jax.ShapeDtypeStruct((rows, POOL_WIDTH), jnp.float32),
            jax.ShapeDtypeStruct((rows, MEM_WIDTH), jnp.bfloat16),
        ],
        compiler_params=_params("arbitrary"),
        name="pool_in_proj",
    )(x2d, gain, w_stack)


def _mem_kv_kernel(x_ref, g_ref, w_ref, o_ref):
    hn = _rms_normalize(x_ref[...], g_ref[...])
    o_ref[...] = jnp.dot(hn, w_ref[...], preferred_element_type=jnp.float32).astype(jnp.bfloat16)


def _mem_kv_proj(mem2d, gain, w_stack, layer):
    rows = mem2d.shape[0]
    n = w_stack.shape[2]
    return pl.pallas_call(
        _mem_kv_kernel,
        grid=(1,),
        in_specs=[
            pl.BlockSpec((rows, D_MODEL), lambda i: (0, 0)),
            pl.BlockSpec((1, D_MODEL), lambda i: (0, 0)),
            pl.BlockSpec((None, D_MODEL, n), lambda i: (layer, 0, 0)),
        ],
        out_specs=pl.BlockSpec((rows, n), lambda i: (0, 0)),
        out_shape=jax.ShapeDtypeStruct((rows, n), jnp.bfloat16),
        compiler_params=_params("arbitrary"),
        name="mem_kv_proj",
    )(mem2d, gain, w_stack)


def _fox_in_kernel(x_ref, g_ref, wqkv_ref, wqm_ref, wf_ref, o_ref, f_ref):
    hn = _rms_normalize(x_ref[...], g_ref[...])
    f_ref[...] = jnp.dot(hn, wf_ref[...], preferred_element_type=jnp.float32)
    for part, scale in enumerate((FOX_QSCALE, None, None)):
        cols = slice(part * FOX_WIDTH, (part + 1) * FOX_WIDTH)
        res = jnp.dot(hn, wqkv_ref[:, cols], preferred_element_type=jnp.float32)
        if scale is not None:
            res = res * scale
        o_ref[:, cols] = res.astype(jnp.bfloat16)
    o_ref[:, 3 * FOX_WIDTH:] = jnp.dot(hn, wqm_ref[...], preferred_element_type=jnp.float32).astype(jnp.bfloat16)


def _fox_in_proj(x2d, gain, w_qkv, w_qm, w_f):
    rows = x2d.shape[0]
    return pl.pallas_call(
        _fox_in_kernel,
        grid=(rows // ROW_TILE,),
        in_specs=[
            pl.BlockSpec((ROW_TILE, D_MODEL), lambda i: (i, 0)),
            pl.BlockSpec((1, D_MODEL), lambda i: (0, 0)),
            pl.BlockSpec((D_MODEL, 3 * FOX_WIDTH), lambda i: (0, 0), pipeline_mode=_RESIDENT),
            pl.BlockSpec((D_MODEL, MEM_WIDTH), lambda i: (0, 0), pipeline_mode=_RESIDENT),
            pl.BlockSpec((D_MODEL, LANES), lambda i: (0, 0), pipeline_mode=_RESIDENT),
        ],
        out_specs=[
            pl.BlockSpec((ROW_TILE, FOX_PROJ_COLS), lambda i: (i, 0)),
            pl.BlockSpec((ROW_TILE, LANES), lambda i: (i, 0)),
        ],
        out_shape=[
            jax.ShapeDtypeStruct((rows, FOX_PROJ_COLS), jnp.bfloat16),
            jax.ShapeDtypeStruct((rows, LANES), jnp.float32),
        ],
        compiler_params=_params("arbitrary"),
        name="fox_in_proj",
    )(x2d, gain, w_qkv, w_qm, w_f)


def _pool_mix_kernel(u_ref, w_ref, s_ref, y_ref, ext_ref):
    i = pl.program_id(1)
    tm = u_ref.shape[0]

    @pl.when(i == 0)
    def _():
        ext_ref[0:POOL_HALO, :] = jnp.zeros((POOL_HALO, POOL_WIDTH), jnp.float32)

    @pl.when(i > 0)
    def _():
        ext_ref[0:POOL_HALO, :] = ext_ref[tm:tm + POOL_HALO, :]

    ext_ref[POOL_HALO:, :] = u_ref[...]

    t = i * tm + lax.broadcasted_iota(jnp.int32, (tm, 1), 0)
    for g, win in enumerate(POOL_WINDOWS):
        c0 = g * POOL_GROUP_WIDTH
        c1 = c0 + POOL_GROUP_WIDTH
        tok = ext_ref[POOL_HALO:, c0:c1]
        wsum = tok
        for back in range(1, win):
            wsum = wsum + ext_ref[POOL_HALO - back:POOL_HALO - back + tm, c0:c1]
        count = jnp.minimum(t + 1, win).astype(jnp.float32)
        pooled = wsum / count - tok
        mixed = jnp.dot(pooled.astype(jnp.bfloat16), w_ref[g], preferred_element_type=jnp.float32)
        y_ref[:, c0:c1] = (mixed * s_ref[:, c0:c1]).astype(jnp.bfloat16)


def _pool_mix(u, w_grp_stack, scale):
    b, s, _ = u.shape
    return pl.pallas_call(
        _pool_mix_kernel,
        grid=(b, s // ROW_TILE),
        in_specs=[
            pl.BlockSpec((None, ROW_TILE, POOL_WIDTH), lambda bi, i: (bi, i, 0)),
            pl.BlockSpec((None, POOL_GROUPS, POOL_GROUP_WIDTH, POOL_GROUP_WIDTH), lambda bi, i: (0, 0, 0, 0)),
            pl.BlockSpec((1, POOL_WIDTH), lambda bi, i: (0, 0)),
        ],
        out_specs=pl.BlockSpec((None, ROW_TILE, POOL_WIDTH), lambda bi, i: (bi, i, 0)),
        out_shape=jax.ShapeDtypeStruct((b, s, POOL_WIDTH), jnp.bfloat16),
        scratch_shapes=[pltpu.VMEM((ROW_TILE + POOL_HALO, POOL_WIDTH), jnp.float32)],
        compiler_params=_params("arbitrary", "arbitrary"),
        name="pool_mix",
    )(u, w_grp_stack, scale)


def _mem_attn_kernel(q_ref, k_ref, v_ref, o_ref):
    inv_sqrt = 1.0 / math.sqrt(HEAD_DIM)
    for h in range(MEM_HEADS):
        c0 = h * HEAD_DIM
        c1 = c0 + HEAD_DIM
        logits = lax.dot_general(q_ref[:, c0:c1], k_ref[:, c0:c1], (((1,), (1,)), ((), ())),
                                 preferred_element_type=jnp.float32)
        logits = logits * inv_sqrt
        m = jnp.max(logits, axis=-1, keepdims=True)
        e = jnp.exp(logits - m)
        p = e / jnp.sum(e, axis=-1, keepdims=True)
        out = jnp.dot(p.astype(jnp.bfloat16), v_ref[:, c0:c1], preferred_element_type=jnp.float32)
        o_ref[:, c0:c1] = out.astype(jnp.bfloat16)


def _mem_attn(q_arr, q_col_block, kv, seq):
    b = q_arr.shape[0]
    return pl.pallas_call(
        _mem_attn_kernel,
        grid=(b, seq // ROW_TILE),
        in_specs=[
            pl.BlockSpec((None, ROW_TILE, MEM_WIDTH), lambda bi, i: (bi, i, q_col_block)),
            pl.BlockSpec((None, MEM_LEN, MEM_WIDTH), lambda bi, i: (bi, 0, 0)),
            pl.BlockSpec((None, MEM_LEN, MEM_WIDTH), lambda bi, i: (bi, 0, 1)),
        ],
        out_specs=pl.BlockSpec((None, ROW_TILE, MEM_WIDTH), lambda bi, i: (bi, i, 0)),
        out_shape=jax.ShapeDtypeStruct((b, seq, MEM_WIDTH), jnp.bfloat16),
        compiler_params=_params("arbitrary", "arbitrary"),
        name="mem_attn",
    )(q_arr, kv, kv)


def _piece_selector():
    sel = np.zeros((3 * LANES, FOX_HEADS * LANES), np.float32)
    for h in range(FOX_HEADS):
        for piece in range(3):
            sel[piece * LANES + h, h * LANES + piece] = 1.0
            sel[piece * LANES + h, h * LANES + 3 + piece] = -1.0
    return jnp.asarray(sel, jnp.bfloat16)


def _bf16_pieces(x):
    hi = x.astype(jnp.bfloat16)
    r1 = x - hi.astype(jnp.float32)
    mid = r1.astype(jnp.bfloat16)
    r2 = r1 - mid.astype(jnp.float32)
    lo = r2.astype(jnp.bfloat16)
    return hi, mid, lo


def _fox_gate_kernel(f_ref, b_ref, sel_ref, qe_ref, ke_ref, carry_ref):
    i = pl.program_id(1)
    tm = f_ref.shape[0]

    @pl.when(i == 0)
    def _():
        carry_ref[...] = jnp.zeros_like(carry_ref)

    z = f_ref[...] + b_ref[...]
    log_f = jnp.minimum(z, 0.0) - jnp.log1p(jnp.exp(-jnp.abs(z)))
    row = lax.broadcasted_iota(jnp.int32, (tm, tm), 0)
    col = lax.broadcasted_iota(jnp.int32, (tm, tm), 1)
    tri = (col <= row).astype(jnp.float32)
    csum = jnp.dot(tri, log_f, preferred_element_type=jnp.float32,
                   precision=lax.Precision.HIGHEST) + carry_ref[...]
    carry_ref[...] = csum[tm - 1:tm, :]

    pieces = jnp.concatenate(_bf16_pieces(csum * LOG2E), axis=1)
    placed = jnp.dot(pieces, sel_ref[...], preferred_element_type=jnp.float32)
    lane = lax.broadcasted_iota(jnp.int32, (tm, LANES), 1)
    k_side = (lane >= 3) & (lane < 6)
    q_ones = jnp.where(k_side, 1.0, 0.0)
    k_ones = jnp.where(lane < 3, 1.0, 0.0)
    for h in range(FOX_HEADS):
        blk = placed[:, h * LANES:(h + 1) * LANES]
        qe_ref[h] = jnp.where(lane < 3, blk, q_ones).astype(jnp.bfloat16)
        ke_ref[h] = jnp.where(k_side, blk, k_ones).astype(jnp.bfloat16)


def _fox_gate_operands(f_logit, bias):
    b, s, _ = f_logit.shape
    out = jax.ShapeDtypeStruct((b, FOX_HEADS, s, LANES), jnp.bfloat16)
    spec = pl.BlockSpec((None, FOX_HEADS, CUMSUM_TILE, LANES), lambda bi, i: (bi, 0, i, 0))
    return pl.pallas_call(
        _fox_gate_kernel,
        grid=(b, s // CUMSUM_TILE),
        in_specs=[
            pl.BlockSpec((None, CUMSUM_TILE, LANES), lambda bi, i: (bi, i, 0)),
            pl.BlockSpec((1, LANES), lambda bi, i: (0, 0)),
            pl.BlockSpec((3 * LANES, FOX_HEADS * LANES), lambda bi, i: (0, 0)),
        ],
        out_specs=[spec, spec],
        out_shape=[out, out],
        scratch_shapes=[pltpu.VMEM((1, LANES), jnp.float32)],
        compiler_params=_params("arbitrary", "arbitrary"),
        name="fox_gate_operands",
    )(f_logit, bias, _piece_selector())


def _fox_attn_kernel(q_ref, qe_ref, k_ref, ke_ref, v_ref, o_ref):
    i = pl.program_id(2)
    tq, tk = FOX_TQ, FOX_TK
    per_tile = tq // tk
    q2 = jnp.concatenate([q_ref[...], qe_ref[...]], axis=1)
    lane = lax.broadcasted_iota(jnp.int32, (tk, LANES), 1)
    v_ext = jnp.where(lane == 0, 1.0, 0.0).astype(jnp.bfloat16)

    def block(j, carry, row0=0, masked=False):
        m, acc = carry
        ks = pl.ds(pl.multiple_of(j * tk, tk), tk)
        k2 = jnp.concatenate([k_ref[ks, :], ke_ref[ks, :]], axis=1)
        v2 = jnp.concatenate([v_ref[ks, :], v_ext], axis=1)
        s = lax.dot_general(q2[row0:], k2, (((1,), (1,)), ((), ())), preferred_element_type=jnp.float32)
        if masked:
            row = lax.broadcasted_iota(jnp.int32, s.shape, 0)
            col = lax.broadcasted_iota(jnp.int32, s.shape, 1)
            s = jnp.where(col <= row, s, -jnp.inf)
        m_old = m[row0:]
        m_new = jnp.maximum(m_old, jnp.max(s, axis=-1, keepdims=True))
        alpha = jnp.exp2(m_old - m_new)
        p = jnp.exp2(s - m_new).astype(jnp.bfloat16)
        acc_new = alpha * acc[row0:] + jnp.dot(p, v2, preferred_element_type=jnp.float32)
        if row0:
            m_new = jnp.concatenate([m[:row0], m_new], axis=0)
            acc_new = jnp.concatenate([acc[:row0], acc_new], axis=0)
        return m_new, acc_new

    carry = (jnp.full((tq, 1), NEG_BIG, jnp.float32),
             jnp.zeros((tq, 2 * HEAD_DIM), jnp.float32))
    n_full = i * per_tile
    done = 0
    for n in FOX_UNROLLS:
        def body(t, c, n=n, base=done):
            for u in range(n):
                c = block(base + t * n + u, c)
            return c
        trips = (n_full - done) // n
        carry = lax.fori_loop(0, trips, body, carry)
        done = done + trips * n

    for d in range(per_tile):
        carry = block(i * per_tile + d, carry, row0=d * tk, masked=True)
    _, acc = carry
    o_ref[...] = (acc[:, :HEAD_DIM] / acc[:, HEAD_DIM:HEAD_DIM + 1]).astype(jnp.bfloat16)


def _fox_attention(proj, qe, ke):
    b, s, _ = proj.shape
    assert FOX_TQ % FOX_TK == 0 and FOX_UNROLLS[-1] == FOX_TQ // FOX_TK
    return pl.pallas_call(
        _fox_attn_kernel,
        grid=(b, FOX_HEADS, s // FOX_TQ),
        in_specs=[
            pl.BlockSpec((None, FOX_TQ, HEAD_DIM), lambda bi, h, i: (bi, i, h)),
            pl.BlockSpec((None, None, FOX_TQ, LANES), lambda bi, h, i: (bi, h, i, 0)),
            pl.BlockSpec((None, s, HEAD_DIM), lambda bi, h, i: (bi, 0, FOX_HEADS + h)),
            pl.BlockSpec((None, None, s, LANES), lambda bi, h, i: (bi, h, 0, 0)),
            pl.BlockSpec((None, s, HEAD_DIM), lambda bi, h, i: (bi, 0, 2 * FOX_HEADS + h)),
        ],
        out_specs=pl.BlockSpec((None, FOX_TQ, HEAD_DIM), lambda bi, h, i: (bi, i, h)),
        out_shape=jax.ShapeDtypeStruct((b, s, FOX_WIDTH), jnp.bfloat16),
        compiler_params=_params("arbitrary", "arbitrary", "arbitrary"),
        name="fox_attention",
    )(proj, qe, proj, ke, proj)


def _out_proj_kernel(x_ref, ya_ref, yb_ref, wa_ref, wb_ref, g_ref, o_ref, hn_ref):
    acc = jnp.dot(ya_ref[...], wa_ref[...], preferred_element_type=jnp.float32)
    acc = acc + jnp.dot(yb_ref[...], wb_ref[...], preferred_element_type=jnp.float32)
    x_new = x_ref[...] + acc
    o_ref[...] = x_new
    hn_ref[...] = _rms_normalize(x_new, g_ref[...])


def _out_proj(x2d, y_mix, y_mem, w_stack, layer, ffn_gain):
    rows = x2d.shape[0]
    mem_row_block = POOL_WIDTH // MEM_WIDTH
    return pl.pallas_call(
        _out_proj_kernel,
        grid=(rows // ROW_TILE,),
        in_specs=[
            pl.BlockSpec((ROW_TILE, D_MODEL), lambda i: (i, 0)),
            pl.BlockSpec((ROW_TILE, POOL_WIDTH), lambda i: (i, 0)),
            pl.BlockSpec((ROW_TILE, MEM_WIDTH), lambda i: (i, 0)),
            pl.BlockSpec((None, POOL_WIDTH, D_MODEL), lambda i: (layer, 0, 0), pipeline_mode=_RESIDENT),
            pl.BlockSpec((None, MEM_WIDTH, D_MODEL), lambda i: (layer, mem_row_block, 0),
                         pipeline_mode=_RESIDENT),
            pl.BlockSpec((1, D_MODEL), lambda i: (0, 0)),
        ],
        out_specs=[pl.BlockSpec((ROW_TILE, D_MODEL), lambda i: (i, 0)),
                   pl.BlockSpec((ROW_TILE, D_MODEL), lambda i: (i, 0))],
        out_shape=[jax.ShapeDtypeStruct((rows, D_MODEL), jnp.float32),
                   jax.ShapeDtypeStruct((rows, D_MODEL), jnp.bfloat16)],
        compiler_params=_params("arbitrary"),
        name="out_proj",
    )(x2d, y_mix, y_mem, w_stack, w_stack, ffn_gain)


def _ffn_kernel(x_ref, hn_ref, w1_ref, w2_ref, g_ref, o_ref, *, final_norm):
    @pl.when(pl.program_id(1) == 0)
    def _():
        o_ref[...] = x_ref[...]

    a = jnp.dot(hn_ref[...], w1_ref[...], preferred_element_type=jnp.float32)
    a = jnp.square(jnp.maximum(a, 0.0)).astype(jnp.bfloat16)
    o_ref[...] += jnp.dot(a, w2_ref[...], preferred_element_type=jnp.float32)

    if final_norm:
        @pl.when(pl.program_id(1) == pl.num_programs(1) - 1)
        def _():
            o_ref[...] = _rms_scale(o_ref[...], g_ref[...])


def _ffn(x2d, hn2d, w1_stack, w2_stack, layer, final_gain, final_norm):
    rows = x2d.shape[0]
    return pl.pallas_call(
        functools.partial(_ffn_kernel, final_norm=final_norm),
        grid=(rows // ROW_TILE, FFN_HIDDEN // FFN_TILE),
        in_specs=[
            pl.BlockSpec((ROW_TILE, D_MODEL), lambda i, j: (i, 0)),
            pl.BlockSpec((ROW_TILE, D_MODEL), lambda i, j: (i, 0)),
            pl.BlockSpec((None, D_MODEL, FFN_TILE), lambda i, j: (layer, 0, j)),
            pl.BlockSpec((None, FFN_TILE, D_MODEL), lambda i, j: (layer, j, 0)),
            pl.BlockSpec((1, D_MODEL), lambda i, j: (0, 0)),
        ],
        out_specs=pl.BlockSpec((ROW_TILE, D_MODEL), lambda i, j: (i, 0)),
        out_shape=jax.ShapeDtypeStruct((rows, D_MODEL), jnp.float32),
        compiler_params=_params("arbitrary", "arbitrary"),
        name="ffn_final" if final_norm else "ffn",
    )(x2d, hn2d, w1_stack, w2_stack, final_gain)


def kernel(x, mem, norm_mix, norm_mem, pool_w_in, pool_w_grp, pool_scale, fox_w_in, fox_b_f,
           w_mem_kv, w_out, norm_ffn, w_ffn1, w_ffn2, norm_final):
    b, s, d = x.shape
    rows = b * s
    depth = w_out.shape[0]
    bf16 = jnp.bfloat16
    f32 = jnp.float32
    x2d = x.reshape(rows, d)
    mem2d = mem.reshape(b * MEM_LEN, d)
    w_mem_kv_b, w_out_b = w_mem_kv.astype(bf16), w_out.astype(bf16)
    w_ffn1_b, w_ffn2_b = w_ffn1.astype(bf16), w_ffn2.astype(bf16)

    def row(v):
        return v.reshape(1, -1).astype(f32)

    def mix_tail(x2d, layer, y_mix, q_arr, q_col_block):
        kv = _mem_kv_proj(mem2d, row(norm_mem[layer]), w_mem_kv_b, layer)
        y_mem = _mem_attn(q_arr, q_col_block, kv.reshape(b, MEM_LEN, 2 * MEM_WIDTH), s)
        x2d, hn2d = _out_proj(x2d, y_mix.reshape(rows, POOL_WIDTH), y_mem.reshape(rows, MEM_WIDTH),
                              w_out_b, layer, row(norm_ffn[layer]))
        return _ffn(x2d, hn2d, w_ffn1_b, w_ffn2_b, layer, row(norm_final), layer == depth - 1)

    u, q_mem = _pool_in_proj(x2d, row(norm_mix[0]), pool_w_in.astype(bf16))
    y_mix = _pool_mix(u.reshape(b, s, POOL_WIDTH), pool_w_grp.astype(bf16), row(pool_scale[0]))
    x2d = mix_tail(x2d, 0, y_mix, q_mem.reshape(b, s, MEM_WIDTH), 0)

    w_in = fox_w_in[0]
    n_qkv = 3 * FOX_WIDTH
    w_qkv = w_in[:, :n_qkv].astype(bf16)
    w_qm = w_in[:, n_qkv + FOX_HEADS:].astype(bf16)
    w_f = jnp.pad(w_in[:, n_qkv:n_qkv + FOX_HEADS], ((0, 0), (0, LANES - FOX_HEADS))).astype(bf16)
    proj, f_logit = _fox_in_proj(x2d, row(norm_mix[1]), w_qkv, w_qm, w_f)
    proj = proj.reshape(b, s, FOX_PROJ_COLS)
    bias_f = jnp.pad(fox_b_f[0].astype(f32), (0, LANES - FOX_HEADS)).reshape(1, LANES)
    qe, ke = _fox_gate_operands(f_logit.reshape(b, s, LANES), bias_f)
    y_mix = _fox_attention(proj, qe, ke)
    x2d = mix_tail(x2d, 1, y_mix, proj, FOX_PROJ_COLS // MEM_WIDTH - 1)

    return x2d.reshape(b, s, d)
```

```python
import functools
import math

import jax
import jax.numpy as jnp
import numpy as np
from jax import lax
from jax.experimental import pallas as pl
from jax.experimental.pallas import tpu as pltpu

D_MODEL = 2048
HEAD_DIM = 128
MEM_LEN = 256
MEM_HEADS = 4
MEM_WIDTH = MEM_HEADS * HEAD_DIM
POOL_WIDTH = D_MODEL - MEM_WIDTH
POOL_GROUPS = 4
POOL_GROUP_WIDTH = POOL_WIDTH // POOL_GROUPS
POOL_WINDOWS = (2, 4, 8, 16)
POOL_HALO = 16
FOX_HEADS = POOL_WIDTH // HEAD_DIM
FOX_WIDTH = FOX_HEADS * HEAD_DIM
FOX_PROJ_COLS = 3 * FOX_WIDTH + MEM_WIDTH
FFN_HIDDEN = 4 * D_MODEL
RMS_EPS = 1e-6

LANES = 128
VMEM_LIMIT = 56 * 1024 * 1024

ROW_TILE = 512
FFN_TILE = 1024
FOX_TQ = 1024
FOX_TK = 512
FOX_UNROLL = 8
CUMSUM_TILE = 256

LOG2E = math.log2(math.e)
FOX_QSCALE = LOG2E / math.sqrt(HEAD_DIM)
NEG_BIG = -1e30

_RESIDENT = pl.Buffered(1)


def _params(*sem):
    return pltpu.CompilerParams(dimension_semantics=sem, vmem_limit_bytes=VMEM_LIMIT)


def _rms_scale(x, gain):
    ms = jnp.mean(x * x, axis=-1, keepdims=True)
    return (x * lax.rsqrt(ms + RMS_EPS)) * gain


def _rms_normalize(x, gain):
    return _rms_scale(x, gain).astype(jnp.bfloat16)


def _pool_in_kernel(x_ref, g_ref, w_ref, u_ref, qm_ref):
    hn = _rms_normalize(x_ref[...], g_ref[...])
    res = jnp.dot(hn, w_ref[...], preferred_element_type=jnp.float32)
    u_ref[...] = res[:, :POOL_WIDTH]
    qm_ref[...] = res[:, POOL_WIDTH:].astype(jnp.bfloat16)


def _pool_in_proj(x2d, gain, w_stack):
    rows = x2d.shape[0]
    return pl.pallas_call(
        _pool_in_kernel,
        grid=(rows // ROW_TILE,),
        in_specs=[
            pl.BlockSpec((ROW_TILE, D_MODEL), lambda i: (i, 0)),
            pl.BlockSpec((1, D_MODEL), lambda i: (0, 0)),
            pl.BlockSpec((None, D_MODEL, D_MODEL), lambda i: (0, 0, 0), pipeline_mode=_RESIDENT),
        ],
        out_specs=[
            pl.BlockSpec((ROW_TILE, POOL_WIDTH), lambda i: (i, 0)),
            pl.BlockSpec((ROW_TILE, MEM_WIDTH), lambda i: (i, 0)),
        ],
        out_shape=[
            jax.ShapeDtypeStruct((rows, POOL_WIDTH), jnp.float32),
            jax.ShapeDtypeStruct((rows, MEM_WIDTH), jnp.bfloat16),
        ],
        compiler_params=_params("arbitrary"),
        name="pool_in_proj",
    )(x2d, gain, w_stack)


def _mem_kv_kernel(x_ref, g_ref, w_ref, o_ref):
    hn = _rms_normalize(x_ref[...], g_ref[...])
    o_ref[...] = jnp.dot(hn, w_ref[...], preferred_element_type=jnp.float32).astype(jnp.bfloat16)


def _mem_kv_proj(mem2d, gain, w_stack, layer):
    rows = mem2d.shape[0]
    n = w_stack.shape[2]
    return pl.pallas_call(
        _mem_kv_kernel,
        grid=(1,),
        in_specs=[
            pl.BlockSpec((rows, D_MODEL), lambda i: (0, 0)),
            pl.BlockSpec((1, D_MODEL), lambda i: (0, 0)),
            pl.BlockSpec((None, D_MODEL, n), lambda i: (layer, 0, 0)),
        ],
        out_specs=pl.BlockSpec((rows, n), lambda i: (0, 0)),
        out_shape=jax.ShapeDtypeStruct((rows, n), jnp.bfloat16),
        compiler_params=_params("arbitrary"),
        name="mem_kv_proj",
    )(mem2d, gain, w_stack)


def _fox_in_kernel(x_ref, g_ref, wqkv_ref, wqm_ref, wf_ref, o_ref, f_ref):
    hn = _rms_normalize(x_ref[...], g_ref[...])
    f_ref[...] = jnp.dot(hn, wf_ref[...], preferred_element_type=jnp.float32)
    for part, scale in enumerate((FOX_QSCALE, None, None)):
        cols = slice(part * FOX_WIDTH, (part + 1) * FOX_WIDTH)
        res = jnp.dot(hn, wqkv_ref[:, cols], preferred_element_type=jnp.float32)
        if scale is not None:
            res = res * scale
        o_ref[:, cols] = res.astype(jnp.bfloat16)
    o_ref[:, 3 * FOX_WIDTH:] = jnp.dot(hn, wqm_ref[...], preferred_element_type=jnp.float32).astype(jnp.bfloat16)


def _fox_in_proj(x2d, gain, w_qkv, w_qm, w_f):
    rows = x2d.shape[0]
    return pl.pallas_call(
        _fox_in_kernel,
        grid=(rows // ROW_TILE,),
        in_specs=[
            pl.BlockSpec((ROW_TILE, D_MODEL), lambda i: (i, 0)),
            pl.BlockSpec((1, D_MODEL), lambda i: (0, 0)),
            pl.BlockSpec((D_MODEL, 3 * FOX_WIDTH), lambda i: (0, 0), pipeline_mode=_RESIDENT),
            pl.BlockSpec((D_MODEL, MEM_WIDTH), lambda i: (0, 0), pipeline_mode=_RESIDENT),
            pl.BlockSpec((D_MODEL, LANES), lambda i: (0, 0), pipeline_mode=_RESIDENT),
        ],
        out_specs=[
            pl.BlockSpec((ROW_TILE, FOX_PROJ_COLS), lambda i: (i, 0)),
            pl.BlockSpec((ROW_TILE, LANES), lambda i: (i, 0)),
        ],
        out_shape=[
            jax.ShapeDtypeStruct((rows, FOX_PROJ_COLS), jnp.bfloat16),
            jax.ShapeDtypeStruct((rows, LANES), jnp.float32),
        ],
        compiler_params=_params("arbitrary"),
        name="fox_in_proj",
    )(x2d, gain, w_qkv, w_qm, w_f)


def _pool_mix_kernel(u_ref, w_ref, s_ref, y_ref, ext_ref):
    i = pl.program_id(1)
    tm = u_ref.shape[0]

    @pl.when(i == 0)
    def _():
        ext_ref[0:POOL_HALO, :] = jnp.zeros((POOL_HALO, POOL_WIDTH), jnp.float32)

    @pl.when(i > 0)
    def _():
        ext_ref[0:POOL_HALO, :] = ext_ref[tm:tm + POOL_HALO, :]

    ext_ref[POOL_HALO:, :] = u_ref[...]

    t = i * tm + lax.broadcasted_iota(jnp.int32, (tm, 1), 0)
    for g, win in enumerate(POOL_WINDOWS):
        c0 = g * POOL_GROUP_WIDTH
        c1 = c0 + POOL_GROUP_WIDTH
        tok = ext_ref[POOL_HALO:, c0:c1]
        wsum = tok
        for back in range(1, win):
            wsum = wsum + ext_ref[POOL_HALO - back:POOL_HALO - back + tm, c0:c1]
        count = jnp.minimum(t + 1, win).astype(jnp.float32)
        pooled = wsum / count - tok
        mixed = jnp.dot(pooled.astype(jnp.bfloat16), w_ref[g], preferred_element_type=jnp.float32)
        y_ref[:, c0:c1] = (mixed * s_ref[:, c0:c1]).astype(jnp.bfloat16)


def _pool_mix(u, w_grp_stack, scale):
    b, s, _ = u.shape
    return pl.pallas_call(
        _pool_mix_kernel,
        grid=(b, s // ROW_TILE),
        in_specs=[
            pl.BlockSpec((None, ROW_TILE, POOL_WIDTH), lambda bi, i: (bi, i, 0)),
            pl.BlockSpec((None, POOL_GROUPS, POOL_GROUP_WIDTH, POOL_GROUP_WIDTH), lambda bi, i: (0, 0, 0, 0)),
            pl.BlockSpec((1, POOL_WIDTH), lambda bi, i: (0, 0)),
        ],
        out_specs=pl.BlockSpec((None, ROW_TILE, POOL_WIDTH), lambda bi, i: (bi, i, 0)),
        out_shape=jax.ShapeDtypeStruct((b, s, POOL_WIDTH), jnp.bfloat16),
        scratch_shapes=[pltpu.VMEM((ROW_TILE + POOL_HALO, POOL_WIDTH), jnp.float32)],
        compiler_params=_params("arbitrary", "arbitrary"),
        name="pool_mix",
    )(u, w_grp_stack, scale)


def _mem_attn_kernel(q_ref, k_ref, v_ref, o_ref):
    inv_sqrt = 1.0 / math.sqrt(HEAD_DIM)
    for h in range(MEM_HEADS):
        c0 = h * HEAD_DIM
        c1 = c0 + HEAD_DIM
        logits = lax.dot_general(q_ref[:, c0:c1], k_ref[:, c0:c1], (((1,), (1,)), ((), ())),
                                 preferred_element_type=jnp.float32)
        logits = logits * inv_sqrt
        m = jnp.max(logits, axis=-1, keepdims=True)
        e = jnp.exp(logits - m)
        p = e / jnp.sum(e, axis=-1, keepdims=True)
        out = jnp.dot(p.astype(jnp.bfloat16), v_ref[:, c0:c1], preferred_element_type=jnp.float32)
        o_ref[:, c0:c1] = out.astype(jnp.bfloat16)


def _mem_attn(q_arr, q_col_block, kv, seq):
    b = q_arr.shape[0]
    return pl.pallas_call(
        _mem_attn_kernel,
        grid=(b, seq // ROW_TILE),
        in_specs=[
            pl.BlockSpec((None, ROW_TILE, MEM_WIDTH), lambda bi, i: (bi, i, q_col_block)),
            pl.BlockSpec((None, MEM_LEN, MEM_WIDTH), lambda bi, i: (bi, 0, 0)),
            pl.BlockSpec((None, MEM_LEN, MEM_WIDTH), lambda bi, i: (bi, 0, 1)),
        ],
        out_specs=pl.BlockSpec((None, ROW_TILE, MEM_WIDTH), lambda bi, i: (bi, i, 0)),
        out_shape=jax.ShapeDtypeStruct((b, seq, MEM_WIDTH), jnp.bfloat16),
        compiler_params=_params("arbitrary", "arbitrary"),
        name="mem_attn",
    )(q_arr, kv, kv)


def _piece_selector():
    sel = np.zeros((3 * LANES, FOX_HEADS * LANES), np.float32)
    for h in range(FOX_HEADS):
        for piece in range(3):
            sel[piece * LANES + h, h * LANES + piece] = 1.0
            sel[piece * LANES + h, h * LANES + 3 + piece] = -1.0
    return jnp.asarray(sel, jnp.bfloat16)


def _bf16_pieces(x):
    hi = x.astype(jnp.bfloat16)
    r1 = x - hi.astype(jnp.float32)
    mid = r1.astype(jnp.bfloat16)
    r2 = r1 - mid.astype(jnp.float32)
    lo = r2.astype(jnp.bfloat16)
    return hi, mid, lo


def _fox_gate_kernel(f_ref, b_ref, sel_ref, qe_ref, ke_ref, carry_ref):
    i = pl.program_id(1)
    tm = f_ref.shape[0]

    @pl.when(i == 0)
    def _():
        carry_ref[...] = jnp.zeros_like(carry_ref)

    z = f_ref[...] + b_ref[...]
    log_f = jnp.minimum(z, 0.0) - jnp.log1p(jnp.exp(-jnp.abs(z)))
    row = lax.broadcasted_iota(jnp.int32, (tm, tm), 0)
    col = lax.broadcasted_iota(jnp.int32, (tm, tm), 1)
    tri = (col <= row).astype(jnp.float32)
    csum = jnp.dot(tri, log_f, preferred_element_type=jnp.float32,
                   precision=lax.Precision.HIGHEST) + carry_ref[...]
    carry_ref[...] = csum[tm - 1:tm, :]

    pieces = jnp.concatenate(_bf16_pieces(csum * LOG2E), axis=1)
    placed = jnp.dot(pieces, sel_ref[...], preferred_element_type=jnp.float32)
    lane = lax.broadcasted_iota(jnp.int32, (tm, LANES), 1)
    k_side = (lane >= 3) & (lane < 6)
    q_ones = jnp.where(k_side, 1.0, 0.0)
    k_ones = jnp.where(lane < 3, 1.0, 0.0)
    for h in range(FOX_HEADS):
        blk = placed[:, h * LANES:(h + 1) * LANES]
        qe_ref[h] = jnp.where(lane < 3, blk, q_ones).astype(jnp.bfloat16)
        ke_ref[h] = jnp.where(k_side, blk, k_ones).astype(jnp.bfloat16)


def _fox_gate_operands(f_logit, bias):
    b, s, _ = f_logit.shape
    out = jax.ShapeDtypeStruct((b, FOX_HEADS, s, LANES), jnp.bfloat16)
    spec = pl.BlockSpec((None, FOX_HEADS, CUMSUM_TILE, LANES), lambda bi, i: (bi, 0, i, 0))
    return pl.pallas_call(
        _fox_gate_kernel,
        grid=(b, s // CUMSUM_TILE),
        in_specs=[
            pl.BlockSpec((None, CUMSUM_TILE, LANES), lambda bi, i: (bi, i, 0)),
            pl.BlockSpec((1, LANES), lambda bi, i: (0, 0)),
            pl.BlockSpec((3 * LANES, FOX_HEADS * LANES), lambda bi, i: (0, 0)),
        ],
        out_specs=[spec, spec],
        out_shape=[out, out],
        scratch_shapes=[pltpu.VMEM((1, LANES), jnp.float32)],
        compiler_params=_params("arbitrary", "arbitrary"),
        name="fox_gate_operands",
    )(f_logit, bias, _piece_selector())


def _fox_attn_kernel(q_ref, qe_ref, k_ref, ke_ref, v_ref, o_ref):
    i = pl.program_id(2)
    tq, tk = FOX_TQ, FOX_TK
    per_tile = tq // tk
    q2 = jnp.concatenate([q_ref[...], qe_ref[...]], axis=1)
    lane = lax.broadcasted_iota(jnp.int32, (tk, LANES), 1)
    v_ext = jnp.where(lane == 0, 1.0, 0.0).astype(jnp.bfloat16)

    def block(j, carry, row0=0, masked=False):
        m, acc = carry
        ks = pl.ds(pl.multiple_of(j * tk, tk), tk)
        k2 = jnp.concatenate([k_ref[ks, :], ke_ref[ks, :]], axis=1)
        v2 = jnp.concatenate([v_ref[ks, :], v_ext], axis=1)
        s = lax.dot_general(q2[row0:], k2, (((1,), (1,)), ((), ())), preferred_element_type=jnp.float32)
        if masked:
            row = lax.broadcasted_iota(jnp.int32, s.shape, 0)
            col = lax.broadcasted_iota(jnp.int32, s.shape, 1)
            s = jnp.where(col <= row, s, -jnp.inf)
        m_old = m[row0:]
        m_new = jnp.maximum(m_old, jnp.max(s, axis=-1, keepdims=True))
        alpha = jnp.exp2(m_old - m_new)
        p = jnp.exp2(s - m_new).astype(jnp.bfloat16)
        acc_new = alpha * acc[row0:] + jnp.dot(p, v2, preferred_element_type=jnp.float32)
        if row0:
            m_new = jnp.concatenate([m[:row0], m_new], axis=0)
            acc_new = jnp.concatenate([acc[:row0], acc_new], axis=0)
        return m_new, acc_new

    carry = (jnp.full((tq, 1), NEG_BIG, jnp.float32),
             jnp.zeros((tq, 2 * HEAD_DIM), jnp.float32))
    n_full = i * per_tile

    def body(t, c):
        for u in range(FOX_UNROLL):
            c = block(t * FOX_UNROLL + u, c)
        return c
    groups = n_full // FOX_UNROLL
    carry = lax.fori_loop(0, groups, body, carry)

    def tail(c, n_rest):
        base = groups * FOX_UNROLL
        for u in range(n_rest):
            c = block(base + u, c)
        for d in range(per_tile):
            c = block(i * per_tile + d, c, row0=d * tk, masked=True)
        _, acc = c
        o_ref[...] = (acc[:, :HEAD_DIM] / acc[:, HEAD_DIM:HEAD_DIM + 1]).astype(jnp.bfloat16)

    rest = n_full - groups * FOX_UNROLL
    for n_rest in range(0, FOX_UNROLL, per_tile):
        pl.when(rest == n_rest)(functools.partial(tail, carry, n_rest))


def _fox_attention(proj, qe, ke):
    b, s, _ = proj.shape
    assert FOX_TQ % FOX_TK == 0 and FOX_UNROLL % (FOX_TQ // FOX_TK) == 0
    return pl.pallas_call(
        _fox_attn_kernel,
        grid=(b, FOX_HEADS, s // FOX_TQ),
        in_specs=[
            pl.BlockSpec((None, FOX_TQ, HEAD_DIM), lambda bi, h, i: (bi, i, h)),
            pl.BlockSpec((None, None, FOX_TQ, LANES), lambda bi, h, i: (bi, h, i, 0)),
            pl.BlockSpec((None, s, HEAD_DIM), lambda bi, h, i: (bi, 0, FOX_HEADS + h)),
            pl.BlockSpec((None, None, s, LANES), lambda bi, h, i: (bi, h, 0, 0)),
            pl.BlockSpec((None, s, HEAD_DIM), lambda bi, h, i: (bi, 0, 2 * FOX_HEADS + h)),
        ],
        out_specs=pl.BlockSpec((None, FOX_TQ, HEAD_DIM), lambda bi, h, i: (bi, i, h)),
        out_shape=jax.ShapeDtypeStruct((b, s, FOX_WIDTH), jnp.bfloat16),
        compiler_params=_params("arbitrary", "arbitrary", "arbitrary"),
        name="fox_attention",
    )(proj, qe, proj, ke, proj)


def _out_proj_kernel(x_ref, ya_ref, yb_ref, wa_ref, wb_ref, g_ref, o_ref, hn_ref):
    acc = jnp.dot(ya_ref[...], wa_ref[...], preferred_element_type=jnp.float32)
    acc = acc + jnp.dot(yb_ref[...], wb_ref[...], preferred_element_type=jnp.float32)
    x_new = x_ref[...] + acc
    o_ref[...] = x_new
    hn_ref[...] = _rms_normalize(x_new, g_ref[...])


def _out_proj(x2d, y_mix, y_mem, w_stack, layer, ffn_gain):
    rows = x2d.shape[0]
    mem_row_block = POOL_WIDTH // MEM_WIDTH
    return pl.pallas_call(
        _out_proj_kernel,
        grid=(rows // ROW_TILE,),
        in_specs=[
            pl.BlockSpec((ROW_TILE, D_MODEL), lambda i: (i, 0)),
            pl.BlockSpec((ROW_TILE, POOL_WIDTH), lambda i: (i, 0)),
            pl.BlockSpec((ROW_TILE, MEM_WIDTH), lambda i: (i, 0)),
            pl.BlockSpec((None, POOL_WIDTH, D_MODEL), lambda i: (layer, 0, 0), pipeline_mode=_RESIDENT),
            pl.BlockSpec((None, MEM_WIDTH, D_MODEL), lambda i: (layer, mem_row_block, 0),
                         pipeline_mode=_RESIDENT),
            pl.BlockSpec((1, D_MODEL), lambda i: (0, 0)),
        ],
        out_specs=[pl.BlockSpec((ROW_TILE, D_MODEL), lambda i: (i, 0)),
                   pl.BlockSpec((ROW_TILE, D_MODEL), lambda i: (i, 0))],
        out_shape=[jax.ShapeDtypeStruct((rows, D_MODEL), jnp.float32),
                   jax.ShapeDtypeStruct((rows, D_MODEL), jnp.bfloat16)],
        compiler_params=_params("arbitrary"),
        name="out_proj",
    )(x2d, y_mix, y_mem, w_stack, w_stack, ffn_gain)


def _ffn_kernel(x_ref, hn_ref, w1_ref, w2_ref, g_ref, o_ref, *, final_norm):
    @pl.when(pl.program_id(1) == 0)
    def _():
        o_ref[...] = x_ref[...]

    a = jnp.dot(hn_ref[...], w1_ref[...], preferred_element_type=jnp.float32)
    a = jnp.square(jnp.maximum(a, 0.0)).astype(jnp.bfloat16)
    o_ref[...] += jnp.dot(a, w2_ref[...], preferred_element_type=jnp.float32)

    if final_norm:
        @pl.when(pl.program_id(1) == pl.num_programs(1) - 1)
        def _():
            o_ref[...] = _rms_scale(o_ref[...], g_ref[...])


def _ffn(x2d, hn2d, w1_stack, w2_stack, layer, final_gain, final_norm):
    rows = x2d.shape[0]
    return pl.pallas_call(
        functools.partial(_ffn_kernel, final_norm=final_norm),
        grid=(rows // ROW_TILE, FFN_HIDDEN // FFN_TILE),
        in_specs=[
            pl.BlockSpec((ROW_TILE, D_MODEL), lambda i, j: (i, 0)),
            pl.BlockSpec((ROW_TILE, D_MODEL), lambda i, j: (i, 0)),
            pl.BlockSpec((None, D_MODEL, FFN_TILE), lambda i, j: (layer, 0, j)),
            pl.BlockSpec((None, FFN_TILE, D_MODEL), lambda i, j: (layer, j, 0)),
            pl.BlockSpec((1, D_MODEL), lambda i, j: (0, 0)),
        ],
        out_specs=pl.BlockSpec((ROW_TILE, D_MODEL), lambda i, j: (i, 0)),
        out_shape=jax.ShapeDtypeStruct((rows, D_MODEL), jnp.float32),
        compiler_params=_params("arbitrary", "arbitrary"),
        name="ffn_final" if final_norm else "ffn",
    )(x2d, hn2d, w1_stack, w2_stack, final_gain)


def kernel(x, mem, norm_mix, norm_mem, pool_w_in, pool_w_grp, pool_scale, fox_w_in, fox_b_f,
           w_mem_kv, w_out, norm_ffn, w_ffn1, w_ffn2, norm_final):
    b, s, d = x.shape
    rows = b * s
    depth = w_out.shape[0]
    bf16 = jnp.bfloat16
    f32 = jnp.float32
    x2d = x.reshape(rows, d)
    mem2d = mem.reshape(b * MEM_LEN, d)
    w_mem_kv_b, w_out_b = w_mem_kv.astype(bf16), w_out.astype(bf16)
    w_ffn1_b, w_ffn2_b = w_ffn1.astype(bf16), w_ffn2.astype(bf16)

    def row(v):
        return v.reshape(1, -1).astype(f32)

    def mix_tail(x2d, layer, y_mix, q_arr, q_col_block):
        kv = _mem_kv_proj(mem2d, row(norm_mem[layer]), w_mem_kv_b, layer)
        y_mem = _mem_attn(q_arr, q_col_block, kv.reshape(b, MEM_LEN, 2 * MEM_WIDTH), s)
        x2d, hn2d = _out_proj(x2d, y_mix.reshape(rows, POOL_WIDTH), y_mem.reshape(rows, MEM_WIDTH),
                              w_out_b, layer, row(norm_ffn[layer]))
        return _ffn(x2d, hn2d, w_ffn1_b, w_ffn2_b, layer, row(norm_final), layer == depth - 1)

    u, q_mem = _pool_in_proj(x2d, row(norm_mix[0]), pool_w_in.astype(bf16))
    y_mix = _pool_mix(u.reshape(b, s, POOL_WIDTH), pool_w_grp.astype(bf16), row(pool_scale[0]))
    x2d = mix_tail(x2d, 0, y_mix, q_mem.reshape(b, s, MEM_WIDTH), 0)

    w_in = fox_w_in[0]
    n_qkv = 3 * FOX_WIDTH
    w_qkv = w_in[:, :n_qkv].astype(bf16)
    w_qm = w_in[:, n_qkv + FOX_HEADS:].astype(bf16)
    w_f = jnp.pad(w_in[:, n_qkv:n_qkv + FOX_HEADS], ((0, 0), (0, LANES - FOX_HEADS))).astype(bf16)
    proj, f_logit = _fox_in_proj(x2d, row(norm_mix[1]), w_qkv, w_qm, w_f)
    proj = proj.reshape(b, s, FOX_PROJ_COLS)
    bias_f = jnp.pad(fox_b_f[0].astype(f32), (0, LANES - FOX_HEADS)).reshape(1, LANES)
    qe, ke = _fox_gate_operands(f_logit.reshape(b, s, LANES), bias_f)
    y_mix = _fox_attention(proj, qe, ke)
    x2d = mix_tail(x2d, 1, y_mix, proj, FOX_PROJ_COLS // MEM_WIDTH - 1)

    return x2d.reshape(b, s, d)
```

```python
import functools
import math

import jax
import jax.numpy as jnp
import numpy as np
from jax import lax
from jax.experimental import pallas as pl
from jax.experimental.pallas import tpu as pltpu

D_MODEL = 2048
HEAD_DIM = 128
MEM_LEN = 256
MEM_HEADS = 4
MEM_WIDTH = MEM_HEADS * HEAD_DIM
POOL_WIDTH = D_MODEL - MEM_WIDTH
POOL_GROUPS = 4
POOL_GROUP_WIDTH = POOL_WIDTH // POOL_GROUPS
POOL_WINDOWS = (2, 4, 8, 16)
POOL_HALO = 32
FOX_HEADS = POOL_WIDTH // HEAD_DIM
FOX_WIDTH = FOX_HEADS * HEAD_DIM
FOX_PROJ_COLS = 3 * FOX_WIDTH + MEM_WIDTH
FFN_HIDDEN = 4 * D_MODEL
RMS_EPS = 1e-6

LANES = 128
VMEM_LIMIT = 56 * 1024 * 1024

ROW_TILE = 512
FFN_TILE = 1024
FOX_TQ = 1024
FOX_TK = 512
FOX_UNROLLS = (8,)
CUMSUM_TILE = 256

LOG2E = math.log2(math.e)
FOX_QSCALE = LOG2E / math.sqrt(HEAD_DIM)
NEG_BIG = -1e30

_RESIDENT = pl.Buffered(1)


def _params(*sem):
    return pltpu.CompilerParams(dimension_semantics=sem, vmem_limit_bytes=VMEM_LIMIT)


def _rms_scale(x, gain):
    ms = jnp.mean(x * x, axis=-1, keepdims=True)
    return (x * lax.rsqrt(ms + RMS_EPS)) * gain


def _rms_normalize(x, gain):
    return _rms_scale(x, gain).astype(jnp.bfloat16)


def _pool_in_kernel(x_ref, g_ref, w_ref, u_ref, qm_ref):
    hn = _rms_normalize(x_ref[...], g_ref[...])
    res = jnp.dot(hn, w_ref[...], preferred_element_type=jnp.float32)
    u_ref[...] = res[:, :POOL_WIDTH]
    qm_ref[...] = res[:, POOL_WIDTH:].astype(jnp.bfloat16)


def _pool_in_proj(x2d, gain, w_stack):
    rows = x2d.shape[0]
    return pl.pallas_call(
        _pool_in_kernel,
        grid=(rows // ROW_TILE,),
        in_specs=[
            pl.BlockSpec((ROW_TILE, D_MODEL), lambda i: (i, 0)),
            pl.BlockSpec((1, D_MODEL), lambda i: (0, 0)),
            pl.BlockSpec((None, D_MODEL, D_MODEL), lambda i: (0, 0, 0), pipeline_mode=_RESIDENT),
        ],
        out_specs=[
            pl.BlockSpec((ROW_TILE, POOL_WIDTH), lambda i: (i, 0)),
            pl.BlockSpec((ROW_TILE, MEM_WIDTH), lambda i: (i, 0)),
        ],
        out_shape=[
            jax.ShapeDtypeStruct((rows, POOL_WIDTH), jnp.float32),
            jax.ShapeDtypeStruct((rows, MEM_WIDTH), jnp.bfloat16),
        ],
        compiler_params=_params("arbitrary"),
        name="pool_in_proj",
    )(x2d, gain, w_stack)


def _mem_kv_kernel(x_ref, g_ref, w_ref, o_ref):
    hn = _rms_normalize(x_ref[...], g_ref[...])
    o_ref[...] = jnp.dot(hn, w_ref[...], preferred_element_type=jnp.float32).astype(jnp.bfloat16)


def _mem_kv_proj(mem2d, gain, w_stack, layer):
    rows = mem2d.shape[0]
    n = w_stack.shape[2]
    return pl.pallas_call(
        _mem_kv_kernel,
        grid=(1,),
        in_specs=[
            pl.BlockSpec((rows, D_MODEL), lambda i: (0, 0)),
            pl.BlockSpec((1, D_MODEL), lambda i: (0, 0)),
            pl.BlockSpec((None, D_MODEL, n), lambda i: (layer, 0, 0)),
        ],
        out_specs=pl.BlockSpec((rows, n), lambda i: (0, 0)),
        out_shape=jax.ShapeDtypeStruct((rows, n), jnp.bfloat16),
        compiler_params=_params("arbitrary"),
        name="mem_kv_proj",
    )(mem2d, gain, w_stack)


def _fox_in_kernel(x_ref, g_ref, wqkv_ref, wqm_ref, wf_ref, o_ref, f_ref):
    hn = _rms_normalize(x_ref[...], g_ref[...])
    f_ref[...] = jnp.dot(hn, wf_ref[...], preferred_element_type=jnp.float32)
    for part, scale in enumerate((FOX_QSCALE, None, None)):
        cols = slice(part * FOX_WIDTH, (part + 1) * FOX_WIDTH)
        res = jnp.dot(hn, wqkv_ref[:, cols], preferred_element_type=jnp.float32)
        if scale is not None:
            res = res * scale
        o_ref[:, cols] = res.astype(jnp.bfloat16)
    o_ref[:, 3 * FOX_WIDTH:] = jnp.dot(hn, wqm_ref[...], preferred_element_type=jnp.float32).astype(jnp.bfloat16)


def _fox_in_proj(x2d, gain, w_qkv, w_qm, w_f):
    rows = x2d.shape[0]
    return pl.pallas_call(
        _fox_in_kernel,
        grid=(rows // ROW_TILE,),
        in_specs=[
            pl.BlockSpec((ROW_TILE, D_MODEL), lambda i: (i, 0)),
            pl.BlockSpec((1, D_MODEL), lambda i: (0, 0)),
            pl.BlockSpec((D_MODEL, 3 * FOX_WIDTH), lambda i: (0, 0), pipeline_mode=_RESIDENT),
            pl.BlockSpec((D_MODEL, MEM_WIDTH), lambda i: (0, 0), pipeline_mode=_RESIDENT),
            pl.BlockSpec((D_MODEL, LANES), lambda i: (0, 0), pipeline_mode=_RESIDENT),
        ],
        out_specs=[
            pl.BlockSpec((ROW_TILE, FOX_PROJ_COLS), lambda i: (i, 0)),
            pl.BlockSpec((ROW_TILE, LANES), lambda i: (i, 0)),
        ],
        out_shape=[
            jax.ShapeDtypeStruct((rows, FOX_PROJ_COLS), jnp.bfloat16),
            jax.ShapeDtypeStruct((rows, LANES), jnp.float32),
        ],
        compiler_params=_params("arbitrary"),
        name="fox_in_proj",
    )(x2d, gain, w_qkv, w_qm, w_f)


def _pool_mix_kernel(u_ref, w_ref, s_ref, y_ref, ext_ref, lvl_ref):
    i = pl.program_id(1)
    tm = u_ref.shape[0]
    rows = tm + POOL_HALO

    @pl.when(i == 0)
    def _():
        ext_ref[0:POOL_HALO, :] = jnp.zeros((POOL_HALO, POOL_WIDTH), jnp.float32)

    @pl.when(i > 0)
    def _():
        ext_ref[0:POOL_HALO, :] = ext_ref[tm:tm + POOL_HALO, :]

    ext_ref[POOL_HALO:, :] = u_ref[...]

    t = i * tm + lax.broadcasted_iota(jnp.int32, (tm, 1), 0)
    for g, win in enumerate(POOL_WINDOWS):
        c0 = g * POOL_GROUP_WIDTH
        c1 = c0 + POOL_GROUP_WIDTH
        tok = ext_ref[POOL_HALO:, c0:c1]
        levels = win.bit_length() - 1
        wsum = None
        for lvl in range(levels):
            shift = 1 << lvl
            start = POOL_HALO - 8 * (levels - 1 - lvl)
            if lvl == 0:
                cur = ext_ref[start:, c0:c1] + ext_ref[start - shift:rows - shift, c0:c1]
            else:
                src = lvl_ref.at[(lvl - 1) % 2]
                cur = src[start:, :] + src[start - shift:rows - shift, :]
            if lvl + 1 < levels:
                lvl_ref[lvl % 2, start:, :] = cur
            else:
                wsum = cur
        count = jnp.minimum(t + 1, win).astype(jnp.float32)
        pooled = wsum / count - tok
        mixed = jnp.dot(pooled.astype(jnp.bfloat16), w_ref[g], preferred_element_type=jnp.float32)
        y_ref[:, c0:c1] = (mixed * s_ref[:, c0:c1]).astype(jnp.bfloat16)


def _pool_mix(u, w_grp_stack, scale):
    b, s, _ = u.shape
    assert all(w & (w - 1) == 0 for w in POOL_WINDOWS)
    assert POOL_HALO == 8 * (max(POOL_WINDOWS).bit_length() - 1)
    return pl.pallas_call(
        _pool_mix_kernel,
        grid=(b, s // ROW_TILE),
        in_specs=[
            pl.BlockSpec((None, ROW_TILE, POOL_WIDTH), lambda bi, i: (bi, i, 0)),
            pl.BlockSpec((None, POOL_GROUPS, POOL_GROUP_WIDTH, POOL_GROUP_WIDTH), lambda bi, i: (0, 0, 0, 0)),
            pl.BlockSpec((1, POOL_WIDTH), lambda bi, i: (0, 0)),
        ],
        out_specs=pl.BlockSpec((None, ROW_TILE, POOL_WIDTH), lambda bi, i: (bi, i, 0)),
        out_shape=jax.ShapeDtypeStruct((b, s, POOL_WIDTH), jnp.bfloat16),
        scratch_shapes=[pltpu.VMEM((ROW_TILE + POOL_HALO, POOL_WIDTH), jnp.float32),
                        pltpu.VMEM((2, ROW_TILE + POOL_HALO, POOL_GROUP_WIDTH), jnp.float32)],
        compiler_params=_params("arbitrary", "arbitrary"),
        name="pool_mix",
    )(u, w_grp_stack, scale)


def _piece_selector():
    sel = np.zeros((3 * LANES, FOX_HEADS * LANES), np.float32)
    for h in range(FOX_HEADS):
        for piece in range(3):
            sel[piece * LANES + h, h * LANES + piece] = 1.0
            sel[piece * LANES + h, h * LANES + 3 + piece] = -1.0
    return jnp.asarray(sel, jnp.bfloat16)


def _bf16_pieces(x):
    hi = x.astype(jnp.bfloat16)
    r1 = x - hi.astype(jnp.float32)
    mid = r1.astype(jnp.bfloat16)
    r2 = r1 - mid.astype(jnp.float32)
    lo = r2.astype(jnp.bfloat16)
    return hi, mid, lo


def _fox_gate_kernel(f_ref, b_ref, sel_ref, qe_ref, ke_ref, carry_ref):
    i = pl.program_id(1)
    tm = f_ref.shape[0]

    @pl.when(i == 0)
    def _():
        carry_ref[...] = jnp.zeros_like(carry_ref)

    z = f_ref[...] + b_ref[...]
    log_f = jnp.minimum(z, 0.0) - jnp.log1p(jnp.exp(-jnp.abs(z)))
    row = lax.broadcasted_iota(jnp.int32, (tm, tm), 0)
    col = lax.broadcasted_iota(jnp.int32, (tm, tm), 1)
    tri = (col <= row).astype(jnp.float32)
    csum = jnp.dot(tri, log_f, preferred_element_type=jnp.float32,
                   precision=lax.Precision.HIGHEST) + carry_ref[...]
    carry_ref[...] = csum[tm - 1:tm, :]

    pieces = jnp.concatenate(_bf16_pieces(csum * LOG2E), axis=1)
    placed = jnp.dot(pieces, sel_ref[...], preferred_element_type=jnp.float32)
    lane = lax.broadcasted_iota(jnp.int32, (tm, LANES), 1)
    k_side = (lane >= 3) & (lane < 6)
    q_ones = jnp.where(k_side, 1.0, 0.0)
    k_ones = jnp.where(lane < 3, 1.0, 0.0)
    for h in range(FOX_HEADS):
        blk = placed[:, h * LANES:(h + 1) * LANES]
        qe_ref[h] = jnp.where(lane < 3, blk, q_ones).astype(jnp.bfloat16)
        ke_ref[h] = jnp.where(k_side, blk, k_ones).astype(jnp.bfloat16)


def _fox_gate_operands(f_logit, bias):
    b, s, _ = f_logit.shape
    out = jax.ShapeDtypeStruct((b, FOX_HEADS, s, LANES), jnp.bfloat16)
    spec = pl.BlockSpec((None, FOX_HEADS, CUMSUM_TILE, LANES), lambda bi, i: (bi, 0, i, 0))
    return pl.pallas_call(
        _fox_gate_kernel,
        grid=(b, s // CUMSUM_TILE),
        in_specs=[
            pl.BlockSpec((None, CUMSUM_TILE, LANES), lambda bi, i: (bi, i, 0)),
            pl.BlockSpec((1, LANES), lambda bi, i: (0, 0)),
            pl.BlockSpec((3 * LANES, FOX_HEADS * LANES), lambda bi, i: (0, 0)),
        ],
        out_specs=[spec, spec],
        out_shape=[out, out],
        scratch_shapes=[pltpu.VMEM((1, LANES), jnp.float32)],
        compiler_params=_params("arbitrary", "arbitrary"),
        name="fox_gate_operands",
    )(f_logit, bias, _piece_selector())


def _fox_attn_kernel(q_ref, qe_ref, k_ref, ke_ref, v_ref, o_ref):
    i = pl.program_id(2)
    tq, tk = FOX_TQ, FOX_TK
    per_tile = tq // tk
    q2 = jnp.concatenate([q_ref[...], qe_ref[...]], axis=1)
    lane = lax.broadcasted_iota(jnp.int32, (tk, LANES), 1)
    v_ext = jnp.where(lane == 0, 1.0, 0.0).astype(jnp.bfloat16)

    def block(j, carry, row0=0, masked=False):
        m, acc = carry
        ks = pl.ds(pl.multiple_of(j * tk, tk), tk)
        k2 = jnp.concatenate([k_ref[ks, :], ke_ref[ks, :]], axis=1)
        v2 = jnp.concatenate([v_ref[ks, :], v_ext], axis=1)
        s = lax.dot_general(q2[row0:], k2, (((1,), (1,)), ((), ())), preferred_element_type=jnp.float32)
        if masked:
            row = lax.broadcasted_iota(jnp.int32, s.shape, 0)
            col = lax.broadcasted_iota(jnp.int32, s.shape, 1)
            s = jnp.where(col <= row, s, -jnp.inf)
        m_old = m[row0:]
        m_new = jnp.maximum(m_old, jnp.max(s, axis=-1, keepdims=True))
        alpha = jnp.exp2(m_old - m_new)
        p = jnp.exp2(s - m_new).astype(jnp.bfloat16)
        acc_new = alpha * acc[row0:] + jnp.dot(p, v2, preferred_element_type=jnp.float32)
        if row0:
            m_new = jnp.concatenate([m[:row0], m_new], axis=0)
            acc_new = jnp.concatenate([acc[:row0], acc_new], axis=0)
        return m_new, acc_new

    carry = (jnp.full((tq, 1), NEG_BIG, jnp.float32),
             jnp.zeros((tq, 2 * HEAD_DIM), jnp.float32))
    n_full = i * per_tile

    done = 0
    for n in FOX_UNROLLS:
        def body(t, c, n=n, base=done):
            for u in range(n):
                c = block(base + t * n + u, c)
            return c
        trips = (n_full - done) // n
        carry = lax.fori_loop(0, trips, body, carry)
        done = done + trips * n

    def tail(c, n_rest):
        for u in range(n_rest):
            c = block(done + u, c)
        for d in range(per_tile):
            c = block(i * per_tile + d, c, row0=d * tk, masked=True)
        _, acc = c
        o_ref[...] = (acc[:, :HEAD_DIM] / acc[:, HEAD_DIM:HEAD_DIM + 1]).astype(jnp.bfloat16)

    rest = n_full - done
    for n_rest in range(0, FOX_UNROLLS[-1], per_tile):
        pl.when(rest == n_rest)(functools.partial(tail, carry, n_rest))


def _fox_attention(proj, qe, ke):
    b, s, _ = proj.shape
    assert FOX_TQ % FOX_TK == 0 and all(n % (FOX_TQ // FOX_TK) == 0 for n in FOX_UNROLLS)
    return pl.pallas_call(
        _fox_attn_kernel,
        grid=(b, FOX_HEADS, s // FOX_TQ),
        in_specs=[
            pl.BlockSpec((None, FOX_TQ, HEAD_DIM), lambda bi, h, i: (bi, i, h)),
            pl.BlockSpec((None, None, FOX_TQ, LANES), lambda bi, h, i: (bi, h, i, 0)),
            pl.BlockSpec((None, s, HEAD_DIM), lambda bi, h, i: (bi, 0, FOX_HEADS + h)),
            pl.BlockSpec((None, None, s, LANES), lambda bi, h, i: (bi, h, 0, 0)),
            pl.BlockSpec((None, s, HEAD_DIM), lambda bi, h, i: (bi, 0, 2 * FOX_HEADS + h)),
        ],
        out_specs=pl.BlockSpec((None, FOX_TQ, HEAD_DIM), lambda bi, h, i: (bi, i, h)),
        out_shape=jax.ShapeDtypeStruct((b, s, FOX_WIDTH), jnp.bfloat16),
        compiler_params=_params("arbitrary", "arbitrary", "arbitrary"),
        name="fox_attention",
    )(proj, qe, proj, ke, proj)


def _mem_attention(q_ref, k_ref, v_ref):
    inv_sqrt = 1.0 / math.sqrt(HEAD_DIM)
    heads = []
    for h in range(MEM_HEADS):
        c0 = h * HEAD_DIM
        c1 = c0 + HEAD_DIM
        logits = lax.dot_general(q_ref[:, c0:c1], k_ref[:, c0:c1], (((1,), (1,)), ((), ())),
                                 preferred_element_type=jnp.float32)
        logits = logits * inv_sqrt
        m = jnp.max(logits, axis=-1, keepdims=True)
        e = jnp.exp(logits - m)
        p = e / jnp.sum(e, axis=-1, keepdims=True)
        out = jnp.dot(p.astype(jnp.bfloat16), v_ref[:, c0:c1], preferred_element_type=jnp.float32)
        heads.append(out.astype(jnp.bfloat16))
    return jnp.concatenate(heads, axis=1)


def _out_proj_kernel(x_ref, ya_ref, q_ref, k_ref, v_ref, wa_ref, wb_ref, g_ref, o_ref, hn_ref):
    y_mem = _mem_attention(q_ref, k_ref, v_ref)
    acc = jnp.dot(ya_ref[...], wa_ref[...], preferred_element_type=jnp.float32)
    acc = acc + jnp.dot(y_mem, wb_ref[...], preferred_element_type=jnp.float32)
    x_new = x_ref[...] + acc
    o_ref[...] = x_new
    hn_ref[...] = _rms_normalize(x_new, g_ref[...])


def _out_proj(x2d, y_mix, q_arr, q_col_block, kv, w_stack, layer, ffn_gain):
    rows = x2d.shape[0]
    tiles_per_batch = rows // kv.shape[0] // ROW_TILE
    mem_row_block = POOL_WIDTH // MEM_WIDTH
    return pl.pallas_call(
        _out_proj_kernel,
        grid=(rows // ROW_TILE,),
        in_specs=[
            pl.BlockSpec((ROW_TILE, D_MODEL), lambda i: (i, 0)),
            pl.BlockSpec((ROW_TILE, POOL_WIDTH), lambda i: (i, 0)),
            pl.BlockSpec((ROW_TILE, MEM_WIDTH), lambda i: (i, q_col_block)),
            pl.BlockSpec((None, MEM_LEN, MEM_WIDTH), lambda i: (i // tiles_per_batch, 0, 0)),
            pl.BlockSpec((None, MEM_LEN, MEM_WIDTH), lambda i: (i // tiles_per_batch, 0, 1)),
            pl.BlockSpec((None, POOL_WIDTH, D_MODEL), lambda i: (layer, 0, 0), pipeline_mode=_RESIDENT),
            pl.BlockSpec((None, MEM_WIDTH, D_MODEL), lambda i: (layer, mem_row_block, 0),
                         pipeline_mode=_RESIDENT),
            pl.BlockSpec((1, D_MODEL), lambda i: (0, 0)),
        ],
        out_specs=[pl.BlockSpec((ROW_TILE, D_MODEL), lambda i: (i, 0)),
                   pl.BlockSpec((ROW_TILE, D_MODEL), lambda i: (i, 0))],
        out_shape=[jax.ShapeDtypeStruct((rows, D_MODEL), jnp.float32),
                   jax.ShapeDtypeStruct((rows, D_MODEL), jnp.bfloat16)],
        compiler_params=_params("arbitrary"),
        name="out_proj",
    )(x2d, y_mix, q_arr, kv, kv, w_stack, w_stack, ffn_gain)


def _ffn_kernel(x_ref, hn_ref, w1_ref, w2_ref, g_ref, o_ref, *, final_norm):
    @pl.when(pl.program_id(1) == 0)
    def _():
        o_ref[...] = x_ref[...]

    a = jnp.dot(hn_ref[...], w1_ref[...], preferred_element_type=jnp.float32)
    a = jnp.square(jnp.maximum(a, 0.0)).astype(jnp.bfloat16)
    o_ref[...] += jnp.dot(a, w2_ref[...], preferred_element_type=jnp.float32)

    if final_norm:
        @pl.when(pl.program_id(1) == pl.num_programs(1) - 1)
        def _():
            o_ref[...] = _rms_scale(o_ref[...], g_ref[...])


def _ffn(x2d, hn2d, w1_stack, w2_stack, layer, final_gain, final_norm):
    rows = x2d.shape[0]
    return pl.pallas_call(
        functools.partial(_ffn_kernel, final_norm=final_norm),
        grid=(rows // ROW_TILE, FFN_HIDDEN // FFN_TILE),
        in_specs=[
            pl.BlockSpec((ROW_TILE, D_MODEL), lambda i, j: (i, 0)),
            pl.BlockSpec((ROW_TILE, D_MODEL), lambda i, j: (i, 0)),
            pl.BlockSpec((None, D_MODEL, FFN_TILE), lambda i, j: (layer, 0, j)),
            pl.BlockSpec((None, FFN_TILE, D_MODEL), lambda i, j: (layer, j, 0)),
            pl.BlockSpec((1, D_MODEL), lambda i, j: (0, 0)),
        ],
        out_specs=pl.BlockSpec((ROW_TILE, D_MODEL), lambda i, j: (i, 0)),
        out_shape=jax.ShapeDtypeStruct((rows, D_MODEL), jnp.float32),
        compiler_params=_params("arbitrary", "arbitrary"),
        name="ffn_final" if final_norm else "ffn",
    )(x2d, hn2d, w1_stack, w2_stack, final_gain)


def kernel(x, mem, norm_mix, norm_mem, pool_w_in, pool_w_grp, pool_scale, fox_w_in, fox_b_f,
           w_mem_kv, w_out, norm_ffn, w_ffn1, w_ffn2, norm_final):
    b, s, d = x.shape
    rows = b * s
    depth = w_out.shape[0]
    bf16 = jnp.bfloat16
    f32 = jnp.float32
    x2d = x.reshape(rows, d)
    mem2d = mem.reshape(b * MEM_LEN, d)
    w_mem_kv_b, w_out_b = w_mem_kv.astype(bf16), w_out.astype(bf16)
    w_ffn1_b, w_ffn2_b = w_ffn1.astype(bf16), w_ffn2.astype(bf16)

    def row(v):
        return v.reshape(1, -1).astype(f32)

    def mix_tail(x2d, layer, y_mix, q_arr, q_col_block):
        kv = _mem_kv_proj(mem2d, row(norm_mem[layer]), w_mem_kv_b, layer).reshape(b, MEM_LEN, 2 * MEM_WIDTH)
        x2d, hn2d = _out_proj(x2d, y_mix.reshape(rows, POOL_WIDTH), q_arr.reshape(rows, -1), q_col_block,
                              kv, w_out_b, layer, row(norm_ffn[layer]))
        return _ffn(x2d, hn2d, w_ffn1_b, w_ffn2_b, layer, row(norm_final), layer == depth - 1)

    u, q_mem = _pool_in_proj(x2d, row(norm_mix[0]), pool_w_in.astype(bf16))
    y_mix = _pool_mix(u.reshape(b, s, POOL_WIDTH), pool_w_grp.astype(bf16), row(pool_scale[0]))
    x2d = mix_tail(x2d, 0, y_mix, q_mem.reshape(b, s, MEM_WIDTH), 0)

    w_in = fox_w_in[0]
    n_qkv = 3 * FOX_WIDTH
    w_qkv = w_in[:, :n_qkv].astype(bf16)
    w_qm = w_in[:, n_qkv + FOX_HEADS:].astype(bf16)
    w_f = jnp.pad(w_in[:, n_qkv:n_qkv + FOX_HEADS], ((0, 0), (0, LANES - FOX_HEADS))).astype(bf16)
    proj, f_logit = _fox_in_proj(x2d, row(norm_mix[1]), w_qkv, w_qm, w_f)
    proj = proj.reshape(b, s, FOX_PROJ_COLS)
    bias_f = jnp.pad(fox_b_f[0].astype(f32), (0, LANES - FOX_HEADS)).reshape(1, LANES)
    qe, ke = _fox_gate_operands(f_logit.reshape(b, s, LANES), bias_f)
    y_mix = _fox_attention(proj, qe, ke)
    x2d = mix_tail(x2d, 1, y_mix, proj, FOX_PROJ_COLS // MEM_WIDTH - 1)

    return x2d.reshape(b, s, d)
```

```python
import functools
import math

import jax
import jax.numpy as jnp
import numpy as np
from jax import lax
from jax.experimental import pallas as pl
from jax.experimental.pallas import tpu as pltpu

D_MODEL = 2048
HEAD_DIM = 128
MEM_LEN = 256
MEM_HEADS = 4
MEM_WIDTH = MEM_HEADS * HEAD_DIM
POOL_WIDTH = D_MODEL - MEM_WIDTH
POOL_GROUPS = 4
POOL_GROUP_WIDTH = POOL_WIDTH // POOL_GROUPS
POOL_WINDOWS = (2, 4, 8, 16)
POOL_HALO = 32
FOX_HEADS = POOL_WIDTH // HEAD_DIM
FOX_WIDTH = FOX_HEADS * HEAD_DIM
FOX_PROJ_COLS = 3 * FOX_WIDTH + MEM_WIDTH
FFN_HIDDEN = 4 * D_MODEL
RMS_EPS = 1e-6

LANES = 128
VMEM_LIMIT = 56 * 1024 * 1024

ROW_TILE = 512
FFN_TILE = 1024
FOX_TQ = 1024
FOX_TK = 512
FOX_UNROLLS = (8,)
CUMSUM_TILE = 256

LOG2E = math.log2(math.e)
FOX_QSCALE = LOG2E / math.sqrt(HEAD_DIM)
NEG_BIG = -1e30

_RESIDENT = pl.Buffered(1)


def _params(*sem):
    return pltpu.CompilerParams(dimension_semantics=sem, vmem_limit_bytes=VMEM_LIMIT)


def _rms_scale(x, gain):
    ms = jnp.mean(x * x, axis=-1, keepdims=True)
    return (x * lax.rsqrt(ms + RMS_EPS)) * gain


def _rms_normalize(x, gain):
    return _rms_scale(x, gain).astype(jnp.bfloat16)


def _pool_in_kernel(x_ref, g_ref, w_ref, u_ref, qm_ref):
    hn = _rms_normalize(x_ref[...], g_ref[...])
    res = jnp.dot(hn, w_ref[...], preferred_element_type=jnp.float32)
    u_ref[...] = res[:, :POOL_WIDTH]
    qm_ref[...] = res[:, POOL_WIDTH:].astype(jnp.bfloat16)


def _pool_in_proj(x2d, gain, w_stack):
    rows = x2d.shape[0]
    return pl.pallas_call(
        _pool_in_kernel,
        grid=(rows // ROW_TILE,),
        in_specs=[
            pl.BlockSpec((ROW_TILE, D_MODEL), lambda i: (i, 0)),
            pl.BlockSpec((1, D_MODEL), lambda i: (0, 0)),
            pl.BlockSpec((None, D_MODEL, D_MODEL), lambda i: (0, 0, 0), pipeline_mode=_RESIDENT),
        ],
        out_specs=[
            pl.BlockSpec((ROW_TILE, POOL_WIDTH), lambda i: (i, 0)),
            pl.BlockSpec((ROW_TILE, MEM_WIDTH), lambda i: (i, 0)),
        ],
        out_shape=[
            jax.ShapeDtypeStruct((rows, POOL_WIDTH), jnp.float32),
            jax.ShapeDtypeStruct((rows, MEM_WIDTH), jnp.bfloat16),
        ],
        compiler_params=_params("arbitrary"),
        name="pool_in_proj",
    )(x2d, gain, w_stack)


def _mem_kv_kernel(x_ref, g_ref, w_ref, o_ref):
    hn = _rms_normalize(x_ref[...], g_ref[...])
    o_ref[...] = jnp.dot(hn, w_ref[...], preferred_element_type=jnp.float32).astype(jnp.bfloat16)


def _mem_kv_proj(mem2d, gain, w_stack, layer):
    rows = mem2d.shape[0]
    n = w_stack.shape[2]
    return pl.pallas_call(
        _mem_kv_kernel,
        grid=(1,),
        in_specs=[
            pl.BlockSpec((rows, D_MODEL), lambda i: (0, 0)),
            pl.BlockSpec((1, D_MODEL), lambda i: (0, 0)),
            pl.BlockSpec((None, D_MODEL, n), lambda i: (layer, 0, 0)),
        ],
        out_specs=pl.BlockSpec((rows, n), lambda i: (0, 0)),
        out_shape=jax.ShapeDtypeStruct((rows, n), jnp.bfloat16),
        compiler_params=_params("arbitrary"),
        name="mem_kv_proj",
    )(mem2d, gain, w_stack)


PIECE_LANES = 16


def _piece_selector():
    sel = np.zeros((LANES, FOX_HEADS * LANES), np.float32)
    for h in range(FOX_HEADS):
        for piece in range(3):
            sel[piece * PIECE_LANES + h, h * LANES + piece] = 1.0
            sel[piece * PIECE_LANES + h, h * LANES + 3 + piece] = -1.0
    return jnp.asarray(sel, jnp.bfloat16)


def _bf16_pieces(x):
    hi = x.astype(jnp.bfloat16).astype(jnp.float32)
    r1 = x - hi
    mid = r1.astype(jnp.bfloat16).astype(jnp.float32)
    lo = (r1 - mid).astype(jnp.bfloat16).astype(jnp.float32)
    return hi, mid, lo


def _fox_in_kernel(x_ref, g_ref, wqkv_ref, wqm_ref, wf_ref, b_ref, sel_ref, o_ref, qe_ref, ke_ref, carry_ref,
                   *, tiles_per_batch):
    @pl.when(pl.program_id(0) % tiles_per_batch == 0)
    def _():
        carry_ref[...] = jnp.zeros_like(carry_ref)

    hn = _rms_normalize(x_ref[...], g_ref[...])

    z = jnp.dot(hn, wf_ref[...], preferred_element_type=jnp.float32) + b_ref[...]
    log_f = jnp.minimum(z, 0.0) - jnp.log1p(jnp.exp(-jnp.abs(z)))
    lf_pieces = jnp.concatenate(_bf16_pieces(log_f), axis=1).astype(jnp.bfloat16)
    tc = CUMSUM_TILE
    row = lax.broadcasted_iota(jnp.int32, (tc, tc), 0)
    col = lax.broadcasted_iota(jnp.int32, (tc, tc), 1)
    tri = (col <= row).astype(jnp.bfloat16)
    lane = lax.broadcasted_iota(jnp.int32, (tc, LANES), 1)
    k_side = (lane >= 3) & (lane < 6)
    q_ones = jnp.where(k_side, 1.0, 0.0)
    k_ones = jnp.where(lane < 3, 1.0, 0.0)
    mid_lanes = (lane >= PIECE_LANES) & (lane < PIECE_LANES + FOX_HEADS)
    lo_lanes = (lane >= 2 * PIECE_LANES) & (lane < 2 * PIECE_LANES + FOX_HEADS)
    total = carry_ref[...]
    for r0 in range(0, ROW_TILE, tc):
        part = jnp.dot(tri, lf_pieces[r0:r0 + tc], preferred_element_type=jnp.float32)
        csum = (part[:, :LANES] + part[:, LANES:2 * LANES]) + part[:, 2 * LANES:] + total
        total = csum[tc - 1:tc, :]
        hi, mid, lo = _bf16_pieces(csum * LOG2E)
        packed = jnp.where(lane < FOX_HEADS, hi,
                           jnp.where(mid_lanes, pltpu.roll(mid, PIECE_LANES, 1),
                                     jnp.where(lo_lanes, pltpu.roll(lo, 2 * PIECE_LANES, 1), 0.0)))
        placed = jnp.dot(packed.astype(jnp.bfloat16), sel_ref[...],
                         preferred_element_type=jnp.float32)
        for h in range(FOX_HEADS):
            blk = placed[:, h * LANES:(h + 1) * LANES]
            qe_ref[h, r0:r0 + tc, :] = jnp.where(lane < 3, blk, q_ones).astype(jnp.bfloat16)
            ke_ref[h, r0:r0 + tc, :] = jnp.where(k_side, blk, k_ones).astype(jnp.bfloat16)
    carry_ref[...] = total

    for part, scale in enumerate((FOX_QSCALE, None, None)):
        cols = slice(part * FOX_WIDTH, (part + 1) * FOX_WIDTH)
        res = jnp.dot(hn, wqkv_ref[:, cols], preferred_element_type=jnp.float32)
        if scale is not None:
            res = res * scale
        o_ref[:, cols] = res.astype(jnp.bfloat16)
    o_ref[:, 3 * FOX_WIDTH:] = jnp.dot(hn, wqm_ref[...], preferred_element_type=jnp.float32).astype(jnp.bfloat16)


def _fox_in_proj(x2d, gain, w_qkv, w_qm, w_f, bias, batch):
    rows = x2d.shape[0]
    seq = rows // batch
    tiles_per_batch = seq // ROW_TILE
    ext = jax.ShapeDtypeStruct((batch, FOX_HEADS, seq, LANES), jnp.bfloat16)
    ext_spec = pl.BlockSpec((None, FOX_HEADS, ROW_TILE, LANES),
                            lambda i: (i // tiles_per_batch, 0, i % tiles_per_batch, 0))
    return pl.pallas_call(
        functools.partial(_fox_in_kernel, tiles_per_batch=tiles_per_batch),
        grid=(rows // ROW_TILE,),
        in_specs=[
            pl.BlockSpec((ROW_TILE, D_MODEL), lambda i: (i, 0)),
            pl.BlockSpec((1, D_MODEL), lambda i: (0, 0)),
            pl.BlockSpec((D_MODEL, 3 * FOX_WIDTH), lambda i: (0, 0), pipeline_mode=_RESIDENT),
            pl.BlockSpec((D_MODEL, MEM_WIDTH), lambda i: (0, 0), pipeline_mode=_RESIDENT),
            pl.BlockSpec((D_MODEL, LANES), lambda i: (0, 0), pipeline_mode=_RESIDENT),
            pl.BlockSpec((1, LANES), lambda i: (0, 0)),
            pl.BlockSpec((LANES, FOX_HEADS * LANES), lambda i: (0, 0), pipeline_mode=_RESIDENT),
        ],
        out_specs=[pl.BlockSpec((ROW_TILE, FOX_PROJ_COLS), lambda i: (i, 0)), ext_spec, ext_spec],
        out_shape=[jax.ShapeDtypeStruct((rows, FOX_PROJ_COLS), jnp.bfloat16), ext, ext],
        scratch_shapes=[pltpu.VMEM((1, LANES), jnp.float32)],
        compiler_params=_params("arbitrary"),
        name="fox_in_proj",
    )(x2d, gain, w_qkv, w_qm, w_f, bias, _piece_selector())


def _pool_mix_kernel(u_ref, w_ref, s_ref, y_ref, ext_ref, lvl_ref):
    i = pl.program_id(1)
    tm = u_ref.shape[0]
    rows = tm + POOL_HALO

    @pl.when(i == 0)
    def _():
        ext_ref[0:POOL_HALO, :] = jnp.zeros((POOL_HALO, POOL_WIDTH), jnp.float32)

    @pl.when(i > 0)
    def _():
        ext_ref[0:POOL_HALO, :] = ext_ref[tm:tm + POOL_HALO, :]

    ext_ref[POOL_HALO:, :] = u_ref[...]

    t = i * tm + lax.broadcasted_iota(jnp.int32, (tm, 1), 0)
    for g, win in enumerate(POOL_WINDOWS):
        c0 = g * POOL_GROUP_WIDTH
        c1 = c0 + POOL_GROUP_WIDTH
        tok = ext_ref[POOL_HALO:, c0:c1]
        levels = win.bit_length() - 1
        wsum = None
        for lvl in range(levels):
            shift = 1 << lvl
            start = POOL_HALO - 8 * (levels - 1 - lvl)
            if lvl == 0:
                cur = ext_ref[start:, c0:c1] + ext_ref[start - shift:rows - shift, c0:c1]
            else:
                src = lvl_ref.at[(lvl - 1) % 2]
                cur = src[start:, :] + src[start - shift:rows - shift, :]
            if lvl + 1 < levels:
                lvl_ref[lvl % 2, start:, :] = cur
            else:
                wsum = cur
        count = jnp.minimum(t + 1, win).astype(jnp.float32)
        pooled = wsum / count - tok
        mixed = jnp.dot(pooled.astype(jnp.bfloat16), w_ref[g], preferred_element_type=jnp.float32)
        y_ref[:, c0:c1] = (mixed * s_ref[:, c0:c1]).astype(jnp.bfloat16)


def _pool_mix(u, w_grp_stack, scale):
    b, s, _ = u.shape
    assert all(w & (w - 1) == 0 for w in POOL_WINDOWS)
    assert POOL_HALO == 8 * (max(POOL_WINDOWS).bit_length() - 1)
    return pl.pallas_call(
        _pool_mix_kernel,
        grid=(b, s // ROW_TILE),
        in_specs=[
            pl.BlockSpec((None, ROW_TILE, POOL_WIDTH), lambda bi, i: (bi, i, 0)),
            pl.BlockSpec((None, POOL_GROUPS, POOL_GROUP_WIDTH, POOL_GROUP_WIDTH), lambda bi, i: (0, 0, 0, 0)),
            pl.BlockSpec((1, POOL_WIDTH), lambda bi, i: (0, 0)),
        ],
        out_specs=pl.BlockSpec((None, ROW_TILE, POOL_WIDTH), lambda bi, i: (bi, i, 0)),
        out_shape=jax.ShapeDtypeStruct((b, s, POOL_WIDTH), jnp.bfloat16),
        scratch_shapes=[pltpu.VMEM((ROW_TILE + POOL_HALO, POOL_WIDTH), jnp.float32),
                        pltpu.VMEM((2, ROW_TILE + POOL_HALO, POOL_GROUP_WIDTH), jnp.float32)],
        compiler_params=_params("arbitrary", "arbitrary"),
        name="pool_mix",
    )(u, w_grp_stack, scale)


def _fox_attn_kernel(q_ref, qe_ref, k_ref, ke_ref, v_ref, o_ref):
    i = pl.program_id(2)
    tq, tk = FOX_TQ, FOX_TK
    per_tile = tq // tk
    q2 = jnp.concatenate([q_ref[...], qe_ref[...]], axis=1)
    lane = lax.broadcasted_iota(jnp.int32, (tk, LANES), 1)
    v_ext = jnp.where(lane == 0, 1.0, 0.0).astype(jnp.bfloat16)

    def block(j, carry, row0=0, masked=False):
        m, acc = carry
        ks = pl.ds(pl.multiple_of(j * tk, tk), tk)
        k2 = jnp.concatenate([k_ref[ks, :], ke_ref[ks, :]], axis=1)
        v2 = jnp.concatenate([v_ref[ks, :], v_ext], axis=1)
        s = lax.dot_general(q2[row0:], k2, (((1,), (1,)), ((), ())), preferred_element_type=jnp.float32)
        if masked:
            row = lax.broadcasted_iota(jnp.int32, s.shape, 0)
            col = lax.broadcasted_iota(jnp.int32, s.shape, 1)
            s = jnp.where(col <= row, s, -jnp.inf)
        m_old = m[row0:]
        m_new = jnp.maximum(m_old, jnp.max(s, axis=-1, keepdims=True))
        alpha = jnp.exp2(m_old - m_new)
        p = jnp.exp2(s - m_new).astype(jnp.bfloat16)
        acc_new = alpha * acc[row0:] + jnp.dot(p, v2, preferred_element_type=jnp.float32)
        if row0:
            m_new = jnp.concatenate([m[:row0], m_new], axis=0)
            acc_new = jnp.concatenate([acc[:row0], acc_new], axis=0)
        return m_new, acc_new

    carry = (jnp.full((tq, 1), NEG_BIG, jnp.float32),
             jnp.zeros((tq, 2 * HEAD_DIM), jnp.float32))
    n_full = i * per_tile

    done = 0
    for n in FOX_UNROLLS:
        def body(t, c, n=n, base=done):
            for u in range(n):
                c = block(base + t * n + u, c)
            return c
        trips = (n_full - done) // n
        carry = lax.fori_loop(0, trips, body, carry)
        done = done + trips * n

    def tail(c, n_rest):
        for u in range(n_rest):
            c = block(done + u, c)
        for d in range(per_tile):
            c = block(i * per_tile + d, c, row0=d * tk, masked=True)
        _, acc = c
        o_ref[...] = (acc[:, :HEAD_DIM] / acc[:, HEAD_DIM:HEAD_DIM + 1]).astype(jnp.bfloat16)

    rest = n_full - done
    for n_rest in range(0, FOX_UNROLLS[-1], per_tile):
        pl.when(rest == n_rest)(functools.partial(tail, carry, n_rest))


def _fox_attention(proj, qe, ke):
    b, s, _ = proj.shape
    assert FOX_TQ % FOX_TK == 0 and all(n % (FOX_TQ // FOX_TK) == 0 for n in FOX_UNROLLS)
    return pl.pallas_call(
        _fox_attn_kernel,
        grid=(b, FOX_HEADS, s // FOX_TQ),
        in_specs=[
            pl.BlockSpec((None, FOX_TQ, HEAD_DIM), lambda bi, h, i: (bi, i, h)),
            pl.BlockSpec((None, None, FOX_TQ, LANES), lambda bi, h, i: (bi, h, i, 0)),
            pl.BlockSpec((None, s, HEAD_DIM), lambda bi, h, i: (bi, 0, FOX_HEADS + h)),
            pl.BlockSpec((None, None, s, LANES), lambda bi, h, i: (bi, h, 0, 0)),
            pl.BlockSpec((None, s, HEAD_DIM), lambda bi, h, i: (bi, 0, 2 * FOX_HEADS + h)),
        ],
        out_specs=pl.BlockSpec((None, FOX_TQ, HEAD_DIM), lambda bi, h, i: (bi, i, h)),
        out_shape=jax.ShapeDtypeStruct((b, s, FOX_WIDTH), jnp.bfloat16),
        compiler_params=_params("arbitrary", "arbitrary", "arbitrary"),
        name="fox_attention",
    )(proj, qe, proj, ke, proj)


def _mem_attention(q_ref, k_ref, v_ref):
    inv_sqrt = 1.0 / math.sqrt(HEAD_DIM)
    heads = []
    for h in range(MEM_HEADS):
        c0 = h * HEAD_DIM
        c1 = c0 + HEAD_DIM
        logits = lax.dot_general(q_ref[:, c0:c1], k_ref[:, c0:c1], (((1,), (1,)), ((), ())),
                                 preferred_element_type=jnp.float32)
        logits = logits * inv_sqrt
        m = jnp.max(logits, axis=-1, keepdims=True)
        e = jnp.exp(logits - m)
        p = e / jnp.sum(e, axis=-1, keepdims=True)
        out = jnp.dot(p.astype(jnp.bfloat16), v_ref[:, c0:c1], preferred_element_type=jnp.float32)
        heads.append(out.astype(jnp.bfloat16))
    return jnp.concatenate(heads, axis=1)


def _out_proj_kernel(x_ref, ya_ref, q_ref, k_ref, v_ref, wa_ref, wb_ref, g_ref, o_ref, hn_ref):
    y_mem = _mem_attention(q_ref, k_ref, v_ref)
    acc = jnp.dot(ya_ref[...], wa_ref[...], preferred_element_type=jnp.float32)
    acc = acc + jnp.dot(y_mem, wb_ref[...], preferred_element_type=jnp.float32)
    x_new = x_ref[...] + acc
    o_ref[...] = x_new
    hn_ref[...] = _rms_normalize(x_new, g_ref[...])


def _out_proj(x2d, y_mix, q_arr, q_col_block, kv, w_stack, layer, ffn_gain):
    rows = x2d.shape[0]
    tiles_per_batch = rows // kv.shape[0] // ROW_TILE
    mem_row_block = POOL_WIDTH // MEM_WIDTH
    return pl.pallas_call(
        _out_proj_kernel,
        grid=(rows // ROW_TILE,),
        in_specs=[
            pl.BlockSpec((ROW_TILE, D_MODEL), lambda i: (i, 0)),
            pl.BlockSpec((ROW_TILE, POOL_WIDTH), lambda i: (i, 0)),
            pl.BlockSpec((ROW_TILE, MEM_WIDTH), lambda i: (i, q_col_block)),
            pl.BlockSpec((None, MEM_LEN, MEM_WIDTH), lambda i: (i // tiles_per_batch, 0, 0)),
            pl.BlockSpec((None, MEM_LEN, MEM_WIDTH), lambda i: (i // tiles_per_batch, 0, 1)),
            pl.BlockSpec((None, POOL_WIDTH, D_MODEL), lambda i: (layer, 0, 0), pipeline_mode=_RESIDENT),
            pl.BlockSpec((None, MEM_WIDTH, D_MODEL), lambda i: (layer, mem_row_block, 0),
                         pipeline_mode=_RESIDENT),
            pl.BlockSpec((1, D_MODEL), lambda i: (0, 0)),
        ],
        out_specs=[pl.BlockSpec((ROW_TILE, D_MODEL), lambda i: (i, 0)),
                   pl.BlockSpec((ROW_TILE, D_MODEL), lambda i: (i, 0))],
        out_shape=[jax.ShapeDtypeStruct((rows, D_MODEL), jnp.float32),
                   jax.ShapeDtypeStruct((rows, D_MODEL), jnp.bfloat16)],
        compiler_params=_params("arbitrary"),
        name="out_proj",
    )(x2d, y_mix, q_arr, kv, kv, w_stack, w_stack, ffn_gain)


def _ffn_kernel(x_ref, hn_ref, w1_ref, w2_ref, g_ref, o_ref, *, final_norm):
    @pl.when(pl.program_id(1) == 0)
    def _():
        o_ref[...] = x_ref[...]

    a = jnp.dot(hn_ref[...], w1_ref[...], preferred_element_type=jnp.float32)
    a = jnp.square(jnp.maximum(a, 0.0)).astype(jnp.bfloat16)
    o_ref[...] += jnp.dot(a, w2_ref[...], preferred_element_type=jnp.float32)

    if final_norm:
        @pl.when(pl.program_id(1) == pl.num_programs(1) - 1)
        def _():
            o_ref[...] = _rms_scale(o_ref[...], g_ref[...])


def _ffn(x2d, hn2d, w1_stack, w2_stack, layer, final_gain, final_norm):
    rows = x2d.shape[0]
    return pl.pallas_call(
        functools.partial(_ffn_kernel, final_norm=final_norm),
        grid=(rows // ROW_TILE, FFN_HIDDEN // FFN_TILE),
        in_specs=[
            pl.BlockSpec((ROW_TILE, D_MODEL), lambda i, j: (i, 0)),
            pl.BlockSpec((ROW_TILE, D_MODEL), lambda i, j: (i, 0)),
            pl.BlockSpec((None, D_MODEL, FFN_TILE), lambda i, j: (layer, 0, j)),
            pl.BlockSpec((None, FFN_TILE, D_MODEL), lambda i, j: (layer, j, 0)),
            pl.BlockSpec((1, D_MODEL), lambda i, j: (0, 0)),
        ],
        out_specs=pl.BlockSpec((ROW_TILE, D_MODEL), lambda i, j: (i, 0)),
        out_shape=jax.ShapeDtypeStruct((rows, D_MODEL), jnp.float32),
        compiler_params=_params("arbitrary", "arbitrary"),
        name="ffn_final" if final_norm else "ffn",
    )(x2d, hn2d, w1_stack, w2_stack, final_gain)


def kernel(x, mem, norm_mix, norm_mem, pool_w_in, pool_w_grp, pool_scale, fox_w_in, fox_b_f,
           w_mem_kv, w_out, norm_ffn, w_ffn1, w_ffn2, norm_final):
    b, s, d = x.shape
    rows = b * s
    depth = w_out.shape[0]
    bf16 = jnp.bfloat16
    f32 = jnp.float32
    x2d = x.reshape(rows, d)
    mem2d = mem.reshape(b * MEM_LEN, d)
    w_mem_kv_b, w_out_b = w_mem_kv.astype(bf16), w_out.astype(bf16)
    w_ffn1_b, w_ffn2_b = w_ffn1.astype(bf16), w_ffn2.astype(bf16)

    def row(v):
        return v.reshape(1, -1).astype(f32)

    def mix_tail(x2d, layer, y_mix, q_arr, q_col_block):
        kv = _mem_kv_proj(mem2d, row(norm_mem[layer]), w_mem_kv_b, layer).reshape(b, MEM_LEN, 2 * MEM_WIDTH)
        x2d, hn2d = _out_proj(x2d, y_mix.reshape(rows, POOL_WIDTH), q_arr.reshape(rows, -1), q_col_block,
                              kv, w_out_b, layer, row(norm_ffn[layer]))
        return _ffn(x2d, hn2d, w_ffn1_b, w_ffn2_b, layer, row(norm_final), layer == depth - 1)

    u, q_mem = _pool_in_proj(x2d, row(norm_mix[0]), pool_w_in.astype(bf16))
    y_mix = _pool_mix(u.reshape(b, s, POOL_WIDTH), pool_w_grp.astype(bf16), row(pool_scale[0]))
    x2d = mix_tail(x2d, 0, y_mix, q_mem.reshape(b, s, MEM_WIDTH), 0)

    w_in = fox_w_in[0]
    n_qkv = 3 * FOX_WIDTH
    w_qkv = w_in[:, :n_qkv].astype(bf16)
    w_qm = w_in[:, n_qkv + FOX_HEADS:].astype(bf16)
    w_f = jnp.pad(w_in[:, n_qkv:n_qkv + FOX_HEADS], ((0, 0), (0, LANES - FOX_HEADS))).astype(bf16)
    bias_f = jnp.pad(fox_b_f[0].astype(f32), (0, LANES - FOX_HEADS)).reshape(1, LANES)
    proj, qe, ke = _fox_in_proj(x2d, row(norm_mix[1]), w_qkv, w_qm, w_f, bias_f, b)
    proj = proj.reshape(b, s, FOX_PROJ_COLS)
    y_mix = _fox_attention(proj, qe, ke)
    x2d = mix_tail(x2d, 1, y_mix, proj, FOX_PROJ_COLS // MEM_WIDTH - 1)

    return x2d.reshape(b, s, d)
```

```python
import functools
import math

import jax
import jax.numpy as jnp
import numpy as np
from jax import lax
from jax.experimental import pallas as pl
from jax.experimental.pallas import tpu as pltpu

D_MODEL = 2048
HEAD_DIM = 128
MEM_LEN = 256
MEM_HEADS = 4
MEM_WIDTH = MEM_HEADS * HEAD_DIM
POOL_WIDTH = D_MODEL - MEM_WIDTH
POOL_GROUPS = 4
POOL_GROUP_WIDTH = POOL_WIDTH // POOL_GROUPS
POOL_WINDOWS = (2, 4, 8, 16)
POOL_HALO = 32
FOX_HEADS = POOL_WIDTH // HEAD_DIM
FOX_WIDTH = FOX_HEADS * HEAD_DIM
FOX_PROJ_COLS = 3 * FOX_WIDTH + MEM_WIDTH
FFN_HIDDEN = 4 * D_MODEL
RMS_EPS = 1e-6

LANES = 128
VMEM_LIMIT = 56 * 1024 * 1024

ROW_TILE = 512
FFN_TILE = 1024
FOX_TQ = 1024
FOX_TK = 512
FOX_UNROLLS = (16,)
CUMSUM_TILE = 256

LOG2E = math.log2(math.e)
FOX_QSCALE = LOG2E / math.sqrt(HEAD_DIM)
NEG_BIG = -1e30

_RESIDENT = pl.Buffered(1)


def _params(*sem):
    return pltpu.CompilerParams(dimension_semantics=sem, vmem_limit_bytes=VMEM_LIMIT)


def _rms_scale(x, gain):
    ms = jnp.mean(x * x, axis=-1, keepdims=True)
    return (x * lax.rsqrt(ms + RMS_EPS)) * gain


def _rms_normalize(x, gain):
    return _rms_scale(x, gain).astype(jnp.bfloat16)


def _pool_in_kernel(x_ref, g_ref, w_ref, u_ref, qm_ref):
    hn = _rms_normalize(x_ref[...], g_ref[...])
    res = jnp.dot(hn, w_ref[...], preferred_element_type=jnp.float32)
    u_ref[...] = res[:, :POOL_WIDTH]
    qm_ref[...] = res[:, POOL_WIDTH:].astype(jnp.bfloat16)


def _pool_in_proj(x2d, gain, w_stack):
    rows = x2d.shape[0]
    return pl.pallas_call(
        _pool_in_kernel,
        grid=(rows // ROW_TILE,),
        in_specs=[
            pl.BlockSpec((ROW_TILE, D_MODEL), lambda i: (i, 0)),
            pl.BlockSpec((1, D_MODEL), lambda i: (0, 0)),
            pl.BlockSpec((None, D_MODEL, D_MODEL), lambda i: (0, 0, 0), pipeline_mode=_RESIDENT),
        ],
        out_specs=[
            pl.BlockSpec((ROW_TILE, POOL_WIDTH), lambda i: (i, 0)),
            pl.BlockSpec((ROW_TILE, MEM_WIDTH), lambda i: (i, 0)),
        ],
        out_shape=[
            jax.ShapeDtypeStruct((rows, POOL_WIDTH), jnp.float32),
            jax.ShapeDtypeStruct((rows, MEM_WIDTH), jnp.bfloat16),
        ],
        compiler_params=_params("arbitrary"),
        name="pool_in_proj",
    )(x2d, gain, w_stack)


def _mem_kv_kernel(x_ref, g_ref, w_ref, o_ref):
    hn = _rms_normalize(x_ref[...], g_ref[...])
    o_ref[...] = jnp.dot(hn, w_ref[...], preferred_element_type=jnp.float32).astype(jnp.bfloat16)


def _mem_kv_proj(mem2d, gain, w_stack, layer):
    rows = mem2d.shape[0]
    n = w_stack.shape[2]
    return pl.pallas_call(
        _mem_kv_kernel,
        grid=(1,),
        in_specs=[
            pl.BlockSpec((rows, D_MODEL), lambda i: (0, 0)),
            pl.BlockSpec((1, D_MODEL), lambda i: (0, 0)),
            pl.BlockSpec((None, D_MODEL, n), lambda i: (layer, 0, 0)),
        ],
        out_specs=pl.BlockSpec((rows, n), lambda i: (0, 0)),
        out_shape=jax.ShapeDtypeStruct((rows, n), jnp.bfloat16),
        compiler_params=_params("arbitrary"),
        name="mem_kv_proj",
    )(mem2d, gain, w_stack)


PIECE_LANES = 16


def _piece_selector():
    sel = np.zeros((LANES, FOX_HEADS * LANES), np.float32)
    for h in range(FOX_HEADS):
        for piece in range(3):
            sel[piece * PIECE_LANES + h, h * LANES + piece] = 1.0
            sel[piece * PIECE_LANES + h, h * LANES + 3 + piece] = -1.0
    return jnp.asarray(sel, jnp.bfloat16)


def _bf16_pieces(x):
    hi = x.astype(jnp.bfloat16).astype(jnp.float32)
    r1 = x - hi
    mid = r1.astype(jnp.bfloat16).astype(jnp.float32)
    lo = (r1 - mid).astype(jnp.bfloat16).astype(jnp.float32)
    return hi, mid, lo


def _fox_in_kernel(x_ref, g_ref, wqkv_ref, wqm_ref, wf_ref, b_ref, sel_ref, o_ref, qe_ref, ke_ref, carry_ref,
                   *, tiles_per_batch):
    @pl.when(pl.program_id(0) % tiles_per_batch == 0)
    def _():
        carry_ref[...] = jnp.zeros_like(carry_ref)

    hn = _rms_normalize(x_ref[...], g_ref[...])

    z = jnp.dot(hn, wf_ref[...], preferred_element_type=jnp.float32) + b_ref[...]
    log_f = jnp.minimum(z, 0.0) - jnp.log1p(jnp.exp(-jnp.abs(z)))
    lf_pieces = jnp.concatenate(_bf16_pieces(log_f), axis=1).astype(jnp.bfloat16)
    tc = CUMSUM_TILE
    row = lax.broadcasted_iota(jnp.int32, (tc, tc), 0)
    col = lax.broadcasted_iota(jnp.int32, (tc, tc), 1)
    tri = (col <= row).astype(jnp.bfloat16)
    lane = lax.broadcasted_iota(jnp.int32, (tc, LANES), 1)
    k_side = (lane >= 3) & (lane < 6)
    q_ones = jnp.where(k_side, 1.0, 0.0)
    k_ones = jnp.where(lane < 3, 1.0, 0.0)
    mid_lanes = (lane >= PIECE_LANES) & (lane < PIECE_LANES + FOX_HEADS)
    lo_lanes = (lane >= 2 * PIECE_LANES) & (lane < 2 * PIECE_LANES + FOX_HEADS)
    total = carry_ref[...]
    for r0 in range(0, ROW_TILE, tc):
        part = jnp.dot(tri, lf_pieces[r0:r0 + tc], preferred_element_type=jnp.float32)
        csum = (part[:, :LANES] + part[:, LANES:2 * LANES]) + part[:, 2 * LANES:] + total
        total = csum[tc - 1:tc, :]
        hi, mid, lo = _bf16_pieces(csum * LOG2E)
        packed = jnp.where(lane < FOX_HEADS, hi,
                           jnp.where(mid_lanes, pltpu.roll(mid, PIECE_LANES, 1),
                                     jnp.where(lo_lanes, pltpu.roll(lo, 2 * PIECE_LANES, 1), 0.0)))
        placed = jnp.dot(packed.astype(jnp.bfloat16), sel_ref[...],
                         preferred_element_type=jnp.float32)
        for h in range(FOX_HEADS):
            blk = placed[:, h * LANES:(h + 1) * LANES]
            qe_ref[h, r0:r0 + tc, :] = jnp.where(lane < 3, blk, q_ones).astype(jnp.bfloat16)
            ke_ref[h, r0:r0 + tc, :] = jnp.where(k_side, blk, k_ones).astype(jnp.bfloat16)
    carry_ref[...] = total

    for part, scale in enumerate((FOX_QSCALE, None, None)):
        cols = slice(part * FOX_WIDTH, (part + 1) * FOX_WIDTH)
        res = jnp.dot(hn, wqkv_ref[:, cols], preferred_element_type=jnp.float32)
        if scale is not None:
            res = res * scale
        o_ref[:, cols] = res.astype(jnp.bfloat16)
    o_ref[:, 3 * FOX_WIDTH:] = jnp.dot(hn, wqm_ref[...], preferred_element_type=jnp.float32).astype(jnp.bfloat16)


def _fox_in_proj(x2d, gain, w_qkv, w_qm, w_f, bias, batch):
    rows = x2d.shape[0]
    seq = rows // batch
    tiles_per_batch = seq // ROW_TILE
    ext = jax.ShapeDtypeStruct((batch, FOX_HEADS, seq, LANES), jnp.bfloat16)
    ext_spec = pl.BlockSpec((None, FOX_HEADS, ROW_TILE, LANES),
                            lambda i: (i // tiles_per_batch, 0, i % tiles_per_batch, 0))
    return pl.pallas_call(
        functools.partial(_fox_in_kernel, tiles_per_batch=tiles_per_batch),
        grid=(rows // ROW_TILE,),
        in_specs=[
            pl.BlockSpec((ROW_TILE, D_MODEL), lambda i: (i, 0)),
            pl.BlockSpec((1, D_MODEL), lambda i: (0, 0)),
            pl.BlockSpec((D_MODEL, 3 * FOX_WIDTH), lambda i: (0, 0), pipeline_mode=_RESIDENT),
            pl.BlockSpec((D_MODEL, MEM_WIDTH), lambda i: (0, 0), pipeline_mode=_RESIDENT),
            pl.BlockSpec((D_MODEL, LANES), lambda i: (0, 0), pipeline_mode=_RESIDENT),
            pl.BlockSpec((1, LANES), lambda i: (0, 0)),
            pl.BlockSpec((LANES, FOX_HEADS * LANES), lambda i: (0, 0), pipeline_mode=_RESIDENT),
        ],
        out_specs=[pl.BlockSpec((ROW_TILE, FOX_PROJ_COLS), lambda i: (i, 0)), ext_spec, ext_spec],
        out_shape=[jax.ShapeDtypeStruct((rows, FOX_PROJ_COLS), jnp.bfloat16), ext, ext],
        scratch_shapes=[pltpu.VMEM((1, LANES), jnp.float32)],
        compiler_params=_params("arbitrary"),
        name="fox_in_proj",
    )(x2d, gain, w_qkv, w_qm, w_f, bias, _piece_selector())


def _pool_mix_kernel(u_ref, w_ref, s_ref, y_ref, ext_ref, lvl_ref):
    i = pl.program_id(1)
    tm = u_ref.shape[0]
    rows = tm + POOL_HALO

    @pl.when(i == 0)
    def _():
        ext_ref[0:POOL_HALO, :] = jnp.zeros((POOL_HALO, POOL_WIDTH), jnp.float32)

    @pl.when(i > 0)
    def _():
        ext_ref[0:POOL_HALO, :] = ext_ref[tm:tm + POOL_HALO, :]

    ext_ref[POOL_HALO:, :] = u_ref[...]

    t = i * tm + lax.broadcasted_iota(jnp.int32, (tm, 1), 0)
    for g, win in enumerate(POOL_WINDOWS):
        c0 = g * POOL_GROUP_WIDTH
        c1 = c0 + POOL_GROUP_WIDTH
        tok = ext_ref[POOL_HALO:, c0:c1]
        levels = win.bit_length() - 1
        wsum = None
        for lvl in range(levels):
            shift = 1 << lvl
            start = POOL_HALO - 8 * (levels - 1 - lvl)
            if lvl == 0:
                cur = ext_ref[start:, c0:c1] + ext_ref[start - shift:rows - shift, c0:c1]
            else:
                src = lvl_ref.at[(lvl - 1) % 2]
                cur = src[start:, :] + src[start - shift:rows - shift, :]
            if lvl + 1 < levels:
                lvl_ref[lvl % 2, start:, :] = cur
            else:
                wsum = cur
        count = jnp.minimum(t + 1, win).astype(jnp.float32)
        pooled = wsum / count - tok
        mixed = jnp.dot(pooled.astype(jnp.bfloat16), w_ref[g], preferred_element_type=jnp.float32)
        y_ref[:, c0:c1] = (mixed * s_ref[:, c0:c1]).astype(jnp.bfloat16)


def _pool_mix(u, w_grp_stack, scale):
    b, s, _ = u.shape
    assert all(w & (w - 1) == 0 for w in POOL_WINDOWS)
    assert POOL_HALO == 8 * (max(POOL_WINDOWS).bit_length() - 1)
    return pl.pallas_call(
        _pool_mix_kernel,
        grid=(b, s // ROW_TILE),
        in_specs=[
            pl.BlockSpec((None, ROW_TILE, POOL_WIDTH), lambda bi, i: (bi, i, 0)),
            pl.BlockSpec((None, POOL_GROUPS, POOL_GROUP_WIDTH, POOL_GROUP_WIDTH), lambda bi, i: (0, 0, 0, 0)),
            pl.BlockSpec((1, POOL_WIDTH), lambda bi, i: (0, 0)),
        ],
        out_specs=pl.BlockSpec((None, ROW_TILE, POOL_WIDTH), lambda bi, i: (bi, i, 0)),
        out_shape=jax.ShapeDtypeStruct((b, s, POOL_WIDTH), jnp.bfloat16),
        scratch_shapes=[pltpu.VMEM((ROW_TILE + POOL_HALO, POOL_WIDTH), jnp.float32),
                        pltpu.VMEM((2, ROW_TILE + POOL_HALO, POOL_GROUP_WIDTH), jnp.float32)],
        compiler_params=_params("arbitrary", "arbitrary"),
        name="pool_mix",
    )(u, w_grp_stack, scale)


def _fox_attn_kernel(q_ref, qe_ref, k_ref, ke_ref, v_ref, o_ref):
    i = pl.program_id(2)
    tq, tk = FOX_TQ, FOX_TK
    per_tile = tq // tk
    q2 = jnp.concatenate([q_ref[...], qe_ref[...]], axis=1)
    lane = lax.broadcasted_iota(jnp.int32, (tk, LANES), 1)
    v_ext = jnp.where(lane == 0, 1.0, 0.0).astype(jnp.bfloat16)

    def block(j, carry, row0=0, masked=False):
        m, acc = carry
        ks = pl.ds(pl.multiple_of(j * tk, tk), tk)
        k2 = jnp.concatenate([k_ref[ks, :], ke_ref[ks, :]], axis=1)
        v2 = jnp.concatenate([v_ref[ks, :], v_ext], axis=1)
        s = lax.dot_general(q2[row0:], k2, (((1,), (1,)), ((), ())), preferred_element_type=jnp.float32)
        if masked:
            row = lax.broadcasted_iota(jnp.int32, s.shape, 0)
            col = lax.broadcasted_iota(jnp.int32, s.shape, 1)
            s = jnp.where(col <= row, s, -jnp.inf)
        m_old = m[row0:]
        m_new = jnp.maximum(m_old, jnp.max(s, axis=-1, keepdims=True))
        alpha = jnp.exp2(m_old - m_new)
        p = jnp.exp2(s - m_new).astype(jnp.bfloat16)
        acc_new = alpha * acc[row0:] + jnp.dot(p, v2, preferred_element_type=jnp.float32)
        if row0:
            m_new = jnp.concatenate([m[:row0], m_new], axis=0)
            acc_new = jnp.concatenate([acc[:row0], acc_new], axis=0)
        return m_new, acc_new

    carry = (jnp.full((tq, 1), NEG_BIG, jnp.float32),
             jnp.zeros((tq, 2 * HEAD_DIM), jnp.float32))
    n_full = i * per_tile

    done = 0
    for n in FOX_UNROLLS:
        def body(t, c, n=n, base=done):
            for u in range(n):
                c = block(base + t * n + u, c)
            return c
        trips = (n_full - done) // n
        carry = lax.fori_loop(0, trips, body, carry)
        done = done + trips * n

    def tail(c, n_rest):
        for u in range(n_rest):
            c = block(done + u, c)
        for d in range(per_tile):
            c = block(i * per_tile + d, c, row0=d * tk, masked=True)
        _, acc = c
        o_ref[...] = (acc[:, :HEAD_DIM] / acc[:, HEAD_DIM:HEAD_DIM + 1]).astype(jnp.bfloat16)

    rest = n_full - done
    for n_rest in range(0, FOX_UNROLLS[-1], per_tile):
        pl.when(rest == n_rest)(functools.partial(tail, carry, n_rest))


def _fox_attention(proj, qe, ke):
    b, s, _ = proj.shape
    assert FOX_TQ % FOX_TK == 0 and all(n % (FOX_TQ // FOX_TK) == 0 for n in FOX_UNROLLS)
    return pl.pallas_call(
        _fox_attn_kernel,
        grid=(b, FOX_HEADS, s // FOX_TQ),
        in_specs=[
            pl.BlockSpec((None, FOX_TQ, HEAD_DIM), lambda bi, h, i: (bi, i, h)),
            pl.BlockSpec((None, None, FOX_TQ, LANES), lambda bi, h, i: (bi, h, i, 0)),
            pl.BlockSpec((None, s, HEAD_DIM), lambda bi, h, i: (bi, 0, FOX_HEADS + h)),
            pl.BlockSpec((None, None, s, LANES), lambda bi, h, i: (bi, h, 0, 0)),
            pl.BlockSpec((None, s, HEAD_DIM), lambda bi, h, i: (bi, 0, 2 * FOX_HEADS + h)),
        ],
        out_specs=pl.BlockSpec((None, FOX_TQ, HEAD_DIM), lambda bi, h, i: (bi, i, h)),
        out_shape=jax.ShapeDtypeStruct((b, s, FOX_WIDTH), jnp.bfloat16),
        compiler_params=_params("arbitrary", "arbitrary", "arbitrary"),
        name="fox_attention",
    )(proj, qe, proj, ke, proj)


def _mem_attention(q_ref, k_ref, v_ref):
    inv_sqrt = 1.0 / math.sqrt(HEAD_DIM)
    heads = []
    for h in range(MEM_HEADS):
        c0 = h * HEAD_DIM
        c1 = c0 + HEAD_DIM
        logits = lax.dot_general(q_ref[:, c0:c1], k_ref[:, c0:c1], (((1,), (1,)), ((), ())),
                                 preferred_element_type=jnp.float32)
        logits = logits * inv_sqrt
        m = jnp.max(logits, axis=-1, keepdims=True)
        e = jnp.exp(logits - m)
        p = e / jnp.sum(e, axis=-1, keepdims=True)
        out = jnp.dot(p.astype(jnp.bfloat16), v_ref[:, c0:c1], preferred_element_type=jnp.float32)
        heads.append(out.astype(jnp.bfloat16))
    return jnp.concatenate(heads, axis=1)


def _out_proj_kernel(x_ref, ya_ref, q_ref, k_ref, v_ref, wa_ref, wb_ref, g_ref, o_ref, hn_ref):
    acc = jnp.dot(ya_ref[...], wa_ref[...], preferred_element_type=jnp.float32)
    y_mem = _mem_attention(q_ref, k_ref, v_ref)
    acc = acc + jnp.dot(y_mem, wb_ref[...], preferred_element_type=jnp.float32)
    x_new = x_ref[...] + acc
    o_ref[...] = x_new
    hn_ref[...] = _rms_normalize(x_new, g_ref[...])


def _out_proj(x2d, y_mix, q_arr, q_col_block, kv, w_stack, layer, ffn_gain):
    rows = x2d.shape[0]
    tiles_per_batch = rows // kv.shape[0] // ROW_TILE
    mem_row_block = POOL_WIDTH // MEM_WIDTH
    return pl.pallas_call(
        _out_proj_kernel,
        grid=(rows // ROW_TILE,),
        in_specs=[
            pl.BlockSpec((ROW_TILE, D_MODEL), lambda i: (i, 0)),
            pl.BlockSpec((ROW_TILE, POOL_WIDTH), lambda i: (i, 0)),
            pl.BlockSpec((ROW_TILE, MEM_WIDTH), lambda i: (i, q_col_block)),
            pl.BlockSpec((None, MEM_LEN, MEM_WIDTH), lambda i: (i // tiles_per_batch, 0, 0)),
            pl.BlockSpec((None, MEM_LEN, MEM_WIDTH), lambda i: (i // tiles_per_batch, 0, 1)),
            pl.BlockSpec((None, POOL_WIDTH, D_MODEL), lambda i: (layer, 0, 0), pipeline_mode=_RESIDENT),
            pl.BlockSpec((None, MEM_WIDTH, D_MODEL), lambda i: (layer, mem_row_block, 0),
                         pipeline_mode=_RESIDENT),
            pl.BlockSpec((1, D_MODEL), lambda i: (0, 0)),
        ],
        out_specs=[pl.BlockSpec((ROW_TILE, D_MODEL), lambda i: (i, 0)),
                   pl.BlockSpec((ROW_TILE, D_MODEL), lambda i: (i, 0))],
        out_shape=[jax.ShapeDtypeStruct((rows, D_MODEL), jnp.float32),
                   jax.ShapeDtypeStruct((rows, D_MODEL), jnp.bfloat16)],
        compiler_params=_params("arbitrary"),
        name="out_proj",
    )(x2d, y_mix, q_arr, kv, kv, w_stack, w_stack, ffn_gain)


def _ffn_kernel(x_ref, hn_ref, w1_ref, w2_ref, g_ref, o_ref, *, final_norm):
    @pl.when(pl.program_id(1) == 0)
    def _():
        o_ref[...] = x_ref[...]

    a = jnp.dot(hn_ref[...], w1_ref[...], preferred_element_type=jnp.float32)
    a = jnp.square(jnp.maximum(a, 0.0)).astype(jnp.bfloat16)
    o_ref[...] += jnp.dot(a, w2_ref[...], preferred_element_type=jnp.float32)

    if final_norm:
        @pl.when(pl.program_id(1) == pl.num_programs(1) - 1)
        def _():
            o_ref[...] = _rms_scale(o_ref[...], g_ref[...])


def _ffn(x2d, hn2d, w1_stack, w2_stack, layer, final_gain, final_norm):
    rows = x2d.shape[0]
    return pl.pallas_call(
        functools.partial(_ffn_kernel, final_norm=final_norm),
        grid=(rows // ROW_TILE, FFN_HIDDEN // FFN_TILE),
        in_specs=[
            pl.BlockSpec((ROW_TILE, D_MODEL), lambda i, j: (i, 0)),
            pl.BlockSpec((ROW_TILE, D_MODEL), lambda i, j: (i, 0)),
            pl.BlockSpec((None, D_MODEL, FFN_TILE), lambda i, j: (layer, 0, j)),
            pl.BlockSpec((None, FFN_TILE, D_MODEL), lambda i, j: (layer, j, 0)),
            pl.BlockSpec((1, D_MODEL), lambda i, j: (0, 0)),
        ],
        out_specs=pl.BlockSpec((ROW_TILE, D_MODEL), lambda i, j: (i, 0)),
        out_shape=jax.ShapeDtypeStruct((rows, D_MODEL), jnp.float32),
        compiler_params=_params("arbitrary", "arbitrary"),
        name="ffn_final" if final_norm else "ffn",
    )(x2d, hn2d, w1_stack, w2_stack, final_gain)


def kernel(x, mem, norm_mix, norm_mem, pool_w_in, pool_w_grp, pool_scale, fox_w_in, fox_b_f,
           w_mem_kv, w_out, norm_ffn, w_ffn1, w_ffn2, norm_final):
    b, s, d = x.shape
    rows = b * s
    depth = w_out.shape[0]
    bf16 = jnp.bfloat16
    f32 = jnp.float32
    x2d = x.reshape(rows, d)
    mem2d = mem.reshape(b * MEM_LEN, d)
    w_mem_kv_b, w_out_b = w_mem_kv.astype(bf16), w_out.astype(bf16)
    w_ffn1_b, w_ffn2_b = w_ffn1.astype(bf16), w_ffn2.astype(bf16)

    def row(v):
        return v.reshape(1, -1).astype(f32)

    def mix_tail(x2d, layer, y_mix, q_arr, q_col_block):
        kv = _mem_kv_proj(mem2d, row(norm_mem[layer]), w_mem_kv_b, layer).reshape(b, MEM_LEN, 2 * MEM_WIDTH)
        x2d, hn2d = _out_proj(x2d, y_mix.reshape(rows, POOL_WIDTH), q_arr.reshape(rows, -1), q_col_block,
                              kv, w_out_b, layer, row(norm_ffn[layer]))
        return _ffn(x2d, hn2d, w_ffn1_b, w_ffn2_b, layer, row(norm_final), layer == depth - 1)

    u, q_mem = _pool_in_proj(x2d, row(norm_mix[0]), pool_w_in.astype(bf16))
    y_mix = _pool_mix(u.reshape(b, s, POOL_WIDTH), pool_w_grp.astype(bf16), row(pool_scale[0]))
    x2d = mix_tail(x2d, 0, y_mix, q_mem.reshape(b, s, MEM_WIDTH), 0)

    w_in = fox_w_in[0]
    n_qkv = 3 * FOX_WIDTH
    w_qkv = w_in[:, :n_qkv].astype(bf16)
    w_qm = w_in[:, n_qkv + FOX_HEADS:].astype(bf16)
    w_f = jnp.pad(w_in[:, n_qkv:n_qkv + FOX_HEADS], ((0, 0), (0, LANES - FOX_HEADS))).astype(bf16)
    bias_f = jnp.pad(fox_b_f[0].astype(f32), (0, LANES - FOX_HEADS)).reshape(1, LANES)
    proj, qe, ke = _fox_in_proj(x2d, row(norm_mix[1]), w_qkv, w_qm, w_f, bias_f, b)
    proj = proj.reshape(b, s, FOX_PROJ_COLS)
    y_mix = _fox_attention(proj, qe, ke)
    x2d = mix_tail(x2d, 1, y_mix, proj, FOX_PROJ_COLS // MEM_WIDTH - 1)

    return x2d.reshape(b, s, d)
```

```python
import functools
import math

import jax
import jax.numpy as jnp
import numpy as np
from jax import lax
from jax.experimental import pallas as pl
from jax.experimental.pallas import tpu as pltpu

D_MODEL = 2048
HEAD_DIM = 128
MEM_LEN = 256
MEM_HEADS = 4
MEM_WIDTH = MEM_HEADS * HEAD_DIM
POOL_WIDTH = D_MODEL - MEM_WIDTH
POOL_GROUPS = 4
POOL_GROUP_WIDTH = POOL_WIDTH // POOL_GROUPS
POOL_WINDOWS = (2, 4, 8, 16)
POOL_HALO = 32
FOX_HEADS = POOL_WIDTH // HEAD_DIM
FOX_WIDTH = FOX_HEADS * HEAD_DIM
FOX_PROJ_COLS = 3 * FOX_WIDTH + MEM_WIDTH
FFN_HIDDEN = 4 * D_MODEL
RMS_EPS = 1e-6

LANES = 128
VMEM_LIMIT = 56 * 1024 * 1024

ROW_TILE = 512
FFN_TILE = 1024
FOX_TQ = 1024
FOX_TK = 512
FOX_UNROLLS = (8,)
CUMSUM_TILE = 256

LOG2E = math.log2(math.e)
FOX_QSCALE = LOG2E / math.sqrt(HEAD_DIM)
NEG_BIG = -1e30

_RESIDENT = pl.Buffered(1)


def _params(*sem):
    return pltpu.CompilerParams(dimension_semantics=sem, vmem_limit_bytes=VMEM_LIMIT)


def _rms_scale(x, gain):
    ms = jnp.mean(x * x, axis=-1, keepdims=True)
    return (x * lax.rsqrt(ms + RMS_EPS)) * gain


def _rms_normalize(x, gain):
    return _rms_scale(x, gain).astype(jnp.bfloat16)


def _pool_in_kernel(x_ref, g_ref, w_ref, u_ref, qm_ref):
    hn = _rms_normalize(x_ref[...], g_ref[...])
    res = jnp.dot(hn, w_ref[...], preferred_element_type=jnp.float32)
    u_ref[...] = res[:, :POOL_WIDTH]
    qm_ref[...] = res[:, POOL_WIDTH:].astype(jnp.bfloat16)


def _pool_in_proj(x2d, gain, w_stack):
    rows = x2d.shape[0]
    return pl.pallas_call(
        _pool_in_kernel,
        grid=(rows // ROW_TILE,),
        in_specs=[
            pl.BlockSpec((ROW_TILE, D_MODEL), lambda i: (i, 0)),
            pl.BlockSpec((1, D_MODEL), lambda i: (0, 0)),
            pl.BlockSpec((None, D_MODEL, D_MODEL), lambda i: (0, 0, 0), pipeline_mode=_RESIDENT),
        ],
        out_specs=[
            pl.BlockSpec((ROW_TILE, POOL_WIDTH), lambda i: (i, 0)),
            pl.BlockSpec((ROW_TILE, MEM_WIDTH), lambda i: (i, 0)),
        ],
        out_shape=[
            jax.ShapeDtypeStruct((rows, POOL_WIDTH), jnp.float32),
            jax.ShapeDtypeStruct((rows, MEM_WIDTH), jnp.bfloat16),
        ],
        compiler_params=_params("arbitrary"),
        name="pool_in_proj",
    )(x2d, gain, w_stack)


def _mem_kv_kernel(x_ref, g_ref, w_ref, o_ref):
    hn = _rms_normalize(x_ref[...], g_ref[...])
    o_ref[...] = jnp.dot(hn, w_ref[...], preferred_element_type=jnp.float32).astype(jnp.bfloat16)


def _mem_kv_proj(mem2d, gain, w_stack, layer):
    rows = mem2d.shape[0]
    n = w_stack.shape[2]
    return pl.pallas_call(
        _mem_kv_kernel,
        grid=(1,),
        in_specs=[
            pl.BlockSpec((rows, D_MODEL), lambda i: (0, 0)),
            pl.BlockSpec((1, D_MODEL), lambda i: (0, 0)),
            pl.BlockSpec((None, D_MODEL, n), lambda i: (layer, 0, 0)),
        ],
        out_specs=pl.BlockSpec((rows, n), lambda i: (0, 0)),
        out_shape=jax.ShapeDtypeStruct((rows, n), jnp.bfloat16),
        compiler_params=_params("arbitrary"),
        name="mem_kv_proj",
    )(mem2d, gain, w_stack)


PIECE_LANES = 16


def _piece_selector():
    sel = np.zeros((LANES, FOX_HEADS * LANES), np.float32)
    for h in range(FOX_HEADS):
        for piece in range(3):
            sel[piece * PIECE_LANES + h, h * LANES + piece] = 1.0
            sel[piece * PIECE_LANES + h, h * LANES + 3 + piece] = -1.0
    return jnp.asarray(sel, jnp.bfloat16)


def _bf16_pieces(x):
    hi = x.astype(jnp.bfloat16).astype(jnp.float32)
    r1 = x - hi
    mid = r1.astype(jnp.bfloat16).astype(jnp.float32)
    lo = (r1 - mid).astype(jnp.bfloat16).astype(jnp.float32)
    return hi, mid, lo


def _fox_in_kernel(x_ref, g_ref, wqkv_ref, wqm_ref, wf_ref, b_ref, sel_ref, o_ref, qe_ref, ke_ref,
                   z_ref, carry_ref, *, tiles_per_batch):
    i = pl.program_id(0)

    @pl.when(i == 0)
    def _():
        z_ref[...] = jnp.zeros_like(z_ref)

    @pl.when((i == 0) | ((i - 1) % tiles_per_batch == 0))
    def _():
        carry_ref[...] = jnp.zeros_like(carry_ref)

    z = z_ref[...]
    log_f = jnp.minimum(z, 0.0) - jnp.log1p(jnp.exp(-jnp.abs(z)))
    lf_pieces = jnp.concatenate(_bf16_pieces(log_f), axis=1).astype(jnp.bfloat16)
    tc = CUMSUM_TILE
    row = lax.broadcasted_iota(jnp.int32, (tc, tc), 0)
    col = lax.broadcasted_iota(jnp.int32, (tc, tc), 1)
    tri = (col <= row).astype(jnp.bfloat16)
    lane = lax.broadcasted_iota(jnp.int32, (tc, LANES), 1)
    k_side = (lane >= 3) & (lane < 6)
    q_ones = jnp.where(k_side, 1.0, 0.0)
    k_ones = jnp.where(lane < 3, 1.0, 0.0)
    mid_lanes = (lane >= PIECE_LANES) & (lane < PIECE_LANES + FOX_HEADS)
    lo_lanes = (lane >= 2 * PIECE_LANES) & (lane < 2 * PIECE_LANES + FOX_HEADS)
    total = carry_ref[...]
    for r0 in range(0, ROW_TILE, tc):
        part = jnp.dot(tri, lf_pieces[r0:r0 + tc], preferred_element_type=jnp.float32)
        csum = (part[:, :LANES] + part[:, LANES:2 * LANES]) + part[:, 2 * LANES:] + total
        total = csum[tc - 1:tc, :]
        hi, mid, lo = _bf16_pieces(csum * LOG2E)
        packed = jnp.where(lane < FOX_HEADS, hi,
                           jnp.where(mid_lanes, pltpu.roll(mid, PIECE_LANES, 1),
                                     jnp.where(lo_lanes, pltpu.roll(lo, 2 * PIECE_LANES, 1), 0.0)))
        placed = jnp.dot(packed.astype(jnp.bfloat16), sel_ref[...],
                         preferred_element_type=jnp.float32)
        for h in range(FOX_HEADS):
            blk = placed[:, h * LANES:(h + 1) * LANES]
            qe_ref[h, r0:r0 + tc, :] = jnp.where(lane < 3, blk, q_ones).astype(jnp.bfloat16)
            ke_ref[h, r0:r0 + tc, :] = jnp.where(k_side, blk, k_ones).astype(jnp.bfloat16)
    carry_ref[...] = total

    hn = _rms_normalize(x_ref[...], g_ref[...])
    z_ref[...] = jnp.dot(hn, wf_ref[...], preferred_element_type=jnp.float32) + b_ref[...]
    for part, scale in enumerate((FOX_QSCALE, None, None)):
        cols = slice(part * FOX_WIDTH, (part + 1) * FOX_WIDTH)
        res = jnp.dot(hn, wqkv_ref[:, cols], preferred_element_type=jnp.float32)
        if scale is not None:
            res = res * scale
        o_ref[:, cols] = res.astype(jnp.bfloat16)
    o_ref[:, 3 * FOX_WIDTH:] = jnp.dot(hn, wqm_ref[...], preferred_element_type=jnp.float32).astype(jnp.bfloat16)


def _fox_in_proj(x2d, gain, w_qkv, w_qm, w_f, bias, batch):
    rows = x2d.shape[0]
    seq = rows // batch
    tiles_per_batch = seq // ROW_TILE
    last = rows // ROW_TILE - 1
    ext = jax.ShapeDtypeStruct((batch, FOX_HEADS, seq, LANES), jnp.bfloat16)

    def gate_tile(i):
        t = jnp.maximum(i - 1, 0)
        return (t // tiles_per_batch, 0, t % tiles_per_batch, 0)

    ext_spec = pl.BlockSpec((None, FOX_HEADS, ROW_TILE, LANES), gate_tile)
    return pl.pallas_call(
        functools.partial(_fox_in_kernel, tiles_per_batch=tiles_per_batch),
        grid=(last + 2,),
        in_specs=[
            pl.BlockSpec((ROW_TILE, D_MODEL), lambda i: (jnp.minimum(i, last), 0)),
            pl.BlockSpec((1, D_MODEL), lambda i: (0, 0)),
            pl.BlockSpec((D_MODEL, 3 * FOX_WIDTH), lambda i: (0, 0), pipeline_mode=_RESIDENT),
            pl.BlockSpec((D_MODEL, MEM_WIDTH), lambda i: (0, 0), pipeline_mode=_RESIDENT),
            pl.BlockSpec((D_MODEL, LANES), lambda i: (0, 0), pipeline_mode=_RESIDENT),
            pl.BlockSpec((1, LANES), lambda i: (0, 0)),
            pl.BlockSpec((LANES, FOX_HEADS * LANES), lambda i: (0, 0), pipeline_mode=_RESIDENT),
        ],
        out_specs=[pl.BlockSpec((ROW_TILE, FOX_PROJ_COLS), lambda i: (jnp.minimum(i, last), 0)),
                   ext_spec, ext_spec],
        out_shape=[jax.ShapeDtypeStruct((rows, FOX_PROJ_COLS), jnp.bfloat16), ext, ext],
        scratch_shapes=[pltpu.VMEM((ROW_TILE, LANES), jnp.float32), pltpu.VMEM((1, LANES), jnp.float32)],
        compiler_params=_params("arbitrary"),
        name="fox_in_proj",
    )(x2d, gain, w_qkv, w_qm, w_f, bias, _piece_selector())


def _pool_mix_kernel(u_ref, w_ref, s_ref, y_ref, ext_ref, lvl_ref):
    i = pl.program_id(1)
    tm = u_ref.shape[0]
    rows = tm + POOL_HALO

    @pl.when(i == 0)
    def _():
        ext_ref[0:POOL_HALO, :] = jnp.zeros((POOL_HALO, POOL_WIDTH), jnp.float32)

    @pl.when(i > 0)
    def _():
        ext_ref[0:POOL_HALO, :] = ext_ref[tm:tm + POOL_HALO, :]

    ext_ref[POOL_HALO:, :] = u_ref[...]

    t = i * tm + lax.broadcasted_iota(jnp.int32, (tm, 1), 0)
    for g, win in enumerate(POOL_WINDOWS):
        c0 = g * POOL_GROUP_WIDTH
        c1 = c0 + POOL_GROUP_WIDTH
        tok = ext_ref[POOL_HALO:, c0:c1]
        levels = win.bit_length() - 1
        wsum = None
        for lvl in range(levels):
            shift = 1 << lvl
            start = POOL_HALO - 8 * (levels - 1 - lvl)
            if lvl == 0:
                cur = ext_ref[start:, c0:c1] + ext_ref[start - shift:rows - shift, c0:c1]
            else:
                src = lvl_ref.at[(lvl - 1) % 2]
                cur = src[start:, :] + src[start - shift:rows - shift, :]
            if lvl + 1 < levels:
                lvl_ref[lvl % 2, start:, :] = cur
            else:
                wsum = cur
        count = jnp.minimum(t + 1, win).astype(jnp.float32)
        pooled = wsum / count - tok
        mixed = jnp.dot(pooled.astype(jnp.bfloat16), w_ref[g], preferred_element_type=jnp.float32)
        y_ref[:, c0:c1] = (mixed * s_ref[:, c0:c1]).astype(jnp.bfloat16)


def _pool_mix(u, w_grp_stack, scale):
    b, s, _ = u.shape
    assert all(w & (w - 1) == 0 for w in POOL_WINDOWS)
    assert POOL_HALO == 8 * (max(POOL_WINDOWS).bit_length() - 1)
    return pl.pallas_call(
        _pool_mix_kernel,
        grid=(b, s // ROW_TILE),
        in_specs=[
            pl.BlockSpec((None, ROW_TILE, POOL_WIDTH), lambda bi, i: (bi, i, 0)),
            pl.BlockSpec((None, POOL_GROUPS, POOL_GROUP_WIDTH, POOL_GROUP_WIDTH), lambda bi, i: (0, 0, 0, 0)),
            pl.BlockSpec((1, POOL_WIDTH), lambda bi, i: (0, 0)),
        ],
        out_specs=pl.BlockSpec((None, ROW_TILE, POOL_WIDTH), lambda bi, i: (bi, i, 0)),
        out_shape=jax.ShapeDtypeStruct((b, s, POOL_WIDTH), jnp.bfloat16),
        scratch_shapes=[pltpu.VMEM((ROW_TILE + POOL_HALO, POOL_WIDTH), jnp.float32),
                        pltpu.VMEM((2, ROW_TILE + POOL_HALO, POOL_GROUP_WIDTH), jnp.float32)],
        compiler_params=_params("arbitrary", "arbitrary"),
        name="pool_mix",
    )(u, w_grp_stack, scale)


def _fox_attn_kernel(q_ref, qe_ref, k_ref, ke_ref, v_ref, o_ref):
    i = pl.program_id(2)
    tq, tk = FOX_TQ, FOX_TK
    per_tile = tq // tk
    q2 = jnp.concatenate([q_ref[...], qe_ref[...]], axis=1)
    lane = lax.broadcasted_iota(jnp.int32, (tk, LANES), 1)
    v_ext = jnp.where(lane == 0, 1.0, 0.0).astype(jnp.bfloat16)

    def block(j, carry, row0=0, masked=False):
        m, acc = carry
        ks = pl.ds(pl.multiple_of(j * tk, tk), tk)
        k2 = jnp.concatenate([k_ref[ks, :], ke_ref[ks, :]], axis=1)
        v2 = jnp.concatenate([v_ref[ks, :], v_ext], axis=1)
        s = lax.dot_general(q2[row0:], k2, (((1,), (1,)), ((), ())), preferred_element_type=jnp.float32)
        if masked:
            row = lax.broadcasted_iota(jnp.int32, s.shape, 0)
            col = lax.broadcasted_iota(jnp.int32, s.shape, 1)
            s = jnp.where(col <= row, s, -jnp.inf)
        m_old = m[row0:]
        m_new = jnp.maximum(m_old, jnp.max(s, axis=-1, keepdims=True))
        alpha = jnp.exp2(m_old - m_new)
        p = jnp.exp2(s - m_new).astype(jnp.bfloat16)
        acc_new = alpha * acc[row0:] + jnp.dot(p, v2, preferred_element_type=jnp.float32)
        if row0:
            m_new = jnp.concatenate([m[:row0], m_new], axis=0)
            acc_new = jnp.concatenate([acc[:row0], acc_new], axis=0)
        return m_new, acc_new

    carry = (jnp.full((tq, 1), NEG_BIG, jnp.float32),
             jnp.zeros((tq, 2 * HEAD_DIM), jnp.float32))
    n_full = i * per_tile

    done = 0
    for n in FOX_UNROLLS:
        def body(t, c, n=n, base=done):
            for u in range(n):
                c = block(base + t * n + u, c)
            return c
        trips = (n_full - done) // n
        carry = lax.fori_loop(0, trips, body, carry)
        done = done + trips * n

    def tail(c, n_rest):
        for u in range(n_rest):
            c = block(done + u, c)
        for d in range(per_tile):
            c = block(i * per_tile + d, c, row0=d * tk, masked=True)
        _, acc = c
        o_ref[...] = (acc[:, :HEAD_DIM] / acc[:, HEAD_DIM:HEAD_DIM + 1]).astype(jnp.bfloat16)

    rest = n_full - done
    for n_rest in range(0, FOX_UNROLLS[-1], per_tile):
        pl.when(rest == n_rest)(functools.partial(tail, carry, n_rest))


def _fox_attention(proj, qe, ke):
    b, s, _ = proj.shape
    assert FOX_TQ % FOX_TK == 0 and all(n % (FOX_TQ // FOX_TK) == 0 for n in FOX_UNROLLS)
    return pl.pallas_call(
        _fox_attn_kernel,
        grid=(b, FOX_HEADS, s // FOX_TQ),
        in_specs=[
            pl.BlockSpec((None, FOX_TQ, HEAD_DIM), lambda bi, h, i: (bi, i, h)),
            pl.BlockSpec((None, None, FOX_TQ, LANES), lambda bi, h, i: (bi, h, i, 0)),
            pl.BlockSpec((None, s, HEAD_DIM), lambda bi, h, i: (bi, 0, FOX_HEADS + h)),
            pl.BlockSpec((None, None, s, LANES), lambda bi, h, i: (bi, h, 0, 0)),
            pl.BlockSpec((None, s, HEAD_DIM), lambda bi, h, i: (bi, 0, 2 * FOX_HEADS + h)),
        ],
        out_specs=pl.BlockSpec((None, FOX_TQ, HEAD_DIM), lambda bi, h, i: (bi, i, h)),
        out_shape=jax.ShapeDtypeStruct((b, s, FOX_WIDTH), jnp.bfloat16),
        compiler_params=_params("arbitrary", "arbitrary", "arbitrary"),
        name="fox_attention",
    )(proj, qe, proj, ke, proj)


def _mem_attention(q_ref, k_ref, v_ref):
    inv_sqrt = 1.0 / math.sqrt(HEAD_DIM)
    heads = []
    for h in range(MEM_HEADS):
        c0 = h * HEAD_DIM
        c1 = c0 + HEAD_DIM
        logits = lax.dot_general(q_ref[:, c0:c1], k_ref[:, c0:c1], (((1,), (1,)), ((), ())),
                                 preferred_element_type=jnp.float32)
        logits = logits * inv_sqrt
        m = jnp.max(logits, axis=-1, keepdims=True)
        e = jnp.exp(logits - m)
        p = e / jnp.sum(e, axis=-1, keepdims=True)
        out = jnp.dot(p.astype(jnp.bfloat16), v_ref[:, c0:c1], preferred_element_type=jnp.float32)
        heads.append(out.astype(jnp.bfloat16))
    return jnp.concatenate(heads, axis=1)


def _out_proj_kernel(x_ref, ya_ref, q_ref, k_ref, v_ref, wa_ref, wb_ref, g_ref, o_ref, hn_ref):
    acc = jnp.dot(ya_ref[...], wa_ref[...], preferred_element_type=jnp.float32)
    y_mem = _mem_attention(q_ref, k_ref, v_ref)
    acc = acc + jnp.dot(y_mem, wb_ref[...], preferred_element_type=jnp.float32)
    x_new = x_ref[...] + acc
    o_ref[...] = x_new
    hn_ref[...] = _rms_normalize(x_new, g_ref[...])


def _out_proj(x2d, y_mix, q_arr, q_col_block, kv, w_stack, layer, ffn_gain):
    rows = x2d.shape[0]
    tiles_per_batch = rows // kv.shape[0] // ROW_TILE
    mem_row_block = POOL_WIDTH // MEM_WIDTH
    return pl.pallas_call(
        _out_proj_kernel,
        grid=(rows // ROW_TILE,),
        in_specs=[
            pl.BlockSpec((ROW_TILE, D_MODEL), lambda i: (i, 0)),
            pl.BlockSpec((ROW_TILE, POOL_WIDTH), lambda i: (i, 0)),
            pl.BlockSpec((ROW_TILE, MEM_WIDTH), lambda i: (i, q_col_block)),
            pl.BlockSpec((None, MEM_LEN, MEM_WIDTH), lambda i: (i // tiles_per_batch, 0, 0)),
            pl.BlockSpec((None, MEM_LEN, MEM_WIDTH), lambda i: (i // tiles_per_batch, 0, 1)),
            pl.BlockSpec((None, POOL_WIDTH, D_MODEL), lambda i: (layer, 0, 0), pipeline_mode=_RESIDENT),
            pl.BlockSpec((None, MEM_WIDTH, D_MODEL), lambda i: (layer, mem_row_block, 0),
                         pipeline_mode=_RESIDENT),
            pl.BlockSpec((1, D_MODEL), lambda i: (0, 0)),
        ],
        out_specs=[pl.BlockSpec((ROW_TILE, D_MODEL), lambda i: (i, 0)),
                   pl.BlockSpec((ROW_TILE, D_MODEL), lambda i: (i, 0))],
        out_shape=[jax.ShapeDtypeStruct((rows, D_MODEL), jnp.float32),
                   jax.ShapeDtypeStruct((rows, D_MODEL), jnp.bfloat16)],
        compiler_params=_params("arbitrary"),
        name="out_proj",
    )(x2d, y_mix, q_arr, kv, kv, w_stack, w_stack, ffn_gain)


def _ffn_kernel(x_ref, hn_ref, w1_ref, w2_ref, g_ref, o_ref, *, final_norm):
    @pl.when(pl.program_id(1) == 0)
    def _():
        o_ref[...] = x_ref[...]

    a = jnp.dot(hn_ref[...], w1_ref[...], preferred_element_type=jnp.float32)
    a = jnp.square(jnp.maximum(a, 0.0)).astype(jnp.bfloat16)
    o_ref[...] += jnp.dot(a, w2_ref[...], preferred_element_type=jnp.float32)

    if final_norm:
        @pl.when(pl.program_id(1) == pl.num_programs(1) - 1)
        def _():
            o_ref[...] = _rms_scale(o_ref[...], g_ref[...])


def _ffn(x2d, hn2d, w1_stack, w2_stack, layer, final_gain, final_norm):
    rows = x2d.shape[0]
    return pl.pallas_call(
        functools.partial(_ffn_kernel, final_norm=final_norm),
        grid=(rows // ROW_TILE, FFN_HIDDEN // FFN_TILE),
        in_specs=[
            pl.BlockSpec((ROW_TILE, D_MODEL), lambda i, j: (i, 0)),
            pl.BlockSpec((ROW_TILE, D_MODEL), lambda i, j: (i, 0)),
            pl.BlockSpec((None, D_MODEL, FFN_TILE), lambda i, j: (layer, 0, j)),
            pl.BlockSpec((None, FFN_TILE, D_MODEL), lambda i, j: (layer, j, 0)),
            pl.BlockSpec((1, D_MODEL), lambda i, j: (0, 0)),
        ],
        out_specs=pl.BlockSpec((ROW_TILE, D_MODEL), lambda i, j: (i, 0)),
        out_shape=jax.ShapeDtypeStruct((rows, D_MODEL), jnp.float32),
        compiler_params=_params("arbitrary", "arbitrary"),
        name="ffn_final" if final_norm else "ffn",
    )(x2d, hn2d, w1_stack, w2_stack, final_gain)


def kernel(x, mem, norm_mix, norm_mem, pool_w_in, pool_w_grp, pool_scale, fox_w_in, fox_b_f,
           w_mem_kv, w_out, norm_ffn, w_ffn1, w_ffn2, norm_final):
    b, s, d = x.shape
    rows = b * s
    depth = w_out.shape[0]
    bf16 = jnp.bfloat16
    f32 = jnp.float32
    x2d = x.reshape(rows, d)
    mem2d = mem.reshape(b * MEM_LEN, d)
    w_mem_kv_b, w_out_b = w_mem_kv.astype(bf16), w_out.astype(bf16)
    w_ffn1_b, w_ffn2_b = w_ffn1.astype(bf16), w_ffn2.astype(bf16)

    def row(v):
        return v.reshape(1, -1).astype(f32)

    def mix_tail(x2d, layer, y_mix, q_arr, q_col_block):
        kv = _mem_kv_proj(mem2d, row(norm_mem[layer]), w_mem_kv_b, layer).reshape(b, MEM_LEN, 2 * MEM_WIDTH)
        x2d, hn2d = _out_proj(x2d, y_mix.reshape(rows, POOL_WIDTH), q_arr.reshape(rows, -1), q_col_block,
                              kv, w_out_b, layer, row(norm_ffn[layer]))
        return _ffn(x2d, hn2d, w_ffn1_b, w_ffn2_b, layer, row(norm_final), layer == depth - 1)

    u, q_mem = _pool_in_proj(x2d, row(norm_mix[0]), pool_w_in.astype(bf16))
    y_mix = _pool_mix(u.reshape(b, s, POOL_WIDTH), pool_w_grp.astype(bf16), row(pool_scale[0]))
    x2d = mix_tail(x2d, 0, y_mix, q_mem.reshape(b, s, MEM_WIDTH), 0)

    w_in = fox_w_in[0]
    n_qkv = 3 * FOX_WIDTH
    w_qkv = w_in[:, :n_qkv].astype(bf16)
    w_qm = w_in[:, n_qkv + FOX_HEADS:].astype(bf16)
    w_f = jnp.pad(w_in[:, n_qkv:n_qkv + FOX_HEADS], ((0, 0), (0, LANES - FOX_HEADS))).astype(bf16)
    bias_f = jnp.pad(fox_b_f[0].astype(f32), (0, LANES - FOX_HEADS)).reshape(1, LANES)
    proj, qe, ke = _fox_in_proj(x2d, row(norm_mix[1]), w_qkv, w_qm, w_f, bias_f, b)
    proj = proj.reshape(b, s, FOX_PROJ_COLS)
    y_mix = _fox_attention(proj, qe, ke)
    x2d = mix_tail(x2d, 1, y_mix, proj, FOX_PROJ_COLS // MEM_WIDTH - 1)

    return x2d.reshape(b, s, d)
```

```python
import functools
import math

import jax
import jax.numpy as jnp
import numpy as np
from jax import lax
from jax.experimental import pallas as pl
from jax.experimental.pallas import tpu as pltpu

D_MODEL = 2048
HEAD_DIM = 128
MEM_LEN = 256
MEM_HEADS = 4
MEM_WIDTH = MEM_HEADS * HEAD_DIM
POOL_WIDTH = D_MODEL - MEM_WIDTH
POOL_GROUPS = 4
POOL_GROUP_WIDTH = POOL_WIDTH // POOL_GROUPS
POOL_WINDOWS = (2, 4, 8, 16)
POOL_HALO = 32
FOX_HEADS = POOL_WIDTH // HEAD_DIM
FOX_WIDTH = FOX_HEADS * HEAD_DIM
FOX_PROJ_COLS = 3 * FOX_WIDTH + MEM_WIDTH
FFN_HIDDEN = 4 * D_MODEL
RMS_EPS = 1e-6

LANES = 128
VMEM_LIMIT = 56 * 1024 * 1024

ROW_TILE = 512
FFN_TILE = 1024
FOX_TQ = 1024
FOX_TK = 512
FOX_UNROLLS = (10,)
CUMSUM_TILE = 256

LOG2E = math.log2(math.e)
FOX_QSCALE = LOG2E / math.sqrt(HEAD_DIM)
NEG_BIG = -1e30

_RESIDENT = pl.Buffered(1)


def _params(*sem):
    return pltpu.CompilerParams(dimension_semantics=sem, vmem_limit_bytes=VMEM_LIMIT)


def _rms_scale(x, gain):
    ms = jnp.mean(x * x, axis=-1, keepdims=True)
    return (x * lax.rsqrt(ms + RMS_EPS)) * gain


def _rms_normalize(x, gain):
    return _rms_scale(x, gain).astype(jnp.bfloat16)


def _pool_in_kernel(x_ref, g_ref, w_ref, u_ref, qm_ref):
    hn = _rms_normalize(x_ref[...], g_ref[...])
    res = jnp.dot(hn, w_ref[...], preferred_element_type=jnp.float32)
    u_ref[...] = res[:, :POOL_WIDTH]
    qm_ref[...] = res[:, POOL_WIDTH:].astype(jnp.bfloat16)


def _pool_in_proj(x2d, gain, w_stack):
    rows = x2d.shape[0]
    return pl.pallas_call(
        _pool_in_kernel,
        grid=(rows // ROW_TILE,),
        in_specs=[
            pl.BlockSpec((ROW_TILE, D_MODEL), lambda i: (i, 0)),
            pl.BlockSpec((1, D_MODEL), lambda i: (0, 0)),
            pl.BlockSpec((None, D_MODEL, D_MODEL), lambda i: (0, 0, 0), pipeline_mode=_RESIDENT),
        ],
        out_specs=[
            pl.BlockSpec((ROW_TILE, POOL_WIDTH), lambda i: (i, 0)),
            pl.BlockSpec((ROW_TILE, MEM_WIDTH), lambda i: (i, 0)),
        ],
        out_shape=[
            jax.ShapeDtypeStruct((rows, POOL_WIDTH), jnp.float32),
            jax.ShapeDtypeStruct((rows, MEM_WIDTH), jnp.bfloat16),
        ],
        compiler_params=_params("arbitrary"),
        name="pool_in_proj",
    )(x2d, gain, w_stack)


def _mem_kv_kernel(x_ref, g_ref, w_ref, o_ref):
    hn = _rms_normalize(x_ref[...], g_ref[...])
    o_ref[...] = jnp.dot(hn, w_ref[...], preferred_element_type=jnp.float32).astype(jnp.bfloat16)


def _mem_kv_proj(mem2d, gain, w_stack, layer):
    rows = mem2d.shape[0]
    n = w_stack.shape[2]
    return pl.pallas_call(
        _mem_kv_kernel,
        grid=(1,),
        in_specs=[
            pl.BlockSpec((rows, D_MODEL), lambda i: (0, 0)),
            pl.BlockSpec((1, D_MODEL), lambda i: (0, 0)),
            pl.BlockSpec((None, D_MODEL, n), lambda i: (layer, 0, 0)),
        ],
        out_specs=pl.BlockSpec((rows, n), lambda i: (0, 0)),
        out_shape=jax.ShapeDtypeStruct((rows, n), jnp.bfloat16),
        compiler_params=_params("arbitrary"),
        name="mem_kv_proj",
    )(mem2d, gain, w_stack)


PIECE_LANES = 16


def _piece_selector():
    sel = np.zeros((LANES, FOX_HEADS * LANES), np.float32)
    for h in range(FOX_HEADS):
        for piece in range(3):
            sel[piece * PIECE_LANES + h, h * LANES + piece] = 1.0
            sel[piece * PIECE_LANES + h, h * LANES + 3 + piece] = -1.0
    return jnp.asarray(sel, jnp.bfloat16)


def _bf16_pieces(x):
    hi = x.astype(jnp.bfloat16).astype(jnp.float32)
    r1 = x - hi
    mid = r1.astype(jnp.bfloat16).astype(jnp.float32)
    lo = (r1 - mid).astype(jnp.bfloat16).astype(jnp.float32)
    return hi, mid, lo


def _fox_in_kernel(x_ref, g_ref, wqkv_ref, wqm_ref, wf_ref, b_ref, sel_ref, o_ref, qe_ref, ke_ref,
                   z_ref, carry_ref, *, tiles_per_batch):
    i = pl.program_id(0)

    @pl.when(i == 0)
    def _():
        z_ref[...] = jnp.zeros_like(z_ref)

    @pl.when((i == 0) | ((i - 1) % tiles_per_batch == 0))
    def _():
        carry_ref[...] = jnp.zeros_like(carry_ref)

    z = z_ref[...]
    log_f = jnp.minimum(z, 0.0) - jnp.log1p(jnp.exp(-jnp.abs(z)))
    lf_pieces = jnp.concatenate(_bf16_pieces(log_f), axis=1).astype(jnp.bfloat16)
    tc = CUMSUM_TILE
    row = lax.broadcasted_iota(jnp.int32, (tc, tc), 0)
    col = lax.broadcasted_iota(jnp.int32, (tc, tc), 1)
    tri = (col <= row).astype(jnp.bfloat16)
    lane = lax.broadcasted_iota(jnp.int32, (tc, LANES), 1)
    k_side = (lane >= 3) & (lane < 6)
    q_ones = jnp.where(k_side, 1.0, 0.0)
    k_ones = jnp.where(lane < 3, 1.0, 0.0)
    mid_lanes = (lane >= PIECE_LANES) & (lane < PIECE_LANES + FOX_HEADS)
    lo_lanes = (lane >= 2 * PIECE_LANES) & (lane < 2 * PIECE_LANES + FOX_HEADS)
    total = carry_ref[...]
    for r0 in range(0, ROW_TILE, tc):
        part = jnp.dot(tri, lf_pieces[r0:r0 + tc], preferred_element_type=jnp.float32)
        csum = (part[:, :LANES] + part[:, LANES:2 * LANES]) + part[:, 2 * LANES:] + total
        total = csum[tc - 1:tc, :]
        hi, mid, lo = _bf16_pieces(csum * LOG2E)
        packed = jnp.where(lane < FOX_HEADS, hi,
                           jnp.where(mid_lanes, pltpu.roll(mid, PIECE_LANES, 1),
                                     jnp.where(lo_lanes, pltpu.roll(lo, 2 * PIECE_LANES, 1), 0.0)))
        placed = jnp.dot(packed.astype(jnp.bfloat16), sel_ref[...],
                         preferred_element_type=jnp.float32)
        for h in range(FOX_HEADS):
            blk = placed[:, h * LANES:(h + 1) * LANES]
            qe_ref[h, r0:r0 + tc, :] = jnp.where(lane < 3, blk, q_ones).astype(jnp.bfloat16)
            ke_ref[h, r0:r0 + tc, :] = jnp.where(k_side, blk, k_ones).astype(jnp.bfloat16)
    carry_ref[...] = total

    hn = _rms_normalize(x_ref[...], g_ref[...])
    z_ref[...] = jnp.dot(hn, wf_ref[...], preferred_element_type=jnp.float32) + b_ref[...]
    for part, scale in enumerate((FOX_QSCALE, None, None)):
        cols = slice(part * FOX_WIDTH, (part + 1) * FOX_WIDTH)
        res = jnp.dot(hn, wqkv_ref[:, cols], preferred_element_type=jnp.float32)
        if scale is not None:
            res = res * scale
        o_ref[:, cols] = res.astype(jnp.bfloat16)
    o_ref[:, 3 * FOX_WIDTH:] = jnp.dot(hn, wqm_ref[...], preferred_element_type=jnp.float32).astype(jnp.bfloat16)


def _fox_in_proj(x2d, gain, w_qkv, w_qm, w_f, bias, batch):
    rows = x2d.shape[0]
    seq = rows // batch
    tiles_per_batch = seq // ROW_TILE
    last = rows // ROW_TILE - 1
    ext = jax.ShapeDtypeStruct((batch, FOX_HEADS, seq, LANES), jnp.bfloat16)

    def gate_tile(i):
        t = jnp.maximum(i - 1, 0)
        return (t // tiles_per_batch, 0, t % tiles_per_batch, 0)

    ext_spec = pl.BlockSpec((None, FOX_HEADS, ROW_TILE, LANES), gate_tile)
    return pl.pallas_call(
        functools.partial(_fox_in_kernel, tiles_per_batch=tiles_per_batch),
        grid=(last + 2,),
        in_specs=[
            pl.BlockSpec((ROW_TILE, D_MODEL), lambda i: (jnp.minimum(i, last), 0)),
            pl.BlockSpec((1, D_MODEL), lambda i: (0, 0)),
            pl.BlockSpec((D_MODEL, 3 * FOX_WIDTH), lambda i: (0, 0), pipeline_mode=_RESIDENT),
            pl.BlockSpec((D_MODEL, MEM_WIDTH), lambda i: (0, 0), pipeline_mode=_RESIDENT),
            pl.BlockSpec((D_MODEL, LANES), lambda i: (0, 0), pipeline_mode=_RESIDENT),
            pl.BlockSpec((1, LANES), lambda i: (0, 0)),
            pl.BlockSpec((LANES, FOX_HEADS * LANES), lambda i: (0, 0), pipeline_mode=_RESIDENT),
        ],
        out_specs=[pl.BlockSpec((ROW_TILE, FOX_PROJ_COLS), lambda i: (jnp.minimum(i, last), 0)),
                   ext_spec, ext_spec],
        out_shape=[jax.ShapeDtypeStruct((rows, FOX_PROJ_COLS), jnp.bfloat16), ext, ext],
        scratch_shapes=[pltpu.VMEM((ROW_TILE, LANES), jnp.float32), pltpu.VMEM((1, LANES), jnp.float32)],
        compiler_params=_params("arbitrary"),
        name="fox_in_proj",
    )(x2d, gain, w_qkv, w_qm, w_f, bias, _piece_selector())


def _pool_mix_kernel(u_ref, w_ref, s_ref, y_ref, ext_ref, lvl_ref):
    i = pl.program_id(1)
    tm = u_ref.shape[0]
    rows = tm + POOL_HALO

    @pl.when(i == 0)
    def _():
        ext_ref[0:POOL_HALO, :] = jnp.zeros((POOL_HALO, POOL_WIDTH), jnp.float32)

    @pl.when(i > 0)
    def _():
        ext_ref[0:POOL_HALO, :] = ext_ref[tm:tm + POOL_HALO, :]

    ext_ref[POOL_HALO:, :] = u_ref[...]

    t = i * tm + lax.broadcasted_iota(jnp.int32, (tm, 1), 0)
    for g, win in enumerate(POOL_WINDOWS):
        c0 = g * POOL_GROUP_WIDTH
        c1 = c0 + POOL_GROUP_WIDTH
        tok = ext_ref[POOL_HALO:, c0:c1]
        levels = win.bit_length() - 1
        wsum = None
        for lvl in range(levels):
            shift = 1 << lvl
            start = POOL_HALO - 8 * (levels - 1 - lvl)
            if lvl == 0:
                cur = ext_ref[start:, c0:c1] + ext_ref[start - shift:rows - shift, c0:c1]
            else:
                src = lvl_ref.at[(lvl - 1) % 2]
                cur = src[start:, :] + src[start - shift:rows - shift, :]
            if lvl + 1 < levels:
                lvl_ref[lvl % 2, start:, :] = cur
            else:
                wsum = cur
        count = jnp.minimum(t + 1, win).astype(jnp.float32)
        pooled = wsum / count - tok
        mixed = jnp.dot(pooled.astype(jnp.bfloat16), w_ref[g], preferred_element_type=jnp.float32)
        y_ref[:, c0:c1] = (mixed * s_ref[:, c0:c1]).astype(jnp.bfloat16)


def _pool_mix(u, w_grp_stack, scale):
    b, s, _ = u.shape
    assert all(w & (w - 1) == 0 for w in POOL_WINDOWS)
    assert POOL_HALO == 8 * (max(POOL_WINDOWS).bit_length() - 1)
    return pl.pallas_call(
        _pool_mix_kernel,
        grid=(b, s // ROW_TILE),
        in_specs=[
            pl.BlockSpec((None, ROW_TILE, POOL_WIDTH), lambda bi, i: (bi, i, 0)),
            pl.BlockSpec((None, POOL_GROUPS, POOL_GROUP_WIDTH, POOL_GROUP_WIDTH), lambda bi, i: (0, 0, 0, 0)),
            pl.BlockSpec((1, POOL_WIDTH), lambda bi, i: (0, 0)),
        ],
        out_specs=pl.BlockSpec((None, ROW_TILE, POOL_WIDTH), lambda bi, i: (bi, i, 0)),
        out_shape=jax.ShapeDtypeStruct((b, s, POOL_WIDTH), jnp.bfloat16),
        scratch_shapes=[pltpu.VMEM((ROW_TILE + POOL_HALO, POOL_WIDTH), jnp.float32),
                        pltpu.VMEM((2, ROW_TILE + POOL_HALO, POOL_GROUP_WIDTH), jnp.float32)],
        compiler_params=_params("arbitrary", "arbitrary"),
        name="pool_mix",
    )(u, w_grp_stack, scale)


def _fox_attn_kernel(q_ref, qe_ref, k_ref, ke_ref, v_ref, o_ref):
    i = pl.program_id(2)
    tq, tk = FOX_TQ, FOX_TK
    per_tile = tq // tk
    q2 = jnp.concatenate([q_ref[...], qe_ref[...]], axis=1)
    lane = lax.broadcasted_iota(jnp.int32, (tk, LANES), 1)
    v_ext = jnp.where(lane == 0, 1.0, 0.0).astype(jnp.bfloat16)

    def block(j, carry, row0=0, masked=False):
        m, acc = carry
        ks = pl.ds(pl.multiple_of(j * tk, tk), tk)
        k2 = jnp.concatenate([k_ref[ks, :], ke_ref[ks, :]], axis=1)
        v2 = jnp.concatenate([v_ref[ks, :], v_ext], axis=1)
        s = lax.dot_general(q2[row0:], k2, (((1,), (1,)), ((), ())), preferred_element_type=jnp.float32)
        if masked:
            row = lax.broadcasted_iota(jnp.int32, s.shape, 0)
            col = lax.broadcasted_iota(jnp.int32, s.shape, 1)
            s = jnp.where(col <= row, s, -jnp.inf)
        m_old = m[row0:]
        m_new = jnp.maximum(m_old, jnp.max(s, axis=-1, keepdims=True))
        alpha = jnp.exp2(m_old - m_new)
        p = jnp.exp2(s - m_new).astype(jnp.bfloat16)
        acc_new = alpha * acc[row0:] + jnp.dot(p, v2, preferred_element_type=jnp.float32)
        if row0:
            m_new = jnp.concatenate([m[:row0], m_new], axis=0)
            acc_new = jnp.concatenate([acc[:row0], acc_new], axis=0)
        return m_new, acc_new

    carry = (jnp.full((tq, 1), NEG_BIG, jnp.float32),
             jnp.zeros((tq, 2 * HEAD_DIM), jnp.float32))
    n_full = i * per_tile

    done = 0
    for n in FOX_UNROLLS:
        def body(t, c, n=n, base=done):
            for u in range(n):
                c = block(base + t * n + u, c)
            return c
        trips = (n_full - done) // n
        carry = lax.fori_loop(0, trips, body, carry)
        done = done + trips * n

    def tail(c, n_rest):
        for u in range(n_rest):
            c = block(done + u, c)
        for d in range(per_tile):
            c = block(i * per_tile + d, c, row0=d * tk, masked=True)
        _, acc = c
        o_ref[...] = (acc[:, :HEAD_DIM] / acc[:, HEAD_DIM:HEAD_DIM + 1]).astype(jnp.bfloat16)

    rest = n_full - done
    for n_rest in range(0, FOX_UNROLLS[-1], per_tile):
        pl.when(rest == n_rest)(functools.partial(tail, carry, n_rest))


def _fox_attention(proj, qe, ke):
    b, s, _ = proj.shape
    assert FOX_TQ % FOX_TK == 0 and all(n % (FOX_TQ // FOX_TK) == 0 for n in FOX_UNROLLS)
    return pl.pallas_call(
        _fox_attn_kernel,
        grid=(b, FOX_HEADS, s // FOX_TQ),
        in_specs=[
            pl.BlockSpec((None, FOX_TQ, HEAD_DIM), lambda bi, h, i: (bi, i, h)),
            pl.BlockSpec((None, None, FOX_TQ, LANES), lambda bi, h, i: (bi, h, i, 0)),
            pl.BlockSpec((None, s, HEAD_DIM), lambda bi, h, i: (bi, 0, FOX_HEADS + h)),
            pl.BlockSpec((None, None, s, LANES), lambda bi, h, i: (bi, h, 0, 0)),
            pl.BlockSpec((None, s, HEAD_DIM), lambda bi, h, i: (bi, 0, 2 * FOX_HEADS + h)),
        ],
        out_specs=pl.BlockSpec((None, FOX_TQ, HEAD_DIM), lambda bi, h, i: (bi, i, h)),
        out_shape=jax.ShapeDtypeStruct((b, s, FOX_WIDTH), jnp.bfloat16),
        compiler_params=_params("arbitrary", "arbitrary", "arbitrary"),
        name="fox_attention",
    )(proj, qe, proj, ke, proj)


def _mem_attention(q_ref, k_ref, v_ref):
    inv_sqrt = 1.0 / math.sqrt(HEAD_DIM)
    heads = []
    for h in range(MEM_HEADS):
        c0 = h * HEAD_DIM
        c1 = c0 + HEAD_DIM
        logits = lax.dot_general(q_ref[:, c0:c1], k_ref[:, c0:c1], (((1,), (1,)), ((), ())),
                                 preferred_element_type=jnp.float32)
        logits = logits * inv_sqrt
        m = jnp.max(logits, axis=-1, keepdims=True)
        e = jnp.exp(logits - m)
        p = e / jnp.sum(e, axis=-1, keepdims=True)
        out = jnp.dot(p.astype(jnp.bfloat16), v_ref[:, c0:c1], preferred_element_type=jnp.float32)
        heads.append(out.astype(jnp.bfloat16))
    return jnp.concatenate(heads, axis=1)


def _out_proj_kernel(x_ref, ya_ref, q_ref, k_ref, v_ref, wa_ref, wb_ref, g_ref, o_ref, hn_ref):
    acc = jnp.dot(ya_ref[...], wa_ref[...], preferred_element_type=jnp.float32)
    y_mem = _mem_attention(q_ref, k_ref, v_ref)
    acc = acc + jnp.dot(y_mem, wb_ref[...], preferred_element_type=jnp.float32)
    x_new = x_ref[...] + acc
    o_ref[...] = x_new
    hn_ref[...] = _rms_normalize(x_new, g_ref[...])


def _out_proj(x2d, y_mix, q_arr, q_col_block, kv, w_stack, layer, ffn_gain):
    rows = x2d.shape[0]
    tiles_per_batch = rows // kv.shape[0] // ROW_TILE
    mem_row_block = POOL_WIDTH // MEM_WIDTH
    return pl.pallas_call(
        _out_proj_kernel,
        grid=(rows // ROW_TILE,),
        in_specs=[
            pl.BlockSpec((ROW_TILE, D_MODEL), lambda i: (i, 0)),
            pl.BlockSpec((ROW_TILE, POOL_WIDTH), lambda i: (i, 0)),
            pl.BlockSpec((ROW_TILE, MEM_WIDTH), lambda i: (i, q_col_block)),
            pl.BlockSpec((None, MEM_LEN, MEM_WIDTH), lambda i: (i // tiles_per_batch, 0, 0)),
            pl.BlockSpec((None, MEM_LEN, MEM_WIDTH), lambda i: (i // tiles_per_batch, 0, 1)),
            pl.BlockSpec((None, POOL_WIDTH, D_MODEL), lambda i: (layer, 0, 0), pipeline_mode=_RESIDENT),
            pl.BlockSpec((None, MEM_WIDTH, D_MODEL), lambda i: (layer, mem_row_block, 0),
                         pipeline_mode=_RESIDENT),
            pl.BlockSpec((1, D_MODEL), lambda i: (0, 0)),
        ],
        out_specs=[pl.BlockSpec((ROW_TILE, D_MODEL), lambda i: (i, 0)),
                   pl.BlockSpec((ROW_TILE, D_MODEL), lambda i: (i, 0))],
        out_shape=[jax.ShapeDtypeStruct((rows, D_MODEL), jnp.float32),
                   jax.ShapeDtypeStruct((rows, D_MODEL), jnp.bfloat16)],
        compiler_params=_params("arbitrary"),
        name="out_proj",
    )(x2d, y_mix, q_arr, kv, kv, w_stack, w_stack, ffn_gain)


def _ffn_kernel(x_ref, hn_ref, w1_ref, w2_ref, g_ref, o_ref, *, final_norm):
    @pl.when(pl.program_id(1) == 0)
    def _():
        o_ref[...] = x_ref[...]

    a = jnp.dot(hn_ref[...], w1_ref[...], preferred_element_type=jnp.float32)
    a = jnp.square(jnp.maximum(a, 0.0)).astype(jnp.bfloat16)
    o_ref[...] += jnp.dot(a, w2_ref[...], preferred_element_type=jnp.float32)

    if final_norm:
        @pl.when(pl.program_id(1) == pl.num_programs(1) - 1)
        def _():
            o_ref[...] = _rms_scale(o_ref[...], g_ref[...])


def _ffn(x2d, hn2d, w1_stack, w2_stack, layer, final_gain, final_norm):
    rows = x2d.shape[0]
    return pl.pallas_call(
        functools.partial(_ffn_kernel, final_norm=final_norm),
        grid=(rows // ROW_TILE, FFN_HIDDEN // FFN_TILE),
        in_specs=[
            pl.BlockSpec((ROW_TILE, D_MODEL), lambda i, j: (i, 0)),
            pl.BlockSpec((ROW_TILE, D_MODEL), lambda i, j: (i, 0)),
            pl.BlockSpec((None, D_MODEL, FFN_TILE), lambda i, j: (layer, 0, j)),
            pl.BlockSpec((None, FFN_TILE, D_MODEL), lambda i, j: (layer, j, 0)),
            pl.BlockSpec((1, D_MODEL), lambda i, j: (0, 0)),
        ],
        out_specs=pl.BlockSpec((ROW_TILE, D_MODEL), lambda i, j: (i, 0)),
        out_shape=jax.ShapeDtypeStruct((rows, D_MODEL), jnp.float32),
        compiler_params=_params("arbitrary", "arbitrary"),
        name="ffn_final" if final_norm else "ffn",
    )(x2d, hn2d, w1_stack, w2_stack, final_gain)


def kernel(x, mem, norm_mix, norm_mem, pool_w_in, pool_w_grp, pool_scale, fox_w_in, fox_b_f,
           w_mem_kv, w_out, norm_ffn, w_ffn1, w_ffn2, norm_final):
    b, s, d = x.shape
    rows = b * s
    depth = w_out.shape[0]
    bf16 = jnp.bfloat16
    f32 = jnp.float32
    x2d = x.reshape(rows, d)
    mem2d = mem.reshape(b * MEM_LEN, d)
    w_mem_kv_b, w_out_b = w_mem_kv.astype(bf16), w_out.astype(bf16)
    w_ffn1_b, w_ffn2_b = w_ffn1.astype(bf16), w_ffn2.astype(bf16)

    def row(v):
        return v.reshape(1, -1).astype(f32)

    def mix_tail(x2d, layer, y_mix, q_arr, q_col_block):
        kv = _mem_kv_proj(mem2d, row(norm_mem[layer]), w_mem_kv_b, layer).reshape(b, MEM_LEN, 2 * MEM_WIDTH)
        x2d, hn2d = _out_proj(x2d, y_mix.reshape(rows, POOL_WIDTH), q_arr.reshape(rows, -1), q_col_block,
                              kv, w_out_b, layer, row(norm_ffn[layer]))
        return _ffn(x2d, hn2d, w_ffn1_b, w_ffn2_b, layer, row(norm_final), layer == depth - 1)

    u, q_mem = _pool_in_proj(x2d, row(norm_mix[0]), pool_w_in.astype(bf16))
    y_mix = _pool_mix(u.reshape(b, s, POOL_WIDTH), pool_w_grp.astype(bf16), row(pool_scale[0]))
    x2d = mix_tail(x2d, 0, y_mix, q_mem.reshape(b, s, MEM_WIDTH), 0)

    w_in = fox_w_in[0]
    n_qkv = 3 * FOX_WIDTH
    w_qkv = w_in[:, :n_qkv].astype(bf16)
    w_qm = w_in[:, n_qkv + FOX_HEADS:].astype(bf16)
    w_f = jnp.pad(w_in[:, n_qkv:n_qkv + FOX_HEADS], ((0, 0), (0, LANES - FOX_HEADS))).astype(bf16)
    bias_f = jnp.pad(fox_b_f[0].astype(f32), (0, LANES - FOX_HEADS)).reshape(1, LANES)
    proj, qe, ke = _fox_in_proj(x2d, row(norm_mix[1]), w_qkv, w_qm, w_f, bias_f, b)
    proj = proj.reshape(b, s, FOX_PROJ_COLS)
    y_mix = _fox_attention(proj, qe, ke)
    x2d = mix_tail(x2d, 1, y_mix, proj, FOX_PROJ_COLS // MEM_WIDTH - 1)

    return x2d.reshape(b, s, d)
```

```python
import functools
import math

import jax
import jax.numpy as jnp
import numpy as np
from jax import lax
from jax.experimental import pallas as pl
from jax.experimental.pallas import tpu as pltpu

D_MODEL = 2048
HEAD_DIM = 128
MEM_LEN = 256
MEM_HEADS = 4
MEM_WIDTH = MEM_HEADS * HEAD_DIM
POOL_WIDTH = D_MODEL - MEM_WIDTH
POOL_GROUPS = 4
POOL_GROUP_WIDTH = POOL_WIDTH // POOL_GROUPS
POOL_WINDOWS = (2, 4, 8, 16)
POOL_HALO = 32
FOX_HEADS = POOL_WIDTH // HEAD_DIM
FOX_WIDTH = FOX_HEADS * HEAD_DIM
FOX_PROJ_COLS = 3 * FOX_WIDTH + MEM_WIDTH
FFN_HIDDEN = 4 * D_MODEL
RMS_EPS = 1e-6

LANES = 128
VMEM_LIMIT = 56 * 1024 * 1024

ROW_TILE = 512
FFN_TILE = 1024
FOX_TQ = 1024
FOX_TK = 512
FOX_UNROLLS = (12,)
CUMSUM_TILE = 256

LOG2E = math.log2(math.e)
FOX_QSCALE = LOG2E / math.sqrt(HEAD_DIM)
NEG_BIG = -1e30

_RESIDENT = pl.Buffered(1)


def _params(*sem):
    return pltpu.CompilerParams(dimension_semantics=sem, vmem_limit_bytes=VMEM_LIMIT)


def _rms_scale(x, gain):
    ms = jnp.mean(x * x, axis=-1, keepdims=True)
    return (x * lax.rsqrt(ms + RMS_EPS)) * gain


def _rms_normalize(x, gain):
    return _rms_scale(x, gain).astype(jnp.bfloat16)


def _pool_in_kernel(x_ref, g_ref, w_ref, u_ref, qm_ref):
    hn = _rms_normalize(x_ref[...], g_ref[...])
    res = jnp.dot(hn, w_ref[...], preferred_element_type=jnp.float32)
    u_ref[...] = res[:, :POOL_WIDTH]
    qm_ref[...] = res[:, POOL_WIDTH:].astype(jnp.bfloat16)


def _pool_in_proj(x2d, gain, w_stack):
    rows = x2d.shape[0]
    return pl.pallas_call(
        _pool_in_kernel,
        grid=(rows // ROW_TILE,),
        in_specs=[
            pl.BlockSpec((ROW_TILE, D_MODEL), lambda i: (i, 0)),
            pl.BlockSpec((1, D_MODEL), lambda i: (0, 0)),
            pl.BlockSpec((None, D_MODEL, D_MODEL), lambda i: (0, 0, 0), pipeline_mode=_RESIDENT),
        ],
        out_specs=[
            pl.BlockSpec((ROW_TILE, POOL_WIDTH), lambda i: (i, 0)),
            pl.BlockSpec((ROW_TILE, MEM_WIDTH), lambda i: (i, 0)),
        ],
        out_shape=[
            jax.ShapeDtypeStruct((rows, POOL_WIDTH), jnp.float32),
            jax.ShapeDtypeStruct((rows, MEM_WIDTH), jnp.bfloat16),
        ],
        compiler_params=_params("arbitrary"),
        name="pool_in_proj",
    )(x2d, gain, w_stack)


def _mem_kv_kernel(x_ref, g_ref, w_ref, o_ref):
    hn = _rms_normalize(x_ref[...], g_ref[...])
    o_ref[...] = jnp.dot(hn, w_ref[...], preferred_element_type=jnp.float32).astype(jnp.bfloat16)


def _mem_kv_proj(mem2d, gain, w_stack, layer):
    rows = mem2d.shape[0]
    n = w_stack.shape[2]
    return pl.pallas_call(
        _mem_kv_kernel,
        grid=(1,),
        in_specs=[
            pl.BlockSpec((rows, D_MODEL), lambda i: (0, 0)),
            pl.BlockSpec((1, D_MODEL), lambda i: (0, 0)),
            pl.BlockSpec((None, D_MODEL, n), lambda i: (layer, 0, 0)),
        ],
        out_specs=pl.BlockSpec((rows, n), lambda i: (0, 0)),
        out_shape=jax.ShapeDtypeStruct((rows, n), jnp.bfloat16),
        compiler_params=_params("arbitrary"),
        name="mem_kv_proj",
    )(mem2d, gain, w_stack)


PIECE_LANES = 16


def _piece_selector():
    sel = np.zeros((LANES, FOX_HEADS * LANES), np.float32)
    for h in range(FOX_HEADS):
        for piece in range(3):
            sel[piece * PIECE_LANES + h, h * LANES + piece] = 1.0
            sel[piece * PIECE_LANES + h, h * LANES + 3 + piece] = -1.0
    return jnp.asarray(sel, jnp.bfloat16)


def _bf16_pieces(x):
    hi = x.astype(jnp.bfloat16).astype(jnp.float32)
    r1 = x - hi
    mid = r1.astype(jnp.bfloat16).astype(jnp.float32)
    lo = (r1 - mid).astype(jnp.bfloat16).astype(jnp.float32)
    return hi, mid, lo


def _fox_in_kernel(x_ref, g_ref, wqkv_ref, wqm_ref, wf_ref, b_ref, sel_ref, o_ref, qe_ref, ke_ref,
                   z_ref, carry_ref, *, tiles_per_batch):
    i = pl.program_id(0)

    @pl.when(i == 0)
    def _():
        z_ref[...] = jnp.zeros_like(z_ref)

    @pl.when((i == 0) | ((i - 1) % tiles_per_batch == 0))
    def _():
        carry_ref[...] = jnp.zeros_like(carry_ref)

    z = z_ref[...]
    log_f = jnp.minimum(z, 0.0) - jnp.log1p(jnp.exp(-jnp.abs(z)))
    lf_pieces = jnp.concatenate(_bf16_pieces(log_f), axis=1).astype(jnp.bfloat16)
    tc = CUMSUM_TILE
    row = lax.broadcasted_iota(jnp.int32, (tc, tc), 0)
    col = lax.broadcasted_iota(jnp.int32, (tc, tc), 1)
    tri = (col <= row).astype(jnp.bfloat16)
    lane = lax.broadcasted_iota(jnp.int32, (tc, LANES), 1)
    k_side = (lane >= 3) & (lane < 6)
    q_ones = jnp.where(k_side, 1.0, 0.0)
    k_ones = jnp.where(lane < 3, 1.0, 0.0)
    mid_lanes = (lane >= PIECE_LANES) & (lane < PIECE_LANES + FOX_HEADS)
    lo_lanes = (lane >= 2 * PIECE_LANES) & (lane < 2 * PIECE_LANES + FOX_HEADS)
    total = carry_ref[...]
    for r0 in range(0, ROW_TILE, tc):
        part = jnp.dot(tri, lf_pieces[r0:r0 + tc], preferred_element_type=jnp.float32)
        csum = (part[:, :LANES] + part[:, LANES:2 * LANES]) + part[:, 2 * LANES:] + total
        total = csum[tc - 1:tc, :]
        hi, mid, lo = _bf16_pieces(csum * LOG2E)
        packed = jnp.where(lane < FOX_HEADS, hi,
                           jnp.where(mid_lanes, pltpu.roll(mid, PIECE_LANES, 1),
                                     jnp.where(lo_lanes, pltpu.roll(lo, 2 * PIECE_LANES, 1), 0.0)))
        placed = jnp.dot(packed.astype(jnp.bfloat16), sel_ref[...],
                         preferred_element_type=jnp.float32)
        for h in range(FOX_HEADS):
            blk = placed[:, h * LANES:(h + 1) * LANES]
            qe_ref[h, r0:r0 + tc, :] = jnp.where(lane < 3, blk, q_ones).astype(jnp.bfloat16)
            ke_ref[h, r0:r0 + tc, :] = jnp.where(k_side, blk, k_ones).astype(jnp.bfloat16)
    carry_ref[...] = total

    hn = _rms_normalize(x_ref[...], g_ref[...])
    z_ref[...] = jnp.dot(hn, wf_ref[...], preferred_element_type=jnp.float32) + b_ref[...]
    for part, scale in enumerate((FOX_QSCALE, None, None)):
        cols = slice(part * FOX_WIDTH, (part + 1) * FOX_WIDTH)
        res = jnp.dot(hn, wqkv_ref[:, cols], preferred_element_type=jnp.float32)
        if scale is not None:
            res = res * scale
        o_ref[:, cols] = res.astype(jnp.bfloat16)
    o_ref[:, 3 * FOX_WIDTH:] = jnp.dot(hn, wqm_ref[...], preferred_element_type=jnp.float32).astype(jnp.bfloat16)


def _fox_in_proj(x2d, gain, w_qkv, w_qm, w_f, bias, batch):
    rows = x2d.shape[0]
    seq = rows // batch
    tiles_per_batch = seq // ROW_TILE
    last = rows // ROW_TILE - 1
    ext = jax.ShapeDtypeStruct((batch, FOX_HEADS, seq, LANES), jnp.bfloat16)

    def gate_tile(i):
        t = jnp.maximum(i - 1, 0)
        return (t // tiles_per_batch, 0, t % tiles_per_batch, 0)

    ext_spec = pl.BlockSpec((None, FOX_HEADS, ROW_TILE, LANES), gate_tile)
    return pl.pallas_call(
        functools.partial(_fox_in_kernel, tiles_per_batch=tiles_per_batch),
        grid=(last + 2,),
        in_specs=[
            pl.BlockSpec((ROW_TILE, D_MODEL), lambda i: (jnp.minimum(i, last), 0)),
            pl.BlockSpec((1, D_MODEL), lambda i: (0, 0)),
            pl.BlockSpec((D_MODEL, 3 * FOX_WIDTH), lambda i: (0, 0), pipeline_mode=_RESIDENT),
            pl.BlockSpec((D_MODEL, MEM_WIDTH), lambda i: (0, 0), pipeline_mode=_RESIDENT),
            pl.BlockSpec((D_MODEL, LANES), lambda i: (0, 0), pipeline_mode=_RESIDENT),
            pl.BlockSpec((1, LANES), lambda i: (0, 0)),
            pl.BlockSpec((LANES, FOX_HEADS * LANES), lambda i: (0, 0), pipeline_mode=_RESIDENT),
        ],
        out_specs=[pl.BlockSpec((ROW_TILE, FOX_PROJ_COLS), lambda i: (jnp.minimum(i, last), 0)),
                   ext_spec, ext_spec],
        out_shape=[jax.ShapeDtypeStruct((rows, FOX_PROJ_COLS), jnp.bfloat16), ext, ext],
        scratch_shapes=[pltpu.VMEM((ROW_TILE, LANES), jnp.float32), pltpu.VMEM((1, LANES), jnp.float32)],
        compiler_params=_params("arbitrary"),
        name="fox_in_proj",
    )(x2d, gain, w_qkv, w_qm, w_f, bias, _piece_selector())


def _pool_mix_kernel(u_ref, w_ref, s_ref, y_ref, ext_ref, lvl_ref):
    i = pl.program_id(1)
    tm = u_ref.shape[0]
    rows = tm + POOL_HALO

    @pl.when(i == 0)
    def _():
        ext_ref[0:POOL_HALO, :] = jnp.zeros((POOL_HALO, POOL_WIDTH), jnp.float32)

    @pl.when(i > 0)
    def _():
        ext_ref[0:POOL_HALO, :] = ext_ref[tm:tm + POOL_HALO, :]

    ext_ref[POOL_HALO:, :] = u_ref[...]

    t = i * tm + lax.broadcasted_iota(jnp.int32, (tm, 1), 0)
    for g, win in enumerate(POOL_WINDOWS):
        c0 = g * POOL_GROUP_WIDTH
        c1 = c0 + POOL_GROUP_WIDTH
        tok = ext_ref[POOL_HALO:, c0:c1]
        levels = win.bit_length() - 1
        wsum = None
        for lvl in range(levels):
            shift = 1 << lvl
            start = POOL_HALO - 8 * (levels - 1 - lvl)
            if lvl == 0:
                cur = ext_ref[start:, c0:c1] + ext_ref[start - shift:rows - shift, c0:c1]
            else:
                src = lvl_ref.at[(lvl - 1) % 2]
                cur = src[start:, :] + src[start - shift:rows - shift, :]
            if lvl + 1 < levels:
                lvl_ref[lvl % 2, start:, :] = cur
            else:
                wsum = cur
        count = jnp.minimum(t + 1, win).astype(jnp.float32)
        pooled = wsum / count - tok
        mixed = jnp.dot(pooled.astype(jnp.bfloat16), w_ref[g], preferred_element_type=jnp.float32)
        y_ref[:, c0:c1] = (mixed * s_ref[:, c0:c1]).astype(jnp.bfloat16)


def _pool_mix(u, w_grp_stack, scale):
    b, s, _ = u.shape
    assert all(w & (w - 1) == 0 for w in POOL_WINDOWS)
    assert POOL_HALO == 8 * (max(POOL_WINDOWS).bit_length() - 1)
    return pl.pallas_call(
        _pool_mix_kernel,
        grid=(b, s // ROW_TILE),
        in_specs=[
            pl.BlockSpec((None, ROW_TILE, POOL_WIDTH), lambda bi, i: (bi, i, 0)),
            pl.BlockSpec((None, POOL_GROUPS, POOL_GROUP_WIDTH, POOL_GROUP_WIDTH), lambda bi, i: (0, 0, 0, 0)),
            pl.BlockSpec((1, POOL_WIDTH), lambda bi, i: (0, 0)),
        ],
        out_specs=pl.BlockSpec((None, ROW_TILE, POOL_WIDTH), lambda bi, i: (bi, i, 0)),
        out_shape=jax.ShapeDtypeStruct((b, s, POOL_WIDTH), jnp.bfloat16),
        scratch_shapes=[pltpu.VMEM((ROW_TILE + POOL_HALO, POOL_WIDTH), jnp.float32),
                        pltpu.VMEM((2, ROW_TILE + POOL_HALO, POOL_GROUP_WIDTH), jnp.float32)],
        compiler_params=_params("arbitrary", "arbitrary"),
        name="pool_mix",
    )(u, w_grp_stack, scale)


def _fox_attn_kernel(q_ref, qe_ref, k_ref, ke_ref, v_ref, o_ref):
    i = pl.program_id(2)
    tq, tk = FOX_TQ, FOX_TK
    per_tile = tq // tk
    q2 = jnp.concatenate([q_ref[...], qe_ref[...]], axis=1)
    lane = lax.broadcasted_iota(jnp.int32, (tk, LANES), 1)
    v_ext = jnp.where(lane == 0, 1.0, 0.0).astype(jnp.bfloat16)

    def block(j, carry, row0=0, masked=False):
        m, acc = carry
        ks = pl.ds(pl.multiple_of(j * tk, tk), tk)
        k2 = jnp.concatenate([k_ref[ks, :], ke_ref[ks, :]], axis=1)
        v2 = jnp.concatenate([v_ref[ks, :], v_ext], axis=1)
        s = lax.dot_general(q2[row0:], k2, (((1,), (1,)), ((), ())), preferred_element_type=jnp.float32)
        if masked:
            row = lax.broadcasted_iota(jnp.int32, s.shape, 0)
            col = lax.broadcasted_iota(jnp.int32, s.shape, 1)
            s = jnp.where(col <= row, s, -jnp.inf)
        m_old = m[row0:]
        m_new = jnp.maximum(m_old, jnp.max(s, axis=-1, keepdims=True))
        alpha = jnp.exp2(m_old - m_new)
        p = jnp.exp2(s - m_new).astype(jnp.bfloat16)
        acc_new = alpha * acc[row0:] + jnp.dot(p, v2, preferred_element_type=jnp.float32)
        if row0:
            m_new = jnp.concatenate([m[:row0], m_new], axis=0)
            acc_new = jnp.concatenate([acc[:row0], acc_new], axis=0)
        return m_new, acc_new

    carry = (jnp.full((tq, 1), NEG_BIG, jnp.float32),
             jnp.zeros((tq, 2 * HEAD_DIM), jnp.float32))
    n_full = i * per_tile

    done = 0
    for n in FOX_UNROLLS:
        def body(t, c, n=n, base=done):
            for u in range(n):
                c = block(base + t * n + u, c)
            return c
        trips = (n_full - done) // n
        carry = lax.fori_loop(0, trips, body, carry)
        done = done + trips * n

    def tail(c, n_rest):
        for u in range(n_rest):
            c = block(done + u, c)
        for d in range(per_tile):
            c = block(i * per_tile + d, c, row0=d * tk, masked=True)
        _, acc = c
        o_ref[...] = (acc[:, :HEAD_DIM] / acc[:, HEAD_DIM:HEAD_DIM + 1]).astype(jnp.bfloat16)

    rest = n_full - done
    for n_rest in range(0, FOX_UNROLLS[-1], per_tile):
        pl.when(rest == n_rest)(functools.partial(tail, carry, n_rest))


def _fox_attention(proj, qe, ke):
    b, s, _ = proj.shape
    assert FOX_TQ % FOX_TK == 0 and all(n % (FOX_TQ // FOX_TK) == 0 for n in FOX_UNROLLS)
    return pl.pallas_call(
        _fox_attn_kernel,
        grid=(b, FOX_HEADS, s // FOX_TQ),
        in_specs=[
            pl.BlockSpec((None, FOX_TQ, HEAD_DIM), lambda bi, h, i: (bi, i, h)),
            pl.BlockSpec((None, None, FOX_TQ, LANES), lambda bi, h, i: (bi, h, i, 0)),
            pl.BlockSpec((None, s, HEAD_DIM), lambda bi, h, i: (bi, 0, FOX_HEADS + h)),
            pl.BlockSpec((None, None, s, LANES), lambda bi, h, i: (bi, h, 0, 0)),
            pl.BlockSpec((None, s, HEAD_DIM), lambda bi, h, i: (bi, 0, 2 * FOX_HEADS + h)),
        ],
        out_specs=pl.BlockSpec((None, FOX_TQ, HEAD_DIM), lambda bi, h, i: (bi, i, h)),
        out_shape=jax.ShapeDtypeStruct((b, s, FOX_WIDTH), jnp.bfloat16),
        compiler_params=_params("arbitrary", "arbitrary", "arbitrary"),
        name="fox_attention",
    )(proj, qe, proj, ke, proj)


def _mem_attention(q_ref, k_ref, v_ref):
    inv_sqrt = 1.0 / math.sqrt(HEAD_DIM)
    heads = []
    for h in range(MEM_HEADS):
        c0 = h * HEAD_DIM
        c1 = c0 + HEAD_DIM
        logits = lax.dot_general(q_ref[:, c0:c1], k_ref[:, c0:c1], (((1,), (1,)), ((), ())),
                                 preferred_element_type=jnp.float32)
        logits = logits * inv_sqrt
        m = jnp.max(logits, axis=-1, keepdims=True)
        e = jnp.exp(logits - m)
        p = e / jnp.sum(e, axis=-1, keepdims=True)
        out = jnp.dot(p.astype(jnp.bfloat16), v_ref[:, c0:c1], preferred_element_type=jnp.float32)
        heads.append(out.astype(jnp.bfloat16))
    return jnp.concatenate(heads, axis=1)


def _out_proj_kernel(x_ref, ya_ref, q_ref, k_ref, v_ref, wa_ref, wb_ref, g_ref, o_ref, hn_ref):
    acc = jnp.dot(ya_ref[...], wa_ref[...], preferred_element_type=jnp.float32)
    y_mem = _mem_attention(q_ref, k_ref, v_ref)
    acc = acc + jnp.dot(y_mem, wb_ref[...], preferred_element_type=jnp.float32)
    x_new = x_ref[...] + acc
    o_ref[...] = x_new
    hn_ref[...] = _rms_normalize(x_new, g_ref[...])


def _out_proj(x2d, y_mix, q_arr, q_col_block, kv, w_stack, layer, ffn_gain):
    rows = x2d.shape[0]
    tiles_per_batch = rows // kv.shape[0] // ROW_TILE
    mem_row_block = POOL_WIDTH // MEM_WIDTH
    return pl.pallas_call(
        _out_proj_kernel,
        grid=(rows // ROW_TILE,),
        in_specs=[
            pl.BlockSpec((ROW_TILE, D_MODEL), lambda i: (i, 0)),
            pl.BlockSpec((ROW_TILE, POOL_WIDTH), lambda i: (i, 0)),
            pl.BlockSpec((ROW_TILE, MEM_WIDTH), lambda i: (i, q_col_block)),
            pl.BlockSpec((None, MEM_LEN, MEM_WIDTH), lambda i: (i // tiles_per_batch, 0, 0)),
            pl.BlockSpec((None, MEM_LEN, MEM_WIDTH), lambda i: (i // tiles_per_batch, 0, 1)),
            pl.BlockSpec((None, POOL_WIDTH, D_MODEL), lambda i: (layer, 0, 0), pipeline_mode=_RESIDENT),
            pl.BlockSpec((None, MEM_WIDTH, D_MODEL), lambda i: (layer, mem_row_block, 0),
                         pipeline_mode=_RESIDENT),
            pl.BlockSpec((1, D_MODEL), lambda i: (0, 0)),
        ],
        out_specs=[pl.BlockSpec((ROW_TILE, D_MODEL), lambda i: (i, 0)),
                   pl.BlockSpec((ROW_TILE, D_MODEL), lambda i: (i, 0))],
        out_shape=[jax.ShapeDtypeStruct((rows, D_MODEL), jnp.float32),
                   jax.ShapeDtypeStruct((rows, D_MODEL), jnp.bfloat16)],
        compiler_params=_params("arbitrary"),
        name="out_proj",
    )(x2d, y_mix, q_arr, kv, kv, w_stack, w_stack, ffn_gain)


def _ffn_kernel(x_ref, hn_ref, w1_ref, w2_ref, g_ref, o_ref, *, final_norm):
    @pl.when(pl.program_id(1) == 0)
    def _():
        o_ref[...] = x_ref[...]

    a = jnp.dot(hn_ref[...], w1_ref[...], preferred_element_type=jnp.float32)
    a = jnp.square(jnp.maximum(a, 0.0)).astype(jnp.bfloat16)
    o_ref[...] += jnp.dot(a, w2_ref[...], preferred_element_type=jnp.float32)

    if final_norm:
        @pl.when(pl.program_id(1) == pl.num_programs(1) - 1)
        def _():
            o_ref[...] = _rms_scale(o_ref[...], g_ref[...])


def _ffn(x2d, hn2d, w1_stack, w2_stack, layer, final_gain, final_norm):
    rows = x2d.shape[0]
    return pl.pallas_call(
        functools.partial(_ffn_kernel, final_norm=final_norm),
        grid=(rows // ROW_TILE, FFN_HIDDEN // FFN_TILE),
        in_specs=[
            pl.BlockSpec((ROW_TILE, D_MODEL), lambda i, j: (i, 0)),
            pl.BlockSpec((ROW_TILE, D_MODEL), lambda i, j: (i, 0)),
            pl.BlockSpec((None, D_MODEL, FFN_TILE), lambda i, j: (layer, 0, j)),
            pl.BlockSpec((None, FFN_TILE, D_MODEL), lambda i, j: (layer, j, 0)),
            pl.BlockSpec((1, D_MODEL), lambda i, j: (0, 0)),
        ],
        out_specs=pl.BlockSpec((ROW_TILE, D_MODEL), lambda i, j: (i, 0)),
        out_shape=jax.ShapeDtypeStruct((rows, D_MODEL), jnp.float32),
        compiler_params=_params("arbitrary", "arbitrary"),
        name="ffn_final" if final_norm else "ffn",
    )(x2d, hn2d, w1_stack, w2_stack, final_gain)


def kernel(x, mem, norm_mix, norm_mem, pool_w_in, pool_w_grp, pool_scale, fox_w_in, fox_b_f,
           w_mem_kv, w_out, norm_ffn, w_ffn1, w_ffn2, norm_final):
    b, s, d = x.shape
    rows = b * s
    depth = w_out.shape[0]
    bf16 = jnp.bfloat16
    f32 = jnp.float32
    x2d = x.reshape(rows, d)
    mem2d = mem.reshape(b * MEM_LEN, d)
    w_mem_kv_b, w_out_b = w_mem_kv.astype(bf16), w_out.astype(bf16)
    w_ffn1_b, w_ffn2_b = w_ffn1.astype(bf16), w_ffn2.astype(bf16)

    def row(v):
        return v.reshape(1, -1).astype(f32)

    def mix_tail(x2d, layer, y_mix, q_arr, q_col_block):
        kv = _mem_kv_proj(mem2d, row(norm_mem[layer]), w_mem_kv_b, layer).reshape(b, MEM_LEN, 2 * MEM_WIDTH)
        x2d, hn2d = _out_proj(x2d, y_mix.reshape(rows, POOL_WIDTH), q_arr.reshape(rows, -1), q_col_block,
                              kv, w_out_b, layer, row(norm_ffn[layer]))
        return _ffn(x2d, hn2d, w_ffn1_b, w_ffn2_b, layer, row(norm_final), layer == depth - 1)

    u, q_mem = _pool_in_proj(x2d, row(norm_mix[0]), pool_w_in.astype(bf16))
    y_mix = _pool_mix(u.reshape(b, s, POOL_WIDTH), pool_w_grp.astype(bf16), row(pool_scale[0]))
    x2d = mix_tail(x2d, 0, y_mix, q_mem.reshape(b, s, MEM_WIDTH), 0)

    w_in = fox_w_in[0]
    n_qkv = 3 * FOX_WIDTH
    w_qkv = w_in[:, :n_qkv].astype(bf16)
    w_qm = w_in[:, n_qkv + FOX_HEADS:].astype(bf16)
    w_f = jnp.pad(w_in[:, n_qkv:n_qkv + FOX_HEADS], ((0, 0), (0, LANES - FOX_HEADS))).astype(bf16)
    bias_f = jnp.pad(fox_b_f[0].astype(f32), (0, LANES - FOX_HEADS)).reshape(1, LANES)
    proj, qe, ke = _fox_in_proj(x2d, row(norm_mix[1]), w_qkv, w_qm, w_f, bias_f, b)
    proj = proj.reshape(b, s, FOX_PROJ_COLS)
    y_mix = _fox_attention(proj, qe, ke)
    x2d = mix_tail(x2d, 1, y_mix, proj, FOX_PROJ_COLS // MEM_WIDTH - 1)

    return x2d.reshape(b, s, d)
```

```python
import functools
import math

import jax
import jax.numpy as jnp
import numpy as np
from jax import lax
from jax.experimental import pallas as pl
from jax.experimental.pallas import tpu as pltpu

D_MODEL = 2048
HEAD_DIM = 128
MEM_LEN = 256
MEM_HEADS = 4
MEM_WIDTH = MEM_HEADS * HEAD_DIM
POOL_WIDTH = D_MODEL - MEM_WIDTH
POOL_GROUPS = 4
POOL_GROUP_WIDTH = POOL_WIDTH // POOL_GROUPS
POOL_WINDOWS = (2, 4, 8, 16)
POOL_HALO = 32
FOX_HEADS = POOL_WIDTH // HEAD_DIM
FOX_WIDTH = FOX_HEADS * HEAD_DIM
FOX_PROJ_COLS = 3 * FOX_WIDTH + MEM_WIDTH
FFN_HIDDEN = 4 * D_MODEL
RMS_EPS = 1e-6

LANES = 128
VMEM_LIMIT = 56 * 1024 * 1024

ROW_TILE = 512
FFN_TILE = 1024
FOX_TQ = 1024
FOX_TK = 512
FOX_UNROLLS = (12,)
CUMSUM_TILE = 256

LOG2E = math.log2(math.e)
FOX_QSCALE = LOG2E / math.sqrt(HEAD_DIM)
NEG_BIG = -1e30

_RESIDENT = pl.Buffered(1)


def _params(*sem):
    return pltpu.CompilerParams(dimension_semantics=sem, vmem_limit_bytes=VMEM_LIMIT)


def _rms_scale(x, gain):
    ms = jnp.mean(x * x, axis=-1, keepdims=True)
    return (x * lax.rsqrt(ms + RMS_EPS)) * gain


def _rms_normalize(x, gain):
    return _rms_scale(x, gain).astype(jnp.bfloat16)


def _pool_in_kernel(x_ref, g_ref, w_ref, u_ref, qm_ref):
    hn = _rms_normalize(x_ref[...], g_ref[...])
    res = jnp.dot(hn, w_ref[...], preferred_element_type=jnp.float32)
    u_ref[...] = res[:, :POOL_WIDTH]
    qm_ref[...] = res[:, POOL_WIDTH:].astype(jnp.bfloat16)


def _pool_in_proj(x2d, gain, w_stack):
    rows = x2d.shape[0]
    return pl.pallas_call(
        _pool_in_kernel,
        grid=(rows // ROW_TILE,),
        in_specs=[
            pl.BlockSpec((ROW_TILE, D_MODEL), lambda i: (i, 0)),
            pl.BlockSpec((1, D_MODEL), lambda i: (0, 0)),
            pl.BlockSpec((None, D_MODEL, D_MODEL), lambda i: (0, 0, 0), pipeline_mode=_RESIDENT),
        ],
        out_specs=[
            pl.BlockSpec((ROW_TILE, POOL_WIDTH), lambda i: (i, 0)),
            pl.BlockSpec((ROW_TILE, MEM_WIDTH), lambda i: (i, 0)),
        ],
        out_shape=[
            jax.ShapeDtypeStruct((rows, POOL_WIDTH), jnp.float32),
            jax.ShapeDtypeStruct((rows, MEM_WIDTH), jnp.bfloat16),
        ],
        compiler_params=_params("arbitrary"),
        name="pool_in_proj",
    )(x2d, gain, w_stack)


def _mem_kv_kernel(x_ref, g_ref, w_ref, o_ref):
    hn = _rms_normalize(x_ref[...], g_ref[...])
    o_ref[...] = jnp.dot(hn, w_ref[...], preferred_element_type=jnp.float32).astype(jnp.bfloat16)


def _mem_kv_proj(mem2d, gain, w_stack, layer):
    rows = mem2d.shape[0]
    n = w_stack.shape[2]
    return pl.pallas_call(
        _mem_kv_kernel,
        grid=(1,),
        in_specs=[
            pl.BlockSpec((rows, D_MODEL), lambda i: (0, 0)),
            pl.BlockSpec((1, D_MODEL), lambda i: (0, 0)),
            pl.BlockSpec((None, D_MODEL, n), lambda i: (layer, 0, 0)),
        ],
        out_specs=pl.BlockSpec((rows, n), lambda i: (0, 0)),
        out_shape=jax.ShapeDtypeStruct((rows, n), jnp.bfloat16),
        compiler_params=_params("arbitrary"),
        name="mem_kv_proj",
    )(mem2d, gain, w_stack)


PIECE_LANES = 16


def _piece_selector():
    sel = np.zeros((LANES, FOX_HEADS * LANES), np.float32)
    for h in range(FOX_HEADS):
        for piece in range(3):
            sel[piece * PIECE_LANES + h, h * LANES + piece] = 1.0
            sel[piece * PIECE_LANES + h, h * LANES + 3 + piece] = -1.0
    return jnp.asarray(sel, jnp.bfloat16)


def _bf16_pieces(x):
    hi = x.astype(jnp.bfloat16).astype(jnp.float32)
    r1 = x - hi
    mid = r1.astype(jnp.bfloat16).astype(jnp.float32)
    lo = (r1 - mid).astype(jnp.bfloat16).astype(jnp.float32)
    return hi, mid, lo


def _fox_in_kernel(x_ref, g_ref, wqkv_ref, wqm_ref, wf_ref, b_ref, sel_ref, o_ref, qe_ref, ke_ref,
                   z_ref, carry_ref, *, tiles_per_batch):
    i = pl.program_id(0)

    @pl.when(i == 0)
    def _():
        z_ref[...] = jnp.zeros_like(z_ref)

    @pl.when((i == 0) | ((i - 1) % tiles_per_batch == 0))
    def _():
        carry_ref[...] = jnp.zeros_like(carry_ref)

    z = z_ref[...]
    log_f = jnp.minimum(z, 0.0) - jnp.log1p(jnp.exp(-jnp.abs(z)))
    lf_pieces = jnp.concatenate(_bf16_pieces(log_f), axis=1).astype(jnp.bfloat16)
    tc = CUMSUM_TILE
    row = lax.broadcasted_iota(jnp.int32, (tc, tc), 0)
    col = lax.broadcasted_iota(jnp.int32, (tc, tc), 1)
    tri = (col <= row).astype(jnp.bfloat16)
    lane = lax.broadcasted_iota(jnp.int32, (tc, LANES), 1)
    k_side = (lane >= 3) & (lane < 6)
    q_ones = jnp.where(k_side, 1.0, 0.0)
    k_ones = jnp.where(lane < 3, 1.0, 0.0)
    mid_lanes = (lane >= PIECE_LANES) & (lane < PIECE_LANES + FOX_HEADS)
    lo_lanes = (lane >= 2 * PIECE_LANES) & (lane < 2 * PIECE_LANES + FOX_HEADS)
    total = carry_ref[...]
    for r0 in range(0, ROW_TILE, tc):
        part = jnp.dot(tri, lf_pieces[r0:r0 + tc], preferred_element_type=jnp.float32)
        csum = (part[:, :LANES] + part[:, LANES:2 * LANES]) + part[:, 2 * LANES:] + total
        total = csum[tc - 1:tc, :]
        hi, mid, lo = _bf16_pieces(csum * LOG2E)
        packed = jnp.where(lane < FOX_HEADS, hi,
                           jnp.where(mid_lanes, pltpu.roll(mid, PIECE_LANES, 1),
                                     jnp.where(lo_lanes, pltpu.roll(lo, 2 * PIECE_LANES, 1), 0.0)))
        placed = jnp.dot(packed.astype(jnp.bfloat16), sel_ref[...],
                         preferred_element_type=jnp.float32)
        for h in range(FOX_HEADS):
            blk = placed[:, h * LANES:(h + 1) * LANES]
            qe_ref[h, r0:r0 + tc, :] = jnp.where(lane < 3, blk, q_ones).astype(jnp.bfloat16)
            ke_ref[h, r0:r0 + tc, :] = jnp.where(k_side, blk, k_ones).astype(jnp.bfloat16)
    carry_ref[...] = total

    hn = _rms_normalize(x_ref[...], g_ref[...])
    z_ref[...] = jnp.dot(hn, wf_ref[...], preferred_element_type=jnp.float32) + b_ref[...]
    for part, scale in enumerate((FOX_QSCALE, None, None)):
        cols = slice(part * FOX_WIDTH, (part + 1) * FOX_WIDTH)
        res = jnp.dot(hn, wqkv_ref[:, cols], preferred_element_type=jnp.float32)
        if scale is not None:
            res = res * scale
        o_ref[:, cols] = res.astype(jnp.bfloat16)
    o_ref[:, 3 * FOX_WIDTH:] = jnp.dot(hn, wqm_ref[...], preferred_element_type=jnp.float32).astype(jnp.bfloat16)


def _fox_in_proj(x2d, gain, w_qkv, w_qm, w_f, bias, batch):
    rows = x2d.shape[0]
    seq = rows // batch
    tiles_per_batch = seq // ROW_TILE
    last = rows // ROW_TILE - 1
    ext = jax.ShapeDtypeStruct((batch, FOX_HEADS, seq, LANES), jnp.bfloat16)

    def gate_tile(i):
        t = jnp.maximum(i - 1, 0)
        return (t // tiles_per_batch, 0, t % tiles_per_batch, 0)

    ext_spec = pl.BlockSpec((None, FOX_HEADS, ROW_TILE, LANES), gate_tile)
    return pl.pallas_call(
        functools.partial(_fox_in_kernel, tiles_per_batch=tiles_per_batch),
        grid=(last + 2,),
        in_specs=[
            pl.BlockSpec((ROW_TILE, D_MODEL), lambda i: (jnp.minimum(i, last), 0)),
            pl.BlockSpec((1, D_MODEL), lambda i: (0, 0)),
            pl.BlockSpec((D_MODEL, 3 * FOX_WIDTH), lambda i: (0, 0), pipeline_mode=_RESIDENT),
            pl.BlockSpec((D_MODEL, MEM_WIDTH), lambda i: (0, 0), pipeline_mode=_RESIDENT),
            pl.BlockSpec((D_MODEL, LANES), lambda i: (0, 0), pipeline_mode=_RESIDENT),
            pl.BlockSpec((1, LANES), lambda i: (0, 0)),
            pl.BlockSpec((LANES, FOX_HEADS * LANES), lambda i: (0, 0), pipeline_mode=_RESIDENT),
        ],
        out_specs=[pl.BlockSpec((ROW_TILE, FOX_PROJ_COLS), lambda i: (jnp.minimum(i, last), 0)),
                   ext_spec, ext_spec],
        out_shape=[jax.ShapeDtypeStruct((rows, FOX_PROJ_COLS), jnp.bfloat16), ext, ext],
        scratch_shapes=[pltpu.VMEM((ROW_TILE, LANES), jnp.float32), pltpu.VMEM((1, LANES), jnp.float32)],
        compiler_params=_params("arbitrary"),
        name="fox_in_proj",
    )(x2d, gain, w_qkv, w_qm, w_f, bias, _piece_selector())


def _pool_mix_kernel(u_ref, w_ref, s_ref, y_ref, ext_ref, lvl_ref):
    i = pl.program_id(1)
    tm = u_ref.shape[0]
    rows = tm + POOL_HALO

    @pl.when(i == 0)
    def _():
        ext_ref[0:POOL_HALO, :] = jnp.zeros((POOL_HALO, POOL_WIDTH), jnp.float32)

    @pl.when(i > 0)
    def _():
        ext_ref[0:POOL_HALO, :] = ext_ref[tm:tm + POOL_HALO, :]

    ext_ref[POOL_HALO:, :] = u_ref[...]

    t = i * tm + lax.broadcasted_iota(jnp.int32, (tm, 1), 0)
    for g, win in enumerate(POOL_WINDOWS):
        c0 = g * POOL_GROUP_WIDTH
        c1 = c0 + POOL_GROUP_WIDTH
        tok = ext_ref[POOL_HALO:, c0:c1]
        levels = win.bit_length() - 1
        wsum = None
        for lvl in range(levels):
            shift = 1 << lvl
            start = POOL_HALO - 8 * (levels - 1 - lvl)
            if lvl == 0:
                cur = ext_ref[start:, c0:c1] + ext_ref[start - shift:rows - shift, c0:c1]
            else:
                src = lvl_ref.at[(lvl - 1) % 2]
                cur = src[start:, :] + src[start - shift:rows - shift, :]
            if lvl + 1 < levels:
                lvl_ref[lvl % 2, start:, :] = cur
            else:
                wsum = cur
        count = jnp.minimum(t + 1, win).astype(jnp.float32)
        pooled = wsum / count - tok
        mixed = jnp.dot(pooled.astype(jnp.bfloat16), w_ref[g], preferred_element_type=jnp.float32)
        y_ref[:, c0:c1] = (mixed * s_ref[:, c0:c1]).astype(jnp.bfloat16)


def _pool_mix(u, w_grp_stack, scale):
    b, s, _ = u.shape
    assert all(w & (w - 1) == 0 for w in POOL_WINDOWS)
    assert POOL_HALO == 8 * (max(POOL_WINDOWS).bit_length() - 1)
    return pl.pallas_call(
        _pool_mix_kernel,
        grid=(b, s // ROW_TILE),
        in_specs=[
            pl.BlockSpec((None, ROW_TILE, POOL_WIDTH), lambda bi, i: (bi, i, 0)),
            pl.BlockSpec((None, POOL_GROUPS, POOL_GROUP_WIDTH, POOL_GROUP_WIDTH), lambda bi, i: (0, 0, 0, 0)),
            pl.BlockSpec((1, POOL_WIDTH), lambda bi, i: (0, 0)),
        ],
        out_specs=pl.BlockSpec((None, ROW_TILE, POOL_WIDTH), lambda bi, i: (bi, i, 0)),
        out_shape=jax.ShapeDtypeStruct((b, s, POOL_WIDTH), jnp.bfloat16),
        scratch_shapes=[pltpu.VMEM((ROW_TILE + POOL_HALO, POOL_WIDTH), jnp.float32),
                        pltpu.VMEM((2, ROW_TILE + POOL_HALO, POOL_GROUP_WIDTH), jnp.float32)],
        compiler_params=_params("arbitrary", "arbitrary"),
        name="pool_mix",
    )(u, w_grp_stack, scale)


def _fox_attn_kernel(q_ref, qe_ref, k_ref, ke_ref, v_ref, o_ref):
    i = pl.program_id(2)
    tq, tk = FOX_TQ, FOX_TK
    per_tile = tq // tk
    q2 = jnp.concatenate([q_ref[...], qe_ref[...]], axis=1)
    lane = lax.broadcasted_iota(jnp.int32, (tk, LANES), 1)
    v_ext = jnp.where(lane == 0, 1.0, 0.0).astype(jnp.bfloat16)

    def block(j, carry, row0=0, masked=False):
        m, acc = carry
        ks = pl.ds(pl.multiple_of(j * tk, tk), tk)
        k2 = jnp.concatenate([k_ref[ks, :], ke_ref[ks, :]], axis=1)
        v2 = jnp.concatenate([v_ref[ks, :], v_ext], axis=1)
        s = lax.dot_general(q2[row0:], k2, (((1,), (1,)), ((), ())), preferred_element_type=jnp.float32)
        if masked:
            row = lax.broadcasted_iota(jnp.int32, s.shape, 0)
            col = lax.broadcasted_iota(jnp.int32, s.shape, 1)
            s = jnp.where(col <= row, s, -jnp.inf)
        m_old = m[row0:]
        m_new = jnp.maximum(m_old, jnp.max(s, axis=-1, keepdims=True))
        alpha = jnp.exp2(m_old - m_new)
        p = jnp.exp2(s - m_new).astype(jnp.bfloat16)
        acc_new = alpha * acc[row0:] + jnp.dot(p, v2, preferred_element_type=jnp.float32)
        if row0:
            m_new = jnp.concatenate([m[:row0], m_new], axis=0)
            acc_new = jnp.concatenate([acc[:row0], acc_new], axis=0)
        return m_new, acc_new

    carry = (jnp.full((tq, 1), NEG_BIG, jnp.float32),
             jnp.zeros((tq, 2 * HEAD_DIM), jnp.float32))
    n_full = i * per_tile

    done = 0
    for n in FOX_UNROLLS:
        def body(t, c, n=n, base=done):
            for u in range(n):
                c = block(base + t * n + u, c)
            return c
        trips = (n_full - done) // n
        carry = lax.fori_loop(0, trips, body, carry)
        done = done + trips * n

    def tail(c, n_rest):
        for u in range(n_rest):
            c = block(done + u, c)
        for d in range(per_tile):
            c = block(i * per_tile + d, c, row0=d * tk, masked=True)
        _, acc = c
        o_ref[...] = (acc[:, :HEAD_DIM] / acc[:, HEAD_DIM:HEAD_DIM + 1]).astype(jnp.bfloat16)

    rest = n_full - done
    for n_rest in range(0, FOX_UNROLLS[-1], per_tile):
        pl.when(rest == n_rest)(functools.partial(tail, carry, n_rest))


def _fox_attention(proj, qe, ke):
    b, s, _ = proj.shape
    assert FOX_TQ % FOX_TK == 0 and all(n % (FOX_TQ // FOX_TK) == 0 for n in FOX_UNROLLS)
    return pl.pallas_call(
        _fox_attn_kernel,
        grid=(b, FOX_HEADS, s // FOX_TQ),
        in_specs=[
            pl.BlockSpec((None, FOX_TQ, HEAD_DIM), lambda bi, h, i: (bi, i, h)),
            pl.BlockSpec((None, None, FOX_TQ, LANES), lambda bi, h, i: (bi, h, i, 0)),
            pl.BlockSpec((None, s, HEAD_DIM), lambda bi, h, i: (bi, 0, FOX_HEADS + h)),
            pl.BlockSpec((None, None, s, LANES), lambda bi, h, i: (bi, h, 0, 0)),
            pl.BlockSpec((None, s, HEAD_DIM), lambda bi, h, i: (bi, 0, 2 * FOX_HEADS + h)),
        ],
        out_specs=pl.BlockSpec((None, FOX_TQ, HEAD_DIM), lambda bi, h, i: (bi, i, h)),
        out_shape=jax.ShapeDtypeStruct((b, s, FOX_WIDTH), jnp.bfloat16),
        compiler_params=_params("arbitrary", "arbitrary", "arbitrary"),
        name="fox_attention",
    )(proj, qe, proj, ke, proj)


def _mem_attention(q_ref, k_ref, v_ref):
    inv_sqrt = 1.0 / math.sqrt(HEAD_DIM)
    heads = []
    for h in range(MEM_HEADS):
        c0 = h * HEAD_DIM
        c1 = c0 + HEAD_DIM
        logits = lax.dot_general(q_ref[:, c0:c1], k_ref[:, c0:c1], (((1,), (1,)), ((), ())),
                                 preferred_element_type=jnp.float32)
        logits = logits * inv_sqrt
        m = jnp.max(logits, axis=-1, keepdims=True)
        e = jnp.exp(logits - m)
        p = e / jnp.sum(e, axis=-1, keepdims=True)
        out = jnp.dot(p.astype(jnp.bfloat16), v_ref[:, c0:c1], preferred_element_type=jnp.float32)
        heads.append(out.astype(jnp.bfloat16))
    return jnp.concatenate(heads, axis=1)


def _out_proj_kernel(x_ref, ya_ref, q_ref, k_ref, v_ref, wa_ref, wb_ref, g_ref, o_ref, hn_ref):
    acc = jnp.dot(ya_ref[...], wa_ref[...], preferred_element_type=jnp.float32)
    y_mem = _mem_attention(q_ref, k_ref, v_ref)
    acc = acc + jnp.dot(y_mem, wb_ref[...], preferred_element_type=jnp.float32)
    x_new = x_ref[...] + acc
    o_ref[...] = x_new
    hn_ref[...] = _rms_normalize(x_new, g_ref[...])


def _out_proj(x2d, y_mix, q_arr, q_col_block, kv, w_stack, layer, ffn_gain):
    rows = x2d.shape[0]
    tiles_per_batch = rows // kv.shape[0] // ROW_TILE
    mem_row_block = POOL_WIDTH // MEM_WIDTH
    return pl.pallas_call(
        _out_proj_kernel,
        grid=(rows // ROW_TILE,),
        in_specs=[
            pl.BlockSpec((ROW_TILE, D_MODEL), lambda i: (i, 0)),
            pl.BlockSpec((ROW_TILE, POOL_WIDTH), lambda i: (i, 0)),
            pl.BlockSpec((ROW_TILE, MEM_WIDTH), lambda i: (i, q_col_block)),
            pl.BlockSpec((None, MEM_LEN, MEM_WIDTH), lambda i: (i // tiles_per_batch, 0, 0)),
            pl.BlockSpec((None, MEM_LEN, MEM_WIDTH), lambda i: (i // tiles_per_batch, 0, 1)),
            pl.BlockSpec((None, POOL_WIDTH, D_MODEL), lambda i: (layer, 0, 0), pipeline_mode=_RESIDENT),
            pl.BlockSpec((None, MEM_WIDTH, D_MODEL), lambda i: (layer, mem_row_block, 0),
                         pipeline_mode=_RESIDENT),
            pl.BlockSpec((1, D_MODEL), lambda i: (0, 0)),
        ],
        out_specs=[pl.BlockSpec((ROW_TILE, D_MODEL), lambda i: (i, 0)),
                   pl.BlockSpec((ROW_TILE, D_MODEL), lambda i: (i, 0))],
        out_shape=[jax.ShapeDtypeStruct((rows, D_MODEL), jnp.float32),
                   jax.ShapeDtypeStruct((rows, D_MODEL), jnp.bfloat16)],
        compiler_params=_params("arbitrary"),
        name="out_proj",
    )(x2d, y_mix, q_arr, kv, kv, w_stack, w_stack, ffn_gain)


def _ffn_kernel(x_ref, hn_ref, w1_ref, w2_ref, g_ref, o_ref, *, final_norm):
    @pl.when(pl.program_id(1) == 0)
    def _():
        o_ref[...] = x_ref[...]

    a = jnp.dot(hn_ref[...], w1_ref[...], preferred_element_type=jnp.float32)
    a = jnp.square(jnp.maximum(a, 0.0)).astype(jnp.bfloat16)
    o_ref[...] += jnp.dot(a, w2_ref[...], preferred_element_type=jnp.float32)

    if final_norm:
        @pl.when(pl.program_id(1) == pl.num_programs(1) - 1)
        def _():
            o_ref[...] = _rms_scale(o_ref[...], g_ref[...])


def _ffn(x2d, hn2d, w1_stack, w2_stack, layer, final_gain, final_norm):
    rows = x2d.shape[0]
    return pl.pallas_call(
        functools.partial(_ffn_kernel, final_norm=final_norm),
        grid=(rows // ROW_TILE, FFN_HIDDEN // FFN_TILE),
        in_specs=[
            pl.BlockSpec((ROW_TILE, D_MODEL), lambda i, j: (i, 0)),
            pl.BlockSpec((ROW_TILE, D_MODEL), lambda i, j: (i, 0)),
            pl.BlockSpec((None, D_MODEL, FFN_TILE), lambda i, j: (layer, 0, j)),
            pl.BlockSpec((None, FFN_TILE, D_MODEL), lambda i, j: (layer, j, 0)),
            pl.BlockSpec((1, D_MODEL), lambda i, j: (0, 0)),
        ],
        out_specs=pl.BlockSpec((ROW_TILE, D_MODEL), lambda i, j: (i, 0)),
        out_shape=jax.ShapeDtypeStruct((rows, D_MODEL), jnp.float32),
        compiler_params=_params("arbitrary", "arbitrary"),
        name="ffn_final" if final_norm else "ffn",
    )(x2d, hn2d, w1_stack, w2_stack, final_gain)


def _cast_kernel(w_ref, o_ref):
    o_ref[...] = w_ref[...].astype(jnp.bfloat16)


def _cast_leading_cols(w, n_cols, block_cols=MEM_WIDTH):
    rows = w.shape[0]
    return pl.pallas_call(
        _cast_kernel,
        grid=(n_cols // block_cols,),
        in_specs=[pl.BlockSpec((rows, block_cols), lambda j: (0, j))],
        out_specs=pl.BlockSpec((rows, block_cols), lambda j: (0, j)),
        out_shape=jax.ShapeDtypeStruct((rows, n_cols), jnp.bfloat16),
        compiler_params=_params("arbitrary"),
        name="cast_weights",
    )(w)


def kernel(x, mem, norm_mix, norm_mem, pool_w_in, pool_w_grp, pool_scale, fox_w_in, fox_b_f,
           w_mem_kv, w_out, norm_ffn, w_ffn1, w_ffn2, norm_final):
    b, s, d = x.shape
    rows = b * s
    depth = w_out.shape[0]
    bf16 = jnp.bfloat16
    f32 = jnp.float32
    x2d = x.reshape(rows, d)
    mem2d = mem.reshape(b * MEM_LEN, d)
    w_mem_kv_b, w_out_b = w_mem_kv.astype(bf16), w_out.astype(bf16)
    w_ffn1_b, w_ffn2_b = w_ffn1.astype(bf16), w_ffn2.astype(bf16)

    def row(v):
        return v.reshape(1, -1).astype(f32)

    def mix_tail(x2d, layer, y_mix, q_arr, q_col_block):
        kv = _mem_kv_proj(mem2d, row(norm_mem[layer]), w_mem_kv_b, layer).reshape(b, MEM_LEN, 2 * MEM_WIDTH)
        x2d, hn2d = _out_proj(x2d, y_mix.reshape(rows, POOL_WIDTH), q_arr.reshape(rows, -1), q_col_block,
                              kv, w_out_b, layer, row(norm_ffn[layer]))
        return _ffn(x2d, hn2d, w_ffn1_b, w_ffn2_b, layer, row(norm_final), layer == depth - 1)

    u, q_mem = _pool_in_proj(x2d, row(norm_mix[0]), pool_w_in.astype(bf16))
    y_mix = _pool_mix(u.reshape(b, s, POOL_WIDTH), pool_w_grp.astype(bf16), row(pool_scale[0]))
    x2d = mix_tail(x2d, 0, y_mix, q_mem.reshape(b, s, MEM_WIDTH), 0)

    w_in = fox_w_in[0]
    n_qkv = 3 * FOX_WIDTH
    w_qkv = _cast_leading_cols(w_in, n_qkv)
    w_tail = w_in[:, n_qkv:]
    w_qm = w_tail[:, FOX_HEADS:].astype(bf16)
    w_f = jnp.pad(w_tail[:, :FOX_HEADS], ((0, 0), (0, LANES - FOX_HEADS))).astype(bf16)
    bias_f = jnp.pad(fox_b_f[0].astype(f32), (0, LANES - FOX_HEADS)).reshape(1, LANES)
    proj, qe, ke = _fox_in_proj(x2d, row(norm_mix[1]), w_qkv, w_qm, w_f, bias_f, b)
    proj = proj.reshape(b, s, FOX_PROJ_COLS)
    y_mix = _fox_attention(proj, qe, ke)
    x2d = mix_tail(x2d, 1, y_mix, proj, FOX_PROJ_COLS // MEM_WIDTH - 1)

    return x2d.reshape(b, s, d)
```

```python
import functools
import math

import jax
import jax.numpy as jnp
import numpy as np
from jax import lax
from jax.experimental import pallas as pl
from jax.experimental.pallas import tpu as pltpu

D_MODEL = 2048
HEAD_DIM = 128
MEM_LEN = 256
MEM_HEADS = 4
MEM_WIDTH = MEM_HEADS * HEAD_DIM
POOL_WIDTH = D_MODEL - MEM_WIDTH
POOL_GROUPS = 4
POOL_GROUP_WIDTH = POOL_WIDTH // POOL_GROUPS
POOL_WINDOWS = (2, 4, 8, 16)
POOL_HALO = 32
FOX_HEADS = POOL_WIDTH // HEAD_DIM
FOX_WIDTH = FOX_HEADS * HEAD_DIM
FOX_PROJ_COLS = 3 * FOX_WIDTH + MEM_WIDTH
FFN_HIDDEN = 4 * D_MODEL
RMS_EPS = 1e-6

LANES = 128
VMEM_LIMIT = 56 * 1024 * 1024

ROW_TILE = 512
FFN_ROW_TILE = 1024
FFN_TILE = 512
FOX_TQ = 1024
FOX_TK = 512
FOX_UNROLLS = (12,)
CUMSUM_TILE = 256

LOG2E = math.log2(math.e)
FOX_QSCALE = LOG2E / math.sqrt(HEAD_DIM)
NEG_BIG = -1e30

_RESIDENT = pl.Buffered(1)


def _params(*sem):
    return pltpu.CompilerParams(dimension_semantics=sem, vmem_limit_bytes=VMEM_LIMIT)


def _rms_scale(x, gain):
    ms = jnp.mean(x * x, axis=-1, keepdims=True)
    return (x * lax.rsqrt(ms + RMS_EPS)) * gain


def _rms_normalize(x, gain):
    return _rms_scale(x, gain).astype(jnp.bfloat16)


def _pool_in_kernel(x_ref, g_ref, w_ref, u_ref, qm_ref):
    hn = _rms_normalize(x_ref[...], g_ref[...])
    res = jnp.dot(hn, w_ref[...], preferred_element_type=jnp.float32)
    u_ref[...] = res[:, :POOL_WIDTH]
    qm_ref[...] = res[:, POOL_WIDTH:].astype(jnp.bfloat16)


def _pool_in_proj(x2d, gain, w_stack):
    rows = x2d.shape[0]
    return pl.pallas_call(
        _pool_in_kernel,
        grid=(rows // ROW_TILE,),
        in_specs=[
            pl.BlockSpec((ROW_TILE, D_MODEL), lambda i: (i, 0)),
            pl.BlockSpec((1, D_MODEL), lambda i: (0, 0)),
            pl.BlockSpec((None, D_MODEL, D_MODEL), lambda i: (0, 0, 0), pipeline_mode=_RESIDENT),
        ],
        out_specs=[
            pl.BlockSpec((ROW_TILE, POOL_WIDTH), lambda i: (i, 0)),
            pl.BlockSpec((ROW_TILE, MEM_WIDTH), lambda i: (i, 0)),
        ],
        out_shape=[
            jax.ShapeDtypeStruct((rows, POOL_WIDTH), jnp.float32),
            jax.ShapeDtypeStruct((rows, MEM_WIDTH), jnp.bfloat16),
        ],
        compiler_params=_params("arbitrary"),
        name="pool_in_proj",
    )(x2d, gain, w_stack)


def _mem_kv_kernel(x_ref, g_ref, w_ref, o_ref):
    hn = _rms_normalize(x_ref[...], g_ref[...])
    o_ref[...] = jnp.dot(hn, w_ref[...], preferred_element_type=jnp.float32).astype(jnp.bfloat16)


def _mem_kv_proj(mem2d, gain, w_stack, layer):
    rows = mem2d.shape[0]
    n = w_stack.shape[2]
    return pl.pallas_call(
        _mem_kv_kernel,
        grid=(1,),
        in_specs=[
            pl.BlockSpec((rows, D_MODEL), lambda i: (0, 0)),
            pl.BlockSpec((1, D_MODEL), lambda i: (0, 0)),
            pl.BlockSpec((None, D_MODEL, n), lambda i: (layer, 0, 0)),
        ],
        out_specs=pl.BlockSpec((rows, n), lambda i: (0, 0)),
        out_shape=jax.ShapeDtypeStruct((rows, n), jnp.bfloat16),
        compiler_params=_params("arbitrary"),
        name="mem_kv_proj",
    )(mem2d, gain, w_stack)


PIECE_LANES = 16


def _piece_selector():
    sel = np.zeros((LANES, FOX_HEADS * LANES), np.float32)
    for h in range(FOX_HEADS):
        for piece in range(3):
            sel[piece * PIECE_LANES + h, h * LANES + piece] = 1.0
            sel[piece * PIECE_LANES + h, h * LANES + 3 + piece] = -1.0
    return jnp.asarray(sel, jnp.bfloat16)


def _bf16_pieces(x):
    hi = x.astype(jnp.bfloat16).astype(jnp.float32)
    r1 = x - hi
    mid = r1.astype(jnp.bfloat16).astype(jnp.float32)
    lo = (r1 - mid).astype(jnp.bfloat16).astype(jnp.float32)
    return hi, mid, lo


def _fox_in_kernel(x_ref, g_ref, wqkv_ref, wqm_ref, wf_ref, b_ref, sel_ref, o_ref, qe_ref, ke_ref,
                   z_ref, carry_ref, *, tiles_per_batch):
    i = pl.program_id(0)

    @pl.when(i == 0)
    def _():
        z_ref[...] = jnp.zeros_like(z_ref)

    @pl.when((i == 0) | ((i - 1) % tiles_per_batch == 0))
    def _():
        carry_ref[...] = jnp.zeros_like(carry_ref)

    z = z_ref[...]
    log_f = jnp.minimum(z, 0.0) - jnp.log1p(jnp.exp(-jnp.abs(z)))
    lf_pieces = jnp.concatenate(_bf16_pieces(log_f), axis=1).astype(jnp.bfloat16)
    tc = CUMSUM_TILE
    row = lax.broadcasted_iota(jnp.int32, (tc, tc), 0)
    col = lax.broadcasted_iota(jnp.int32, (tc, tc), 1)
    tri = (col <= row).astype(jnp.bfloat16)
    lane = lax.broadcasted_iota(jnp.int32, (tc, LANES), 1)
    k_side = (lane >= 3) & (lane < 6)
    q_ones = jnp.where(k_side, 1.0, 0.0)
    k_ones = jnp.where(lane < 3, 1.0, 0.0)
    mid_lanes = (lane >= PIECE_LANES) & (lane < PIECE_LANES + FOX_HEADS)
    lo_lanes = (lane >= 2 * PIECE_LANES) & (lane < 2 * PIECE_LANES + FOX_HEADS)
    total = carry_ref[...]
    for r0 in range(0, ROW_TILE, tc):
        part = jnp.dot(tri, lf_pieces[r0:r0 + tc], preferred_element_type=jnp.float32)
        csum = (part[:, :LANES] + part[:, LANES:2 * LANES]) + part[:, 2 * LANES:] + total
        total = csum[tc - 1:tc, :]
        hi, mid, lo = _bf16_pieces(csum * LOG2E)
        packed = jnp.where(lane < FOX_HEADS, hi,
                           jnp.where(mid_lanes, pltpu.roll(mid, PIECE_LANES, 1),
                                     jnp.where(lo_lanes, pltpu.roll(lo, 2 * PIECE_LANES, 1), 0.0)))
        placed = jnp.dot(packed.astype(jnp.bfloat16), sel_ref[...],
                         preferred_element_type=jnp.float32)
        for h in range(FOX_HEADS):
            blk = placed[:, h * LANES:(h + 1) * LANES]
            qe_ref[h, r0:r0 + tc, :] = jnp.where(lane < 3, blk, q_ones).astype(jnp.bfloat16)
            ke_ref[h, r0:r0 + tc, :] = jnp.where(k_side, blk, k_ones).astype(jnp.bfloat16)
    carry_ref[...] = total

    hn = _rms_normalize(x_ref[...], g_ref[...])
    z_ref[...] = jnp.dot(hn, wf_ref[...], preferred_element_type=jnp.float32) + b_ref[...]
    for part, scale in enumerate((FOX_QSCALE, None, None)):
        cols = slice(part * FOX_WIDTH, (part + 1) * FOX_WIDTH)
        res = jnp.dot(hn, wqkv_ref[:, cols], preferred_element_type=jnp.float32)
        if scale is not None:
            res = res * scale
        o_ref[:, cols] = res.astype(jnp.bfloat16)
    o_ref[:, 3 * FOX_WIDTH:] = jnp.dot(hn, wqm_ref[...], preferred_element_type=jnp.float32).astype(jnp.bfloat16)


def _fox_in_proj(x2d, gain, w_qkv, w_qm, w_f, bias, batch):
    rows = x2d.shape[0]
    seq = rows // batch
    tiles_per_batch = seq // ROW_TILE
    last = rows // ROW_TILE - 1
    ext = jax.ShapeDtypeStruct((batch, FOX_HEADS, seq, LANES), jnp.bfloat16)

    def gate_tile(i):
        t = jnp.maximum(i - 1, 0)
        return (t // tiles_per_batch, 0, t % tiles_per_batch, 0)

    ext_spec = pl.BlockSpec((None, FOX_HEADS, ROW_TILE, LANES), gate_tile)
    return pl.pallas_call(
        functools.partial(_fox_in_kernel, tiles_per_batch=tiles_per_batch),
        grid=(last + 2,),
        in_specs=[
            pl.BlockSpec((ROW_TILE, D_MODEL), lambda i: (jnp.minimum(i, last), 0)),
            pl.BlockSpec((1, D_MODEL), lambda i: (0, 0)),
            pl.BlockSpec((D_MODEL, 3 * FOX_WIDTH), lambda i: (0, 0), pipeline_mode=_RESIDENT),
            pl.BlockSpec((D_MODEL, MEM_WIDTH), lambda i: (0, 0), pipeline_mode=_RESIDENT),
            pl.BlockSpec((D_MODEL, LANES), lambda i: (0, 0), pipeline_mode=_RESIDENT),
            pl.BlockSpec((1, LANES), lambda i: (0, 0)),
            pl.BlockSpec((LANES, FOX_HEADS * LANES), lambda i: (0, 0), pipeline_mode=_RESIDENT),
        ],
        out_specs=[pl.BlockSpec((ROW_TILE, FOX_PROJ_COLS), lambda i: (jnp.minimum(i, last), 0)),
                   ext_spec, ext_spec],
        out_shape=[jax.ShapeDtypeStruct((rows, FOX_PROJ_COLS), jnp.bfloat16), ext, ext],
        scratch_shapes=[pltpu.VMEM((ROW_TILE, LANES), jnp.float32), pltpu.VMEM((1, LANES), jnp.float32)],
        compiler_params=_params("arbitrary"),
        name="fox_in_proj",
    )(x2d, gain, w_qkv, w_qm, w_f, bias, _piece_selector())


def _pool_mix_kernel(u_ref, w_ref, s_ref, y_ref, ext_ref, lvl_ref):
    i = pl.program_id(1)
    tm = u_ref.shape[0]
    rows = tm + POOL_HALO

    @pl.when(i == 0)
    def _():
        ext_ref[0:POOL_HALO, :] = jnp.zeros((POOL_HALO, POOL_WIDTH), jnp.float32)

    @pl.when(i > 0)
    def _():
        ext_ref[0:POOL_HALO, :] = ext_ref[tm:tm + POOL_HALO, :]

    ext_ref[POOL_HALO:, :] = u_ref[...]

    t = i * tm + lax.broadcasted_iota(jnp.int32, (tm, 1), 0)
    for g, win in enumerate(POOL_WINDOWS):
        c0 = g * POOL_GROUP_WIDTH
        c1 = c0 + POOL_GROUP_WIDTH
        tok = ext_ref[POOL_HALO:, c0:c1]
        levels = win.bit_length() - 1
        wsum = None
        for lvl in range(levels):
            shift = 1 << lvl
            start = POOL_HALO - 8 * (levels - 1 - lvl)
            if lvl == 0:
                cur = ext_ref[start:, c0:c1] + ext_ref[start - shift:rows - shift, c0:c1]
            else:
                src = lvl_ref.at[(lvl - 1) % 2]
                cur = src[start:, :] + src[start - shift:rows - shift, :]
            if lvl + 1 < levels:
                lvl_ref[lvl % 2, start:, :] = cur
            else:
                wsum = cur
        count = jnp.minimum(t + 1, win).astype(jnp.float32)
        pooled = wsum / count - tok
        mixed = jnp.dot(pooled.astype(jnp.bfloat16), w_ref[g], preferred_element_type=jnp.float32)
        y_ref[:, c0:c1] = (mixed * s_ref[:, c0:c1]).astype(jnp.bfloat16)


def _pool_mix(u, w_grp_stack, scale):
    b, s, _ = u.shape
    assert all(w & (w - 1) == 0 for w in POOL_WINDOWS)
    assert POOL_HALO == 8 * (max(POOL_WINDOWS).bit_length() - 1)
    return pl.pallas_call(
        _pool_mix_kernel,
        grid=(b, s // ROW_TILE),
        in_specs=[
            pl.BlockSpec((None, ROW_TILE, POOL_WIDTH), lambda bi, i: (bi, i, 0)),
            pl.BlockSpec((None, POOL_GROUPS, POOL_GROUP_WIDTH, POOL_GROUP_WIDTH), lambda bi, i: (0, 0, 0, 0)),
            pl.BlockSpec((1, POOL_WIDTH), lambda bi, i: (0, 0)),
        ],
        out_specs=pl.BlockSpec((None, ROW_TILE, POOL_WIDTH), lambda bi, i: (bi, i, 0)),
        out_shape=jax.ShapeDtypeStruct((b, s, POOL_WIDTH), jnp.bfloat16),
        scratch_shapes=[pltpu.VMEM((ROW_TILE + POOL_HALO, POOL_WIDTH), jnp.float32),
                        pltpu.VMEM((2, ROW_TILE + POOL_HALO, POOL_GROUP_WIDTH), jnp.float32)],
        compiler_params=_params("arbitrary", "arbitrary"),
        name="pool_mix",
    )(u, w_grp_stack, scale)


def _fox_attn_kernel(q_ref, qe_ref, k_ref, ke_ref, v_ref, o_ref):
    i = pl.program_id(2)
    tq, tk = FOX_TQ, FOX_TK
    per_tile = tq // tk
    q2 = jnp.concatenate([q_ref[...], qe_ref[...]], axis=1)
    lane = lax.broadcasted_iota(jnp.int32, (tk, LANES), 1)
    v_ext = jnp.where(lane == 0, 1.0, 0.0).astype(jnp.bfloat16)

    def block(j, carry, row0=0, masked=False):
        m, acc = carry
        ks = pl.ds(pl.multiple_of(j * tk, tk), tk)
        k2 = jnp.concatenate([k_ref[ks, :], ke_ref[ks, :]], axis=1)
        v2 = jnp.concatenate([v_ref[ks, :], v_ext], axis=1)
        s = lax.dot_general(q2[row0:], k2, (((1,), (1,)), ((), ())), preferred_element_type=jnp.float32)
        if masked:
            row = lax.broadcasted_iota(jnp.int32, s.shape, 0)
            col = lax.broadcasted_iota(jnp.int32, s.shape, 1)
            s = jnp.where(col <= row, s, -jnp.inf)
        m_old = m[row0:]
        m_new = jnp.maximum(m_old, jnp.max(s, axis=-1, keepdims=True))
        alpha = jnp.exp2(m_old - m_new)
        p = jnp.exp2(s - m_new).astype(jnp.bfloat16)
        acc_new = alpha * acc[row0:] + jnp.dot(p, v2, preferred_element_type=jnp.float32)
        if row0:
            m_new = jnp.concatenate([m[:row0], m_new], axis=0)
            acc_new = jnp.concatenate([acc[:row0], acc_new], axis=0)
        return m_new, acc_new

    carry = (jnp.full((tq, 1), NEG_BIG, jnp.float32),
             jnp.zeros((tq, 2 * HEAD_DIM), jnp.float32))
    n_full = i * per_tile

    done = 0
    for n in FOX_UNROLLS:
        def body(t, c, n=n, base=done):
            for u in range(n):
                c = block(base + t * n + u, c)
            return c
        trips = (n_full - done) // n
        carry = lax.fori_loop(0, trips, body, carry)
        done = done + trips * n

    def tail(c, n_rest):
        for u in range(n_rest):
            c = block(done + u, c)
        for d in range(per_tile):
            c = block(i * per_tile + d, c, row0=d * tk, masked=True)
        _, acc = c
        o_ref[...] = (acc[:, :HEAD_DIM] / acc[:, HEAD_DIM:HEAD_DIM + 1]).astype(jnp.bfloat16)

    rest = n_full - done
    for n_rest in range(0, FOX_UNROLLS[-1], per_tile):
        pl.when(rest == n_rest)(functools.partial(tail, carry, n_rest))


def _fox_attention(proj, qe, ke):
    b, s, _ = proj.shape
    assert FOX_TQ % FOX_TK == 0 and all(n % (FOX_TQ // FOX_TK) == 0 for n in FOX_UNROLLS)
    return pl.pallas_call(
        _fox_attn_kernel,
        grid=(b, FOX_HEADS, s // FOX_TQ),
        in_specs=[
            pl.BlockSpec((None, FOX_TQ, HEAD_DIM), lambda bi, h, i: (bi, i, h)),
            pl.BlockSpec((None, None, FOX_TQ, LANES), lambda bi, h, i: (bi, h, i, 0)),
            pl.BlockSpec((None, s, HEAD_DIM), lambda bi, h, i: (bi, 0, FOX_HEADS + h)),
            pl.BlockSpec((None, None, s, LANES), lambda bi, h, i: (bi, h, 0, 0)),
            pl.BlockSpec((None, s, HEAD_DIM), lambda bi, h, i: (bi, 0, 2 * FOX_HEADS + h)),
        ],
        out_specs=pl.BlockSpec((None, FOX_TQ, HEAD_DIM), lambda bi, h, i: (bi, i, h)),
        out_shape=jax.ShapeDtypeStruct((b, s, FOX_WIDTH), jnp.bfloat16),
        compiler_params=_params("arbitrary", "arbitrary", "arbitrary"),
        name="fox_attention",
    )(proj, qe, proj, ke, proj)


def _mem_attention(q_ref, k_ref, v_ref):
    inv_sqrt = 1.0 / math.sqrt(HEAD_DIM)
    heads = []
    for h in range(MEM_HEADS):
        c0 = h * HEAD_DIM
        c1 = c0 + HEAD_DIM
        logits = lax.dot_general(q_ref[:, c0:c1], k_ref[:, c0:c1], (((1,), (1,)), ((), ())),
                                 preferred_element_type=jnp.float32)
        logits = logits * inv_sqrt
        m = jnp.max(logits, axis=-1, keepdims=True)
        e = jnp.exp(logits - m)
        p = e / jnp.sum(e, axis=-1, keepdims=True)
        out = jnp.dot(p.astype(jnp.bfloat16), v_ref[:, c0:c1], preferred_element_type=jnp.float32)
        heads.append(out.astype(jnp.bfloat16))
    return jnp.concatenate(heads, axis=1)


def _out_proj_kernel(x_ref, ya_ref, q_ref, k_ref, v_ref, wa_ref, wb_ref, g_ref, o_ref, hn_ref):
    acc = jnp.dot(ya_ref[...], wa_ref[...], preferred_element_type=jnp.float32)
    y_mem = _mem_attention(q_ref, k_ref, v_ref)
    acc = acc + jnp.dot(y_mem, wb_ref[...], preferred_element_type=jnp.float32)
    x_new = x_ref[...] + acc
    o_ref[...] = x_new
    hn_ref[...] = _rms_normalize(x_new, g_ref[...])


def _out_proj(x2d, y_mix, q_arr, q_col_block, kv, w_stack, layer, ffn_gain):
    rows = x2d.shape[0]
    tiles_per_batch = rows // kv.shape[0] // ROW_TILE
    mem_row_block = POOL_WIDTH // MEM_WIDTH
    return pl.pallas_call(
        _out_proj_kernel,
        grid=(rows // ROW_TILE,),
        in_specs=[
            pl.BlockSpec((ROW_TILE, D_MODEL), lambda i: (i, 0)),
            pl.BlockSpec((ROW_TILE, POOL_WIDTH), lambda i: (i, 0)),
            pl.BlockSpec((ROW_TILE, MEM_WIDTH), lambda i: (i, q_col_block)),
            pl.BlockSpec((None, MEM_LEN, MEM_WIDTH), lambda i: (i // tiles_per_batch, 0, 0)),
            pl.BlockSpec((None, MEM_LEN, MEM_WIDTH), lambda i: (i // tiles_per_batch, 0, 1)),
            pl.BlockSpec((None, POOL_WIDTH, D_MODEL), lambda i: (layer, 0, 0), pipeline_mode=_RESIDENT),
            pl.BlockSpec((None, MEM_WIDTH, D_MODEL), lambda i: (layer, mem_row_block, 0),
                         pipeline_mode=_RESIDENT),
            pl.BlockSpec((1, D_MODEL), lambda i: (0, 0)),
        ],
        out_specs=[pl.BlockSpec((ROW_TILE, D_MODEL), lambda i: (i, 0)),
                   pl.BlockSpec((ROW_TILE, D_MODEL), lambda i: (i, 0))],
        out_shape=[jax.ShapeDtypeStruct((rows, D_MODEL), jnp.float32),
                   jax.ShapeDtypeStruct((rows, D_MODEL), jnp.bfloat16)],
        compiler_params=_params("arbitrary"),
        name="out_proj",
    )(x2d, y_mix, q_arr, kv, kv, w_stack, w_stack, ffn_gain)


def _ffn_kernel(x_ref, hn_ref, w1_ref, w2_ref, g_ref, o_ref, *, final_norm):
    @pl.when(pl.program_id(1) == 0)
    def _():
        o_ref[...] = x_ref[...]

    a = jnp.dot(hn_ref[...], w1_ref[...], preferred_element_type=jnp.float32)
    a = jnp.square(jnp.maximum(a, 0.0)).astype(jnp.bfloat16)
    o_ref[...] += jnp.dot(a, w2_ref[...], preferred_element_type=jnp.float32)

    if final_norm:
        @pl.when(pl.program_id(1) == pl.num_programs(1) - 1)
        def _():
            o_ref[...] = _rms_scale(o_ref[...], g_ref[...])


def _ffn(x2d, hn2d, w1_stack, w2_stack, layer, final_gain, final_norm):
    rows = x2d.shape[0]
    return pl.pallas_call(
        functools.partial(_ffn_kernel, final_norm=final_norm),
        grid=(rows // FFN_ROW_TILE, FFN_HIDDEN // FFN_TILE),
        in_specs=[
            pl.BlockSpec((FFN_ROW_TILE, D_MODEL), lambda i, j: (i, 0)),
            pl.BlockSpec((FFN_ROW_TILE, D_MODEL), lambda i, j: (i, 0)),
            pl.BlockSpec((None, D_MODEL, FFN_TILE), lambda i, j: (layer, 0, j)),
            pl.BlockSpec((None, FFN_TILE, D_MODEL), lambda i, j: (layer, j, 0)),
            pl.BlockSpec((1, D_MODEL), lambda i, j: (0, 0)),
        ],
        out_specs=pl.BlockSpec((FFN_ROW_TILE, D_MODEL), lambda i, j: (i, 0)),
        out_shape=jax.ShapeDtypeStruct((rows, D_MODEL), jnp.float32),
        compiler_params=_params("arbitrary", "arbitrary"),
        name="ffn_final" if final_norm else "ffn",
    )(x2d, hn2d, w1_stack, w2_stack, final_gain)


def kernel(x, mem, norm_mix, norm_mem, pool_w_in, pool_w_grp, pool_scale, fox_w_in, fox_b_f,
           w_mem_kv, w_out, norm_ffn, w_ffn1, w_ffn2, norm_final):
    b, s, d = x.shape
    rows = b * s
    depth = w_out.shape[0]
    bf16 = jnp.bfloat16
    f32 = jnp.float32
    x2d = x.reshape(rows, d)
    mem2d = mem.reshape(b * MEM_LEN, d)
    w_mem_kv_b, w_out_b = w_mem_kv.astype(bf16), w_out.astype(bf16)
    w_ffn1_b, w_ffn2_b = w_ffn1.astype(bf16), w_ffn2.astype(bf16)

    def row(v):
        return v.reshape(1, -1).astype(f32)

    def mix_tail(x2d, layer, y_mix, q_arr, q_col_block):
        kv = _mem_kv_proj(mem2d, row(norm_mem[layer]), w_mem_kv_b, layer).reshape(b, MEM_LEN, 2 * MEM_WIDTH)
        x2d, hn2d = _out_proj(x2d, y_mix.reshape(rows, POOL_WIDTH), q_arr.reshape(rows, -1), q_col_block,
                              kv, w_out_b, layer, row(norm_ffn[layer]))
        return _ffn(x2d, hn2d, w_ffn1_b, w_ffn2_b, layer, row(norm_final), layer == depth - 1)

    u, q_mem = _pool_in_proj(x2d, row(norm_mix[0]), pool_w_in.astype(bf16))
    y_mix = _pool_mix(u.reshape(b, s, POOL_WIDTH), pool_w_grp.astype(bf16), row(pool_scale[0]))
    x2d = mix_tail(x2d, 0, y_mix, q_mem.reshape(b, s, MEM_WIDTH), 0)

    w_in = fox_w_in[0]
    n_qkv = 3 * FOX_WIDTH
    w_qkv = w_in[:, :n_qkv].astype(bf16)
    w_qm = w_in[:, n_qkv + FOX_HEADS:].astype(bf16)
    w_f = jnp.pad(w_in[:, n_qkv:n_qkv + FOX_HEADS], ((0, 0), (0, LANES - FOX_HEADS))).astype(bf16)
    bias_f = jnp.pad(fox_b_f[0].astype(f32), (0, LANES - FOX_HEADS)).reshape(1, LANES)
    proj, qe, ke = _fox_in_proj(x2d, row(norm_mix[1]), w_qkv, w_qm, w_f, bias_f, b)
    proj = proj.reshape(b, s, FOX_PROJ_COLS)
    y_mix = _fox_attention(proj, qe, ke)
    x2d = mix_tail(x2d, 1, y_mix, proj, FOX_PROJ_COLS // MEM_WIDTH - 1)

    return x2d.reshape(b, s, d)
```

```python
import functools
import math

import jax
import jax.numpy as jnp
import numpy as np
from jax import lax
from jax.experimental import pallas as pl
from jax.experimental.pallas import tpu as pltpu

D_MODEL = 2048
HEAD_DIM = 128
MEM_LEN = 256
MEM_HEADS = 4
MEM_WIDTH = MEM_HEADS * HEAD_DIM
POOL_WIDTH = D_MODEL - MEM_WIDTH
POOL_GROUPS = 4
POOL_GROUP_WIDTH = POOL_WIDTH // POOL_GROUPS
POOL_WINDOWS = (2, 4, 8, 16)
POOL_HALO = 32
FOX_HEADS = POOL_WIDTH // HEAD_DIM
FOX_WIDTH = FOX_HEADS * HEAD_DIM
FOX_PROJ_COLS = 3 * FOX_WIDTH + MEM_WIDTH
FFN_HIDDEN = 4 * D_MODEL
RMS_EPS = 1e-6

LANES = 128
VMEM_LIMIT = 56 * 1024 * 1024
FFN_VMEM_LIMIT = 61 * 1024 * 1024

ROW_TILE = 512
FFN_TILE = 2048
FOX_TQ = 1024
FOX_TK = 512
FOX_UNROLLS = (12,)
CUMSUM_TILE = 256

LOG2E = math.log2(math.e)
FOX_QSCALE = LOG2E / math.sqrt(HEAD_DIM)
NEG_BIG = -1e30

_RESIDENT = pl.Buffered(1)


def _params(*sem):
    return pltpu.CompilerParams(dimension_semantics=sem, vmem_limit_bytes=VMEM_LIMIT)


def _rms_scale(x, gain):
    ms = jnp.mean(x * x, axis=-1, keepdims=True)
    return (x * lax.rsqrt(ms + RMS_EPS)) * gain


def _rms_normalize(x, gain):
    return _rms_scale(x, gain).astype(jnp.bfloat16)


def _pool_in_kernel(x_ref, g_ref, w_ref, u_ref, qm_ref):
    hn = _rms_normalize(x_ref[...], g_ref[...])
    res = jnp.dot(hn, w_ref[...], preferred_element_type=jnp.float32)
    u_ref[...] = res[:, :POOL_WIDTH]
    qm_ref[...] = res[:, POOL_WIDTH:].astype(jnp.bfloat16)


def _pool_in_proj(x2d, gain, w_stack):
    rows = x2d.shape[0]
    return pl.pallas_call(
        _pool_in_kernel,
        grid=(rows // ROW_TILE,),
        in_specs=[
            pl.BlockSpec((ROW_TILE, D_MODEL), lambda i: (i, 0)),
            pl.BlockSpec((1, D_MODEL), lambda i: (0, 0)),
            pl.BlockSpec((None, D_MODEL, D_MODEL), lambda i: (0, 0, 0), pipeline_mode=_RESIDENT),
        ],
        out_specs=[
            pl.BlockSpec((ROW_TILE, POOL_WIDTH), lambda i: (i, 0)),
            pl.BlockSpec((ROW_TILE, MEM_WIDTH), lambda i: (i, 0)),
        ],
        out_shape=[
            jax.ShapeDtypeStruct((rows, POOL_WIDTH), jnp.float32),
            jax.ShapeDtypeStruct((rows, MEM_WIDTH), jnp.bfloat16),
        ],
        compiler_params=_params("arbitrary"),
        name="pool_in_proj",
    )(x2d, gain, w_stack)


def _mem_kv_kernel(x_ref, g_ref, w_ref, o_ref):
    hn = _rms_normalize(x_ref[...], g_ref[...])
    o_ref[...] = jnp.dot(hn, w_ref[...], preferred_element_type=jnp.float32).astype(jnp.bfloat16)


def _mem_kv_proj(mem2d, gain, w_stack, layer):
    rows = mem2d.shape[0]
    n = w_stack.shape[2]
    return pl.pallas_call(
        _mem_kv_kernel,
        grid=(1,),
        in_specs=[
            pl.BlockSpec((rows, D_MODEL), lambda i: (0, 0)),
            pl.BlockSpec((1, D_MODEL), lambda i: (0, 0)),
            pl.BlockSpec((None, D_MODEL, n), lambda i: (layer, 0, 0)),
        ],
        out_specs=pl.BlockSpec((rows, n), lambda i: (0, 0)),
        out_shape=jax.ShapeDtypeStruct((rows, n), jnp.bfloat16),
        compiler_params=_params("arbitrary"),
        name="mem_kv_proj",
    )(mem2d, gain, w_stack)


PIECE_LANES = 16


def _piece_selector():
    sel = np.zeros((LANES, FOX_HEADS * LANES), np.float32)
    for h in range(FOX_HEADS):
        for piece in range(3):
            sel[piece * PIECE_LANES + h, h * LANES + piece] = 1.0
            sel[piece * PIECE_LANES + h, h * LANES + 3 + piece] = -1.0
    return jnp.asarray(sel, jnp.bfloat16)


def _bf16_pieces(x):
    hi = x.astype(jnp.bfloat16).astype(jnp.float32)
    r1 = x - hi
    mid = r1.astype(jnp.bfloat16).astype(jnp.float32)
    lo = (r1 - mid).astype(jnp.bfloat16).astype(jnp.float32)
    return hi, mid, lo


def _fox_in_kernel(x_ref, g_ref, wqkv_ref, wqm_ref, wf_ref, b_ref, sel_ref, o_ref, qe_ref, ke_ref,
                   z_ref, carry_ref, *, tiles_per_batch):
    i = pl.program_id(0)

    @pl.when(i == 0)
    def _():
        z_ref[...] = jnp.zeros_like(z_ref)

    @pl.when((i == 0) | ((i - 1) % tiles_per_batch == 0))
    def _():
        carry_ref[...] = jnp.zeros_like(carry_ref)

    z = z_ref[...]
    log_f = jnp.minimum(z, 0.0) - jnp.log1p(jnp.exp(-jnp.abs(z)))
    lf_pieces = jnp.concatenate(_bf16_pieces(log_f), axis=1).astype(jnp.bfloat16)
    tc = CUMSUM_TILE
    row = lax.broadcasted_iota(jnp.int32, (tc, tc), 0)
    col = lax.broadcasted_iota(jnp.int32, (tc, tc), 1)
    tri = (col <= row).astype(jnp.bfloat16)
    lane = lax.broadcasted_iota(jnp.int32, (tc, LANES), 1)
    k_side = (lane >= 3) & (lane < 6)
    q_ones = jnp.where(k_side, 1.0, 0.0)
    k_ones = jnp.where(lane < 3, 1.0, 0.0)
    mid_lanes = (lane >= PIECE_LANES) & (lane < PIECE_LANES + FOX_HEADS)
    lo_lanes = (lane >= 2 * PIECE_LANES) & (lane < 2 * PIECE_LANES + FOX_HEADS)
    total = carry_ref[...]
    for r0 in range(0, ROW_TILE, tc):
        part = jnp.dot(tri, lf_pieces[r0:r0 + tc], preferred_element_type=jnp.float32)
        csum = (part[:, :LANES] + part[:, LANES:2 * LANES]) + part[:, 2 * LANES:] + total
        total = csum[tc - 1:tc, :]
        hi, mid, lo = _bf16_pieces(csum * LOG2E)
        packed = jnp.where(lane < FOX_HEADS, hi,
                           jnp.where(mid_lanes, pltpu.roll(mid, PIECE_LANES, 1),
                                     jnp.where(lo_lanes, pltpu.roll(lo, 2 * PIECE_LANES, 1), 0.0)))
        placed = jnp.dot(packed.astype(jnp.bfloat16), sel_ref[...],
                         preferred_element_type=jnp.float32)
        for h in range(FOX_HEADS):
            blk = placed[:, h * LANES:(h + 1) * LANES]
            qe_ref[h, r0:r0 + tc, :] = jnp.where(lane < 3, blk, q_ones).astype(jnp.bfloat16)
            ke_ref[h, r0:r0 + tc, :] = jnp.where(k_side, blk, k_ones).astype(jnp.bfloat16)
    carry_ref[...] = total

    hn = _rms_normalize(x_ref[...], g_ref[...])
    z_ref[...] = jnp.dot(hn, wf_ref[...], preferred_element_type=jnp.float32) + b_ref[...]
    for part, scale in enumerate((FOX_QSCALE, None, None)):
        cols = slice(part * FOX_WIDTH, (part + 1) * FOX_WIDTH)
        res = jnp.dot(hn, wqkv_ref[:, cols], preferred_element_type=jnp.float32)
        if scale is not None:
            res = res * scale
        o_ref[:, cols] = res.astype(jnp.bfloat16)
    o_ref[:, 3 * FOX_WIDTH:] = jnp.dot(hn, wqm_ref[...], preferred_element_type=jnp.float32).astype(jnp.bfloat16)


def _fox_in_proj(x2d, gain, w_qkv, w_qm, w_f, bias, batch):
    rows = x2d.shape[0]
    seq = rows // batch
    tiles_per_batch = seq // ROW_TILE
    last = rows // ROW_TILE - 1
    ext = jax.ShapeDtypeStruct((batch, FOX_HEADS, seq, LANES), jnp.bfloat16)

    def gate_tile(i):
        t = jnp.maximum(i - 1, 0)
        return (t // tiles_per_batch, 0, t % tiles_per_batch, 0)

    ext_spec = pl.BlockSpec((None, FOX_HEADS, ROW_TILE, LANES), gate_tile)
    return pl.pallas_call(
        functools.partial(_fox_in_kernel, tiles_per_batch=tiles_per_batch),
        grid=(last + 2,),
        in_specs=[
            pl.BlockSpec((ROW_TILE, D_MODEL), lambda i: (jnp.minimum(i, last), 0)),
            pl.BlockSpec((1, D_MODEL), lambda i: (0, 0)),
            pl.BlockSpec((D_MODEL, 3 * FOX_WIDTH), lambda i: (0, 0), pipeline_mode=_RESIDENT),
            pl.BlockSpec((D_MODEL, MEM_WIDTH), lambda i: (0, 0), pipeline_mode=_RESIDENT),
            pl.BlockSpec((D_MODEL, LANES), lambda i: (0, 0), pipeline_mode=_RESIDENT),
            pl.BlockSpec((1, LANES), lambda i: (0, 0)),
            pl.BlockSpec((LANES, FOX_HEADS * LANES), lambda i: (0, 0), pipeline_mode=_RESIDENT),
        ],
        out_specs=[pl.BlockSpec((ROW_TILE, FOX_PROJ_COLS), lambda i: (jnp.minimum(i, last), 0)),
                   ext_spec, ext_spec],
        out_shape=[jax.ShapeDtypeStruct((rows, FOX_PROJ_COLS), jnp.bfloat16), ext, ext],
        scratch_shapes=[pltpu.VMEM((ROW_TILE, LANES), jnp.float32), pltpu.VMEM((1, LANES), jnp.float32)],
        compiler_params=_params("arbitrary"),
        name="fox_in_proj",
    )(x2d, gain, w_qkv, w_qm, w_f, bias, _piece_selector())


def _pool_mix_kernel(u_ref, w_ref, s_ref, y_ref, ext_ref, lvl_ref):
    i = pl.program_id(1)
    tm = u_ref.shape[0]
    rows = tm + POOL_HALO

    @pl.when(i == 0)
    def _():
        ext_ref[0:POOL_HALO, :] = jnp.zeros((POOL_HALO, POOL_WIDTH), jnp.float32)

    @pl.when(i > 0)
    def _():
        ext_ref[0:POOL_HALO, :] = ext_ref[tm:tm + POOL_HALO, :]

    ext_ref[POOL_HALO:, :] = u_ref[...]

    t = i * tm + lax.broadcasted_iota(jnp.int32, (tm, 1), 0)
    for g, win in enumerate(POOL_WINDOWS):
        c0 = g * POOL_GROUP_WIDTH
        c1 = c0 + POOL_GROUP_WIDTH
        tok = ext_ref[POOL_HALO:, c0:c1]
        levels = win.bit_length() - 1
        wsum = None
        for lvl in range(levels):
            shift = 1 << lvl
            start = POOL_HALO - 8 * (levels - 1 - lvl)
            if lvl == 0:
                cur = ext_ref[start:, c0:c1] + ext_ref[start - shift:rows - shift, c0:c1]
            else:
                src = lvl_ref.at[(lvl - 1) % 2]
                cur = src[start:, :] + src[start - shift:rows - shift, :]
            if lvl + 1 < levels:
                lvl_ref[lvl % 2, start:, :] = cur
            else:
                wsum = cur
        count = jnp.minimum(t + 1, win).astype(jnp.float32)
        pooled = wsum / count - tok
        mixed = jnp.dot(pooled.astype(jnp.bfloat16), w_ref[g], preferred_element_type=jnp.float32)
        y_ref[:, c0:c1] = (mixed * s_ref[:, c0:c1]).astype(jnp.bfloat16)


def _pool_mix(u, w_grp_stack, scale):
    b, s, _ = u.shape
    assert all(w & (w - 1) == 0 for w in POOL_WINDOWS)
    assert POOL_HALO == 8 * (max(POOL_WINDOWS).bit_length() - 1)
    return pl.pallas_call(
        _pool_mix_kernel,
        grid=(b, s // ROW_TILE),
        in_specs=[
            pl.BlockSpec((None, ROW_TILE, POOL_WIDTH), lambda bi, i: (bi, i, 0)),
            pl.BlockSpec((None, POOL_GROUPS, POOL_GROUP_WIDTH, POOL_GROUP_WIDTH), lambda bi, i: (0, 0, 0, 0)),
            pl.BlockSpec((1, POOL_WIDTH), lambda bi, i: (0, 0)),
        ],
        out_specs=pl.BlockSpec((None, ROW_TILE, POOL_WIDTH), lambda bi, i: (bi, i, 0)),
        out_shape=jax.ShapeDtypeStruct((b, s, POOL_WIDTH), jnp.bfloat16),
        scratch_shapes=[pltpu.VMEM((ROW_TILE + POOL_HALO, POOL_WIDTH), jnp.float32),
                        pltpu.VMEM((2, ROW_TILE + POOL_HALO, POOL_GROUP_WIDTH), jnp.float32)],
        compiler_params=_params("arbitrary", "arbitrary"),
        name="pool_mix",
    )(u, w_grp_stack, scale)


def _fox_attn_kernel(q_ref, qe_ref, k_ref, ke_ref, v_ref, o_ref):
    i = pl.program_id(2)
    tq, tk = FOX_TQ, FOX_TK
    per_tile = tq // tk
    q2 = jnp.concatenate([q_ref[...], qe_ref[...]], axis=1)
    lane = lax.broadcasted_iota(jnp.int32, (tk, LANES), 1)
    v_ext = jnp.where(lane == 0, 1.0, 0.0).astype(jnp.bfloat16)

    def block(j, carry, row0=0, masked=False):
        m, acc = carry
        ks = pl.ds(pl.multiple_of(j * tk, tk), tk)
        k2 = jnp.concatenate([k_ref[ks, :], ke_ref[ks, :]], axis=1)
        v2 = jnp.concatenate([v_ref[ks, :], v_ext], axis=1)
        s = lax.dot_general(q2[row0:], k2, (((1,), (1,)), ((), ())), preferred_element_type=jnp.float32)
        if masked:
            row = lax.broadcasted_iota(jnp.int32, s.shape, 0)
            col = lax.broadcasted_iota(jnp.int32, s.shape, 1)
            s = jnp.where(col <= row, s, -jnp.inf)
        m_old = m[row0:]
        m_new = jnp.maximum(m_old, jnp.max(s, axis=-1, keepdims=True))
        alpha = jnp.exp2(m_old - m_new)
        p = jnp.exp2(s - m_new).astype(jnp.bfloat16)
        acc_new = alpha * acc[row0:] + jnp.dot(p, v2, preferred_element_type=jnp.float32)
        if row0:
            m_new = jnp.concatenate([m[:row0], m_new], axis=0)
            acc_new = jnp.concatenate([acc[:row0], acc_new], axis=0)
        return m_new, acc_new

    carry = (jnp.full((tq, 1), NEG_BIG, jnp.float32),
             jnp.zeros((tq, 2 * HEAD_DIM), jnp.float32))
    n_full = i * per_tile

    done = 0
    for n in FOX_UNROLLS:
        def body(t, c, n=n, base=done):
            for u in range(n):
                c = block(base + t * n + u, c)
            return c
        trips = (n_full - done) // n
        carry = lax.fori_loop(0, trips, body, carry)
        done = done + trips * n

    def tail(c, n_rest):
        for u in range(n_rest):
            c = block(done + u, c)
        for d in range(per_tile):
            c = block(i * per_tile + d, c, row0=d * tk, masked=True)
        _, acc = c
        o_ref[...] = (acc[:, :HEAD_DIM] / acc[:, HEAD_DIM:HEAD_DIM + 1]).astype(jnp.bfloat16)

    rest = n_full - done
    for n_rest in range(0, FOX_UNROLLS[-1], per_tile):
        pl.when(rest == n_rest)(functools.partial(tail, carry, n_rest))


def _fox_attention(proj, qe, ke):
    b, s, _ = proj.shape
    assert FOX_TQ % FOX_TK == 0 and all(n % (FOX_TQ // FOX_TK) == 0 for n in FOX_UNROLLS)
    return pl.pallas_call(
        _fox_attn_kernel,
        grid=(b, FOX_HEADS, s // FOX_TQ),
        in_specs=[
            pl.BlockSpec((None, FOX_TQ, HEAD_DIM), lambda bi, h, i: (bi, i, h)),
            pl.BlockSpec((None, None, FOX_TQ, LANES), lambda bi, h, i: (bi, h, i, 0)),
            pl.BlockSpec((None, s, HEAD_DIM), lambda bi, h, i: (bi, 0, FOX_HEADS + h)),
            pl.BlockSpec((None, None, s, LANES), lambda bi, h, i: (bi, h, 0, 0)),
            pl.BlockSpec((None, s, HEAD_DIM), lambda bi, h, i: (bi, 0, 2 * FOX_HEADS + h)),
        ],
        out_specs=pl.BlockSpec((None, FOX_TQ, HEAD_DIM), lambda bi, h, i: (bi, i, h)),
        out_shape=jax.ShapeDtypeStruct((b, s, FOX_WIDTH), jnp.bfloat16),
        compiler_params=_params("arbitrary", "arbitrary", "arbitrary"),
        name="fox_attention",
    )(proj, qe, proj, ke, proj)


def _mem_attention(q_ref, k_ref, v_ref):
    inv_sqrt = 1.0 / math.sqrt(HEAD_DIM)
    heads = []
    for h in range(MEM_HEADS):
        c0 = h * HEAD_DIM
        c1 = c0 + HEAD_DIM
        logits = lax.dot_general(q_ref[:, c0:c1], k_ref[:, c0:c1], (((1,), (1,)), ((), ())),
                                 preferred_element_type=jnp.float32)
        logits = logits * inv_sqrt
        m = jnp.max(logits, axis=-1, keepdims=True)
        e = jnp.exp(logits - m)
        p = e / jnp.sum(e, axis=-1, keepdims=True)
        out = jnp.dot(p.astype(jnp.bfloat16), v_ref[:, c0:c1], preferred_element_type=jnp.float32)
        heads.append(out.astype(jnp.bfloat16))
    return jnp.concatenate(heads, axis=1)


def _out_proj_kernel(x_ref, ya_ref, q_ref, k_ref, v_ref, wa_ref, wb_ref, g_ref, o_ref, hn_ref):
    acc = jnp.dot(ya_ref[...], wa_ref[...], preferred_element_type=jnp.float32)
    y_mem = _mem_attention(q_ref, k_ref, v_ref)
    acc = acc + jnp.dot(y_mem, wb_ref[...], preferred_element_type=jnp.float32)
    x_new = x_ref[...] + acc
    o_ref[...] = x_new
    hn_ref[...] = _rms_normalize(x_new, g_ref[...])


def _out_proj(x2d, y_mix, q_arr, q_col_block, kv, w_stack, layer, ffn_gain):
    rows = x2d.shape[0]
    tiles_per_batch = rows // kv.shape[0] // ROW_TILE
    mem_row_block = POOL_WIDTH // MEM_WIDTH
    return pl.pallas_call(
        _out_proj_kernel,
        grid=(rows // ROW_TILE,),
        in_specs=[
            pl.BlockSpec((ROW_TILE, D_MODEL), lambda i: (i, 0)),
            pl.BlockSpec((ROW_TILE, POOL_WIDTH), lambda i: (i, 0)),
            pl.BlockSpec((ROW_TILE, MEM_WIDTH), lambda i: (i, q_col_block)),
            pl.BlockSpec((None, MEM_LEN, MEM_WIDTH), lambda i: (i // tiles_per_batch, 0, 0)),
            pl.BlockSpec((None, MEM_LEN, MEM_WIDTH), lambda i: (i // tiles_per_batch, 0, 1)),
            pl.BlockSpec((None, POOL_WIDTH, D_MODEL), lambda i: (layer, 0, 0), pipeline_mode=_RESIDENT),
            pl.BlockSpec((None, MEM_WIDTH, D_MODEL), lambda i: (layer, mem_row_block, 0),
                         pipeline_mode=_RESIDENT),
            pl.BlockSpec((1, D_MODEL), lambda i: (0, 0)),
        ],
        out_specs=[pl.BlockSpec((ROW_TILE, D_MODEL), lambda i: (i, 0)),
                   pl.BlockSpec((ROW_TILE, D_MODEL), lambda i: (i, 0))],
        out_shape=[jax.ShapeDtypeStruct((rows, D_MODEL), jnp.float32),
                   jax.ShapeDtypeStruct((rows, D_MODEL), jnp.bfloat16)],
        compiler_params=_params("arbitrary"),
        name="out_proj",
    )(x2d, y_mix, q_arr, kv, kv, w_stack, w_stack, ffn_gain)


def _ffn_kernel(x_ref, hn_ref, w1_ref, w2_ref, g_ref, o_ref, *, final_norm):
    @pl.when(pl.program_id(1) == 0)
    def _():
        o_ref[...] = x_ref[...]

    a = jnp.dot(hn_ref[...], w1_ref[...], preferred_element_type=jnp.float32)
    a = jnp.square(jnp.maximum(a, 0.0)).astype(jnp.bfloat16)
    o_ref[...] += jnp.dot(a, w2_ref[...], preferred_element_type=jnp.float32)

    if final_norm:
        @pl.when(pl.program_id(1) == pl.num_programs(1) - 1)
        def _():
            o_ref[...] = _rms_scale(o_ref[...], g_ref[...])


def _ffn(x2d, hn2d, w1_stack, w2_stack, layer, final_gain, final_norm):
    rows = x2d.shape[0]
    return pl.pallas_call(
        functools.partial(_ffn_kernel, final_norm=final_norm),
        grid=(rows // ROW_TILE, FFN_HIDDEN // FFN_TILE),
        in_specs=[
            pl.BlockSpec((ROW_TILE, D_MODEL), lambda i, j: (i, 0)),
            pl.BlockSpec((ROW_TILE, D_MODEL), lambda i, j: (i, 0)),
            pl.BlockSpec((None, D_MODEL, FFN_TILE), lambda i, j: (layer, 0, j)),
            pl.BlockSpec((None, FFN_TILE, D_MODEL), lambda i, j: (layer, j, 0)),
            pl.BlockSpec((1, D_MODEL), lambda i, j: (0, 0)),
        ],
        out_specs=pl.BlockSpec((ROW_TILE, D_MODEL), lambda i, j: (i, 0)),
        out_shape=jax.ShapeDtypeStruct((rows, D_MODEL), jnp.float32),
        compiler_params=pltpu.CompilerParams(dimension_semantics=("arbitrary", "arbitrary"),
                                             vmem_limit_bytes=FFN_VMEM_LIMIT),
        name="ffn_final" if final_norm else "ffn",
    )(x2d, hn2d, w1_stack, w2_stack, final_gain)


def kernel(x, mem, norm_mix, norm_mem, pool_w_in, pool_w_grp, pool_scale, fox_w_in, fox_b_f,
           w_mem_kv, w_out, norm_ffn, w_ffn1, w_ffn2, norm_final):
    b, s, d = x.shape
    rows = b * s
    depth = w_out.shape[0]
    bf16 = jnp.bfloat16
    f32 = jnp.float32
    x2d = x.reshape(rows, d)
    mem2d = mem.reshape(b * MEM_LEN, d)
    w_mem_kv_b, w_out_b = w_mem_kv.astype(bf16), w_out.astype(bf16)
    w_ffn1_b, w_ffn2_b = w_ffn1.astype(bf16), w_ffn2.astype(bf16)

    def row(v):
        return v.reshape(1, -1).astype(f32)

    def mix_tail(x2d, layer, y_mix, q_arr, q_col_block):
        kv = _mem_kv_proj(mem2d, row(norm_mem[layer]), w_mem_kv_b, layer).reshape(b, MEM_LEN, 2 * MEM_WIDTH)
        x2d, hn2d = _out_proj(x2d, y_mix.reshape(rows, POOL_WIDTH), q_arr.reshape(rows, -1), q_col_block,
                              kv, w_out_b, layer, row(norm_ffn[layer]))
        return _ffn(x2d, hn2d, w_ffn1_b, w_ffn2_b, layer, row(norm_final), layer == depth - 1)

    u, q_mem = _pool_in_proj(x2d, row(norm_mix[0]), pool_w_in.astype(bf16))
    y_mix = _pool_mix(u.reshape(b, s, POOL_WIDTH), pool_w_grp.astype(bf16), row(pool_scale[0]))
    x2d = mix_tail(x2d, 0, y_mix, q_mem.reshape(b, s, MEM_WIDTH), 0)

    w_in = fox_w_in[0]
    n_qkv = 3 * FOX_WIDTH
    w_qkv = w_in[:, :n_qkv].astype(bf16)
    w_qm = w_in[:, n_qkv + FOX_HEADS:].astype(bf16)
    w_f = jnp.pad(w_in[:, n_qkv:n_qkv + FOX_HEADS], ((0, 0), (0, LANES - FOX_HEADS))).astype(bf16)
    bias_f = jnp.pad(fox_b_f[0].astype(f32), (0, LANES - FOX_HEADS)).reshape(1, LANES)
    proj, qe, ke = _fox_in_proj(x2d, row(norm_mix[1]), w_qkv, w_qm, w_f, bias_f, b)
    proj = proj.reshape(b, s, FOX_PROJ_COLS)
    y_mix = _fox_attention(proj, qe, ke)
    x2d = mix_tail(x2d, 1, y_mix, proj, FOX_PROJ_COLS // MEM_WIDTH - 1)

    return x2d.reshape(b, s, d)
```

```python
import functools
import math

import jax
import jax.numpy as jnp
import numpy as np
from jax import lax
from jax.experimental import pallas as pl
from jax.experimental.pallas import tpu as pltpu

D_MODEL = 2048
HEAD_DIM = 128
MEM_LEN = 256
MEM_HEADS = 4
MEM_WIDTH = MEM_HEADS * HEAD_DIM
POOL_WIDTH = D_MODEL - MEM_WIDTH
POOL_GROUPS = 4
POOL_GROUP_WIDTH = POOL_WIDTH // POOL_GROUPS
POOL_WINDOWS = (2, 4, 8, 16)
POOL_HALO = 32
FOX_HEADS = POOL_WIDTH // HEAD_DIM
FOX_WIDTH = FOX_HEADS * HEAD_DIM
FOX_PROJ_COLS = 3 * FOX_WIDTH + MEM_WIDTH
FFN_HIDDEN = 4 * D_MODEL
RMS_EPS = 1e-6

LANES = 128
VMEM_LIMIT = 56 * 1024 * 1024
FFN_VMEM_LIMIT = 61 * 1024 * 1024

ROW_TILE = 512
FFN_TILE = 2048
FOX_TQ = 1024
FOX_TK = 512
FOX_UNROLLS = (8,)
FOX_PAIR = 2
CUMSUM_TILE = 256

LOG2E = math.log2(math.e)
FOX_QSCALE = LOG2E / math.sqrt(HEAD_DIM)
NEG_BIG = -1e30

_RESIDENT = pl.Buffered(1)


def _params(*sem):
    return pltpu.CompilerParams(dimension_semantics=sem, vmem_limit_bytes=VMEM_LIMIT)


def _rms_scale(x, gain):
    ms = jnp.mean(x * x, axis=-1, keepdims=True)
    return (x * lax.rsqrt(ms + RMS_EPS)) * gain


def _rms_normalize(x, gain):
    return _rms_scale(x, gain).astype(jnp.bfloat16)


def _pool_in_kernel(x_ref, g_ref, w_ref, u_ref, qm_ref):
    hn = _rms_normalize(x_ref[...], g_ref[...])
    res = jnp.dot(hn, w_ref[...], preferred_element_type=jnp.float32)
    u_ref[...] = res[:, :POOL_WIDTH]
    qm_ref[...] = res[:, POOL_WIDTH:].astype(jnp.bfloat16)


def _pool_in_proj(x2d, gain, w_stack):
    rows = x2d.shape[0]
    return pl.pallas_call(
        _pool_in_kernel,
        grid=(rows // ROW_TILE,),
        in_specs=[
            pl.BlockSpec((ROW_TILE, D_MODEL), lambda i: (i, 0)),
            pl.BlockSpec((1, D_MODEL), lambda i: (0, 0)),
            pl.BlockSpec((None, D_MODEL, D_MODEL), lambda i: (0, 0, 0), pipeline_mode=_RESIDENT),
        ],
        out_specs=[
            pl.BlockSpec((ROW_TILE, POOL_WIDTH), lambda i: (i, 0)),
            pl.BlockSpec((ROW_TILE, MEM_WIDTH), lambda i: (i, 0)),
        ],
        out_shape=[
            jax.ShapeDtypeStruct((rows, POOL_WIDTH), jnp.float32),
            jax.ShapeDtypeStruct((rows, MEM_WIDTH), jnp.bfloat16),
        ],
        compiler_params=_params("arbitrary"),
        name="pool_in_proj",
    )(x2d, gain, w_stack)


def _mem_kv_kernel(x_ref, g_ref, w_ref, o_ref):
    hn = _rms_normalize(x_ref[...], g_ref[...])
    o_ref[...] = jnp.dot(hn, w_ref[...], preferred_element_type=jnp.float32).astype(jnp.bfloat16)


def _mem_kv_proj(mem2d, gain, w_stack, layer):
    rows = mem2d.shape[0]
    n = w_stack.shape[2]
    return pl.pallas_call(
        _mem_kv_kernel,
        grid=(1,),
        in_specs=[
            pl.BlockSpec((rows, D_MODEL), lambda i: (0, 0)),
            pl.BlockSpec((1, D_MODEL), lambda i: (0, 0)),
            pl.BlockSpec((None, D_MODEL, n), lambda i: (layer, 0, 0)),
        ],
        out_specs=pl.BlockSpec((rows, n), lambda i: (0, 0)),
        out_shape=jax.ShapeDtypeStruct((rows, n), jnp.bfloat16),
        compiler_params=_params("arbitrary"),
        name="mem_kv_proj",
    )(mem2d, gain, w_stack)


PIECE_LANES = 16


def _piece_selector():
    sel = np.zeros((LANES, FOX_HEADS * LANES), np.float32)
    for h in range(FOX_HEADS):
        for piece in range(3):
            sel[piece * PIECE_LANES + h, h * LANES + piece] = 1.0
            sel[piece * PIECE_LANES + h, h * LANES + 3 + piece] = -1.0
    return jnp.asarray(sel, jnp.bfloat16)


def _bf16_pieces(x):
    hi = x.astype(jnp.bfloat16).astype(jnp.float32)
    r1 = x - hi
    mid = r1.astype(jnp.bfloat16).astype(jnp.float32)
    lo = (r1 - mid).astype(jnp.bfloat16).astype(jnp.float32)
    return hi, mid, lo


def _fox_in_kernel(x_ref, g_ref, wqkv_ref, wqm_ref, wf_ref, b_ref, sel_ref, o_ref, qe_ref, ke_ref,
                   z_ref, carry_ref, *, tiles_per_batch):
    i = pl.program_id(0)

    @pl.when(i == 0)
    def _():
        z_ref[...] = jnp.zeros_like(z_ref)

    @pl.when((i == 0) | ((i - 1) % tiles_per_batch == 0))
    def _():
        carry_ref[...] = jnp.zeros_like(carry_ref)

    z = z_ref[...]
    log_f = jnp.minimum(z, 0.0) - jnp.log1p(jnp.exp(-jnp.abs(z)))
    lf_pieces = jnp.concatenate(_bf16_pieces(log_f), axis=1).astype(jnp.bfloat16)
    tc = CUMSUM_TILE
    row = lax.broadcasted_iota(jnp.int32, (tc, tc), 0)
    col = lax.broadcasted_iota(jnp.int32, (tc, tc), 1)
    tri = (col <= row).astype(jnp.bfloat16)
    lane = lax.broadcasted_iota(jnp.int32, (tc, LANES), 1)
    k_side = (lane >= 3) & (lane < 6)
    q_ones = jnp.where(k_side, 1.0, 0.0)
    k_ones = jnp.where(lane < 3, 1.0, 0.0)
    mid_lanes = (lane >= PIECE_LANES) & (lane < PIECE_LANES + FOX_HEADS)
    lo_lanes = (lane >= 2 * PIECE_LANES) & (lane < 2 * PIECE_LANES + FOX_HEADS)
    total = carry_ref[...]
    for r0 in range(0, ROW_TILE, tc):
        part = jnp.dot(tri, lf_pieces[r0:r0 + tc], preferred_element_type=jnp.float32)
        csum = (part[:, :LANES] + part[:, LANES:2 * LANES]) + part[:, 2 * LANES:] + total
        total = csum[tc - 1:tc, :]
        hi, mid, lo = _bf16_pieces(csum * LOG2E)
        packed = jnp.where(lane < FOX_HEADS, hi,
                           jnp.where(mid_lanes, pltpu.roll(mid, PIECE_LANES, 1),
                                     jnp.where(lo_lanes, pltpu.roll(lo, 2 * PIECE_LANES, 1), 0.0)))
        placed = jnp.dot(packed.astype(jnp.bfloat16), sel_ref[...],
                         preferred_element_type=jnp.float32)
        for h in range(FOX_HEADS):
            blk = placed[:, h * LANES:(h + 1) * LANES]
            qe_ref[h, r0:r0 + tc, :] = jnp.where(lane < 3, blk, q_ones).astype(jnp.bfloat16)
            ke_ref[h, r0:r0 + tc, :] = jnp.where(k_side, blk, k_ones).astype(jnp.bfloat16)
    carry_ref[...] = total

    hn = _rms_normalize(x_ref[...], g_ref[...])
    z_ref[...] = jnp.dot(hn, wf_ref[...], preferred_element_type=jnp.float32) + b_ref[...]
    for part, scale in enumerate((FOX_QSCALE, None, None)):
        cols = slice(part * FOX_WIDTH, (part + 1) * FOX_WIDTH)
        res = jnp.dot(hn, wqkv_ref[:, cols], preferred_element_type=jnp.float32)
        if scale is not None:
            res = res * scale
        o_ref[:, cols] = res.astype(jnp.bfloat16)
    o_ref[:, 3 * FOX_WIDTH:] = jnp.dot(hn, wqm_ref[...], preferred_element_type=jnp.float32).astype(jnp.bfloat16)


def _fox_in_proj(x2d, gain, w_qkv, w_qm, w_f, bias, batch):
    rows = x2d.shape[0]
    seq = rows // batch
    tiles_per_batch = seq // ROW_TILE
    last = rows // ROW_TILE - 1
    ext = jax.ShapeDtypeStruct((batch, FOX_HEADS, seq, LANES), jnp.bfloat16)

    def gate_tile(i):
        t = jnp.maximum(i - 1, 0)
        return (t // tiles_per_batch, 0, t % tiles_per_batch, 0)

    ext_spec = pl.BlockSpec((None, FOX_HEADS, ROW_TILE, LANES), gate_tile)
    return pl.pallas_call(
        functools.partial(_fox_in_kernel, tiles_per_batch=tiles_per_batch),
        grid=(last + 2,),
        in_specs=[
            pl.BlockSpec((ROW_TILE, D_MODEL), lambda i: (jnp.minimum(i, last), 0)),
            pl.BlockSpec((1, D_MODEL), lambda i: (0, 0)),
            pl.BlockSpec((D_MODEL, 3 * FOX_WIDTH), lambda i: (0, 0), pipeline_mode=_RESIDENT),
            pl.BlockSpec((D_MODEL, MEM_WIDTH), lambda i: (0, 0), pipeline_mode=_RESIDENT),
            pl.BlockSpec((D_MODEL, LANES), lambda i: (0, 0), pipeline_mode=_RESIDENT),
            pl.BlockSpec((1, LANES), lambda i: (0, 0)),
            pl.BlockSpec((LANES, FOX_HEADS * LANES), lambda i: (0, 0), pipeline_mode=_RESIDENT),
        ],
        out_specs=[pl.BlockSpec((ROW_TILE, FOX_PROJ_COLS), lambda i: (jnp.minimum(i, last), 0)),
                   ext_spec, ext_spec],
        out_shape=[jax.ShapeDtypeStruct((rows, FOX_PROJ_COLS), jnp.bfloat16), ext, ext],
        scratch_shapes=[pltpu.VMEM((ROW_TILE, LANES), jnp.float32), pltpu.VMEM((1, LANES), jnp.float32)],
        compiler_params=_params("arbitrary"),
        name="fox_in_proj",
    )(x2d, gain, w_qkv, w_qm, w_f, bias, _piece_selector())


def _pool_mix_kernel(u_ref, w_ref, s_ref, y_ref, ext_ref, lvl_ref):
    i = pl.program_id(1)
    tm = u_ref.shape[0]
    rows = tm + POOL_HALO

    @pl.when(i == 0)
    def _():
        ext_ref[0:POOL_HALO, :] = jnp.zeros((POOL_HALO, POOL_WIDTH), jnp.float32)

    @pl.when(i > 0)
    def _():
        ext_ref[0:POOL_HALO, :] = ext_ref[tm:tm + POOL_HALO, :]

    ext_ref[POOL_HALO:, :] = u_ref[...]

    t = i * tm + lax.broadcasted_iota(jnp.int32, (tm, 1), 0)
    for g, win in enumerate(POOL_WINDOWS):
        c0 = g * POOL_GROUP_WIDTH
        c1 = c0 + POOL_GROUP_WIDTH
        tok = ext_ref[POOL_HALO:, c0:c1]
        levels = win.bit_length() - 1
        wsum = None
        for lvl in range(levels):
            shift = 1 << lvl
            start = POOL_HALO - 8 * (levels - 1 - lvl)
            if lvl == 0:
                cur = ext_ref[start:, c0:c1] + ext_ref[start - shift:rows - shift, c0:c1]
            else:
                src = lvl_ref.at[(lvl - 1) % 2]
                cur = src[start:, :] + src[start - shift:rows - shift, :]
            if lvl + 1 < levels:
                lvl_ref[lvl % 2, start:, :] = cur
            else:
                wsum = cur
        count = jnp.minimum(t + 1, win).astype(jnp.float32)
        pooled = wsum / count - tok
        mixed = jnp.dot(pooled.astype(jnp.bfloat16), w_ref[g], preferred_element_type=jnp.float32)
        y_ref[:, c0:c1] = (mixed * s_ref[:, c0:c1]).astype(jnp.bfloat16)


def _pool_mix(u, w_grp_stack, scale):
    b, s, _ = u.shape
    assert all(w & (w - 1) == 0 for w in POOL_WINDOWS)
    assert POOL_HALO == 8 * (max(POOL_WINDOWS).bit_length() - 1)
    return pl.pallas_call(
        _pool_mix_kernel,
        grid=(b, s // ROW_TILE),
        in_specs=[
            pl.BlockSpec((None, ROW_TILE, POOL_WIDTH), lambda bi, i: (bi, i, 0)),
            pl.BlockSpec((None, POOL_GROUPS, POOL_GROUP_WIDTH, POOL_GROUP_WIDTH), lambda bi, i: (0, 0, 0, 0)),
            pl.BlockSpec((1, POOL_WIDTH), lambda bi, i: (0, 0)),
        ],
        out_specs=pl.BlockSpec((None, ROW_TILE, POOL_WIDTH), lambda bi, i: (bi, i, 0)),
        out_shape=jax.ShapeDtypeStruct((b, s, POOL_WIDTH), jnp.bfloat16),
        scratch_shapes=[pltpu.VMEM((ROW_TILE + POOL_HALO, POOL_WIDTH), jnp.float32),
                        pltpu.VMEM((2, ROW_TILE + POOL_HALO, POOL_GROUP_WIDTH), jnp.float32)],
        compiler_params=_params("arbitrary", "arbitrary"),
        name="pool_mix",
    )(u, w_grp_stack, scale)


def _fox_attn_kernel(q_ref, qe_ref, k_ref, ke_ref, v_ref, o_ref):
    i = pl.program_id(2)
    tq, tk = FOX_TQ, FOX_TK
    per_tile = tq // tk
    heads = range(FOX_PAIR)

    def cols(h):
        return slice(h * HEAD_DIM, (h + 1) * HEAD_DIM)

    q2 = [jnp.concatenate([q_ref[:, cols(h)], qe_ref[h]], axis=1) for h in heads]
    lane = lax.broadcasted_iota(jnp.int32, (tk, LANES), 1)
    v_ext = jnp.where(lane == 0, 1.0, 0.0).astype(jnp.bfloat16)

    def block(j, carry, row0=0, masked=False):
        return tuple(head_block(h, j, carry[h], row0, masked) for h in heads)

    def head_block(h, j, carry, row0, masked):
        m, acc = carry
        ks = pl.ds(pl.multiple_of(j * tk, tk), tk)
        k2 = jnp.concatenate([k_ref[ks, cols(h)], ke_ref[h, ks, :]], axis=1)
        v2 = jnp.concatenate([v_ref[ks, cols(h)], v_ext], axis=1)
        s = lax.dot_general(q2[h][row0:], k2, (((1,), (1,)), ((), ())), preferred_element_type=jnp.float32)
        if masked:
            row = lax.broadcasted_iota(jnp.int32, s.shape, 0)
            col = lax.broadcasted_iota(jnp.int32, s.shape, 1)
            s = jnp.where(col <= row, s, -jnp.inf)
        m_old = m[row0:]
        m_new = jnp.maximum(m_old, jnp.max(s, axis=-1, keepdims=True))
        alpha = jnp.exp2(m_old - m_new)
        p = jnp.exp2(s - m_new).astype(jnp.bfloat16)
        acc_new = alpha * acc[row0:] + jnp.dot(p, v2, preferred_element_type=jnp.float32)
        if row0:
            m_new = jnp.concatenate([m[:row0], m_new], axis=0)
            acc_new = jnp.concatenate([acc[:row0], acc_new], axis=0)
        return m_new, acc_new

    carry = tuple((jnp.full((tq, 1), NEG_BIG, jnp.float32),
                   jnp.zeros((tq, 2 * HEAD_DIM), jnp.float32)) for _ in heads)
    n_full = i * per_tile

    done = 0
    for n in FOX_UNROLLS:
        def body(t, c, n=n, base=done):
            for u in range(n):
                c = block(base + t * n + u, c)
            return c
        trips = (n_full - done) // n
        carry = lax.fori_loop(0, trips, body, carry)
        done = done + trips * n

    def tail(c, n_rest):
        for u in range(n_rest):
            c = block(done + u, c)
        for d in range(per_tile):
            c = block(i * per_tile + d, c, row0=d * tk, masked=True)
        for h in heads:
            _, acc = c[h]
            o_ref[:, cols(h)] = (acc[:, :HEAD_DIM] / acc[:, HEAD_DIM:HEAD_DIM + 1]).astype(jnp.bfloat16)

    rest = n_full - done
    for n_rest in range(0, FOX_UNROLLS[-1], per_tile):
        pl.when(rest == n_rest)(functools.partial(tail, carry, n_rest))


def _fox_attention(proj, qe, ke):
    b, s, _ = proj.shape
    assert FOX_TQ % FOX_TK == 0 and all(n % (FOX_TQ // FOX_TK) == 0 for n in FOX_UNROLLS)
    groups, width = FOX_HEADS // FOX_PAIR, FOX_PAIR * HEAD_DIM
    return pl.pallas_call(
        _fox_attn_kernel,
        grid=(b, groups, s // FOX_TQ),
        in_specs=[
            pl.BlockSpec((None, FOX_TQ, width), lambda bi, g, i: (bi, i, g)),
            pl.BlockSpec((None, FOX_PAIR, FOX_TQ, LANES), lambda bi, g, i: (bi, g, i, 0)),
            pl.BlockSpec((None, s, width), lambda bi, g, i: (bi, 0, groups + g), pipeline_mode=_RESIDENT),
            pl.BlockSpec((None, FOX_PAIR, s, LANES), lambda bi, g, i: (bi, g, 0, 0), pipeline_mode=_RESIDENT),
            pl.BlockSpec((None, s, width), lambda bi, g, i: (bi, 0, 2 * groups + g), pipeline_mode=_RESIDENT),
        ],
        out_specs=pl.BlockSpec((None, FOX_TQ, width), lambda bi, g, i: (bi, i, g)),
        out_shape=jax.ShapeDtypeStruct((b, s, FOX_WIDTH), jnp.bfloat16),
        compiler_params=_params("arbitrary", "arbitrary", "arbitrary"),
        name="fox_attention",
    )(proj, qe, proj, ke, proj)


def _mem_attention(q_ref, k_ref, v_ref):
    inv_sqrt = 1.0 / math.sqrt(HEAD_DIM)
    heads = []
    for h in range(MEM_HEADS):
        c0 = h * HEAD_DIM
        c1 = c0 + HEAD_DIM
        logits = lax.dot_general(q_ref[:, c0:c1], k_ref[:, c0:c1], (((1,), (1,)), ((), ())),
                                 preferred_element_type=jnp.float32)
        logits = logits * inv_sqrt
        m = jnp.max(logits, axis=-1, keepdims=True)
        e = jnp.exp(logits - m)
        p = e / jnp.sum(e, axis=-1, keepdims=True)
        out = jnp.dot(p.astype(jnp.bfloat16), v_ref[:, c0:c1], preferred_element_type=jnp.float32)
        heads.append(out.astype(jnp.bfloat16))
    return jnp.concatenate(heads, axis=1)


def _out_proj_kernel(x_ref, ya_ref, q_ref, k_ref, v_ref, wa_ref, wb_ref, g_ref, o_ref, hn_ref):
    acc = jnp.dot(ya_ref[...], wa_ref[...], preferred_element_type=jnp.float32)
    y_mem = _mem_attention(q_ref, k_ref, v_ref)
    acc = acc + jnp.dot(y_mem, wb_ref[...], preferred_element_type=jnp.float32)
    x_new = x_ref[...] + acc
    o_ref[...] = x_new
    hn_ref[...] = _rms_normalize(x_new, g_ref[...])


def _out_proj(x2d, y_mix, q_arr, q_col_block, kv, w_stack, layer, ffn_gain):
    rows = x2d.shape[0]
    tiles_per_batch = rows // kv.shape[0] // ROW_TILE
    mem_row_block = POOL_WIDTH // MEM_WIDTH
    return pl.pallas_call(
        _out_proj_kernel,
        grid=(rows // ROW_TILE,),
        in_specs=[
            pl.BlockSpec((ROW_TILE, D_MODEL), lambda i: (i, 0)),
            pl.BlockSpec((ROW_TILE, POOL_WIDTH), lambda i: (i, 0)),
            pl.BlockSpec((ROW_TILE, MEM_WIDTH), lambda i: (i, q_col_block)),
            pl.BlockSpec((None, MEM_LEN, MEM_WIDTH), lambda i: (i // tiles_per_batch, 0, 0)),
            pl.BlockSpec((None, MEM_LEN, MEM_WIDTH), lambda i: (i // tiles_per_batch, 0, 1)),
            pl.BlockSpec((None, POOL_WIDTH, D_MODEL), lambda i: (layer, 0, 0), pipeline_mode=_RESIDENT),
            pl.BlockSpec((None, MEM_WIDTH, D_MODEL), lambda i: (layer, mem_row_block, 0),
                         pipeline_mode=_RESIDENT),
            pl.BlockSpec((1, D_MODEL), lambda i: (0, 0)),
        ],
        out_specs=[pl.BlockSpec((ROW_TILE, D_MODEL), lambda i: (i, 0)),
                   pl.BlockSpec((ROW_TILE, D_MODEL), lambda i: (i, 0))],
        out_shape=[jax.ShapeDtypeStruct((rows, D_MODEL), jnp.float32),
                   jax.ShapeDtypeStruct((rows, D_MODEL), jnp.bfloat16)],
        compiler_params=_params("arbitrary"),
        name="out_proj",
    )(x2d, y_mix, q_arr, kv, kv, w_stack, w_stack, ffn_gain)


def _ffn_kernel(x_ref, hn_ref, w1_ref, w2_ref, g_ref, o_ref, *, final_norm):
    @pl.when(pl.program_id(1) == 0)
    def _():
        o_ref[...] = x_ref[...]

    a = jnp.dot(hn_ref[...], w1_ref[...], preferred_element_type=jnp.float32)
    a = jnp.square(jnp.maximum(a, 0.0)).astype(jnp.bfloat16)
    o_ref[...] += jnp.dot(a, w2_ref[...], preferred_element_type=jnp.float32)

    if final_norm:
        @pl.when(pl.program_id(1) == pl.num_programs(1) - 1)
        def _():
            o_ref[...] = _rms_scale(o_ref[...], g_ref[...])


def _ffn(x2d, hn2d, w1_stack, w2_stack, layer, final_gain, final_norm):
    rows = x2d.shape[0]
    return pl.pallas_call(
        functools.partial(_ffn_kernel, final_norm=final_norm),
        grid=(rows // ROW_TILE, FFN_HIDDEN // FFN_TILE),
        in_specs=[
            pl.BlockSpec((ROW_TILE, D_MODEL), lambda i, j: (i, 0)),
            pl.BlockSpec((ROW_TILE, D_MODEL), lambda i, j: (i, 0)),
            pl.BlockSpec((None, D_MODEL, FFN_TILE), lambda i, j: (layer, 0, j)),
            pl.BlockSpec((None, FFN_TILE, D_MODEL), lambda i, j: (layer, j, 0)),
            pl.BlockSpec((1, D_MODEL), lambda i, j: (0, 0)),
        ],
        out_specs=pl.BlockSpec((ROW_TILE, D_MODEL), lambda i, j: (i, 0)),
        out_shape=jax.ShapeDtypeStruct((rows, D_MODEL), jnp.float32),
        compiler_params=pltpu.CompilerParams(dimension_semantics=("arbitrary", "arbitrary"),
                                             vmem_limit_bytes=FFN_VMEM_LIMIT),
        name="ffn_final" if final_norm else "ffn",
    )(x2d, hn2d, w1_stack, w2_stack, final_gain)


def kernel(x, mem, norm_mix, norm_mem, pool_w_in, pool_w_grp, pool_scale, fox_w_in, fox_b_f,
           w_mem_kv, w_out, norm_ffn, w_ffn1, w_ffn2, norm_final):
    b, s, d = x.shape
    rows = b * s
    depth = w_out.shape[0]
    bf16 = jnp.bfloat16
    f32 = jnp.float32
    x2d = x.reshape(rows, d)
    mem2d = mem.reshape(b * MEM_LEN, d)
    w_mem_kv_b, w_out_b = w_mem_kv.astype(bf16), w_out.astype(bf16)
    w_ffn1_b, w_ffn2_b = w_ffn1.astype(bf16), w_ffn2.astype(bf16)

    def row(v):
        return v.reshape(1, -1).astype(f32)

    def mix_tail(x2d, layer, y_mix, q_arr, q_col_block):
        kv = _mem_kv_proj(mem2d, row(norm_mem[layer]), w_mem_kv_b, layer).reshape(b, MEM_LEN, 2 * MEM_WIDTH)
        x2d, hn2d = _out_proj(x2d, y_mix.reshape(rows, POOL_WIDTH), q_arr.reshape(rows, -1), q_col_block,
                              kv, w_out_b, layer, row(norm_ffn[layer]))
        return _ffn(x2d, hn2d, w_ffn1_b, w_ffn2_b, layer, row(norm_final), layer == depth - 1)

    u, q_mem = _pool_in_proj(x2d, row(norm_mix[0]), pool_w_in.astype(bf16))
    y_mix = _pool_mix(u.reshape(b, s, POOL_WIDTH), pool_w_grp.astype(bf16), row(pool_scale[0]))
    x2d = mix_tail(x2d, 0, y_mix, q_mem.reshape(b, s, MEM_WIDTH), 0)

    w_in = fox_w_in[0]
    n_qkv = 3 * FOX_WIDTH
    w_qkv = w_in[:, :n_qkv].astype(bf16)
    w_qm = w_in[:, n_qkv + FOX_HEADS:].astype(bf16)
    w_f = jnp.pad(w_in[:, n_qkv:n_qkv + FOX_HEADS], ((0, 0), (0, LANES - FOX_HEADS))).astype(bf16)
    bias_f = jnp.pad(fox_b_f[0].astype(f32), (0, LANES - FOX_HEADS)).reshape(1, LANES)
    proj, qe, ke = _fox_in_proj(x2d, row(norm_mix[1]), w_qkv, w_qm, w_f, bias_f, b)
    proj = proj.reshape(b, s, FOX_PROJ_COLS)
    y_mix = _fox_attention(proj, qe, ke)
    x2d = mix_tail(x2d, 1, y_mix, proj, FOX_PROJ_COLS // MEM_WIDTH - 1)

    return x2d.reshape(b, s, d)
```
